```python
import jax, jax.numpy as jnp
from jax import lax
import numpy as np

D_MODEL = 2048
BATCH = 2
SEQ = 8192
DEPTH = 1
DEC_BATCH = 32
DEC_SEQ = 64
PAST_LEN = 4096

CHUNK = 64
Q_BLOCK = 128
SB_HEADS = 16
SB_HEAD_DIM = 64
SB_WIDTH = SB_HEADS * SB_HEAD_DIM
RW_HEADS = 16
RW_HEAD_DIM = 64
RW_WIDTH = RW_HEADS * RW_HEAD_DIM
DECAY_LORA = 96
ICLR_LORA = 96
GATE_LORA = 256
RW_COLS = 3 * RW_WIDTH + DECAY_LORA + ICLR_LORA + GATE_LORA
IN_COLS = 3 * SB_WIDTH + RW_COLS + 2 * D_MODEL
D_FF = 5632
NORM_EPS = 1e-6
GN_EPS = 64e-5

P_SPLITS = [SB_WIDTH, 2 * SB_WIDTH, 3 * SB_WIDTH, 3 * SB_WIDTH + RW_COLS, 3 * SB_WIDTH + RW_COLS + D_MODEL]
RW_SPLITS = [RW_WIDTH, 2 * RW_WIDTH, 3 * RW_WIDTH, 3 * RW_WIDTH + DECAY_LORA, 3 * RW_WIDTH + DECAY_LORA + ICLR_LORA]

kernel_name = 'stickbreak_rwkv7_macaron_stream_step'


def rms_norm(x, g, eps=NORM_EPS):
    xf = x.astype(jnp.float32)
    y = xf * lax.rsqrt(jnp.mean(xf * xf, axis=-1, keepdims=True) + eps)
    return (y * g.astype(jnp.float32)).astype(x.dtype)


def swiglu(x, w_gate, w_up, w_down):
    return (jax.nn.silu(x @ w_gate) * (x @ w_up)) @ w_down


def sb_block(q, q_pos, k, v, k_pos):
    f32 = jnp.float32
    z = jnp.einsum('qhd,khd->hqk', q.astype(f32), k.astype(f32)) * (SB_HEAD_DIM ** -0.5)
    mask = (k_pos[None, :] < q_pos[:, None])[None]
    log_beta = jax.nn.log_sigmoid(z)
    log_keep = jnp.where(mask, jax.nn.log_sigmoid(-z), 0.0)
    tail = lax.cumsum(log_keep, axis=2, reverse=True) - log_keep
    w = jnp.where(mask, jnp.exp(log_beta + tail), 0.0)
    return jnp.einsum('hqk,khd->qhd', w, v.astype(f32)).astype(v.dtype)


def stick_breaking(q, k, v, past_k, past_v):
    B, T, H, Dh = q.shape
    if past_k is None:
        keys, vals, P = k, v, 0
    else:
        P = past_k.shape[1]
        keys = jnp.concatenate([past_k.astype(k.dtype), k], axis=1)
        vals = jnp.concatenate([past_v.astype(v.dtype), v], axis=1)
    k_pos = jnp.arange(P + T)
    q_pos = P + jnp.arange(T)
    qb = min(T, Q_BLOCK)
    nb = T // qb
    pos_blk = q_pos.reshape(nb, qb)

    def per_stream(args):
        q_s, k_s, v_s = args
        q_blk = q_s.reshape(nb, qb, H, Dh)
        o = lax.map(lambda a: sb_block(a[0], a[1], k_s, v_s, k_pos), (q_blk, pos_blk))
        return o.reshape(T, H, Dh)

    return lax.map(per_stream, (q, keys, vals))


def rwkv_scan(r, decay, k, v, a_vec, b_vec, s0):
    xs = tuple(jnp.moveaxis(t.astype(jnp.float32), 1, 0) for t in (r, decay, k, v, a_vec, b_vec))

    def step(S, inp):
        r_t, w_t, k_t, v_t, a_t, b_t = inp
        sa = jnp.einsum('bhij,bhj->bhi', S, a_t)
        S = S * w_t[:, :, None, :] + sa[..., None] * b_t[:, :, None, :] + v_t[..., None] * k_t[:, :, None, :]
        return S, jnp.einsum('bhij,bhj->bhi', S, r_t)

    S, ys = lax.scan(step, s0.astype(jnp.float32), xs)
    return jnp.moveaxis(ys, 0, 1), S


def rwkv7_mixer(p, prev_row, wkv0, lw):
    B, T, _ = p.shape
    H, N = RW_HEADS, RW_HEAD_DIM
    f32 = jnp.float32
    pf = p.astype(f32)
    shifted = jnp.concatenate([prev_row.astype(f32), pf[:, :-1]], axis=1)
    xm = pf + (shifted - pf) * lw['rwkv_mu'].astype(f32)
    r, k, v, wl, al, gl = jnp.split(xm, RW_SPLITS, axis=-1)
    w_log = -jax.nn.softplus(-(lw['rwkv_w0'] + jnp.tanh(wl) @ lw['rwkv_w_w2'])) - 0.5
    decay = jnp.exp(-jnp.exp(w_log))
    a = jax.nn.sigmoid(lw['rwkv_a0'] + al @ lw['rwkv_w_a2'])
    g = jax.nn.sigmoid(gl) @ lw['rwkv_w_g2']
    kk = k * lw['rwkv_k_k']
    k = k * (1.0 + (a - 1.0) * lw['rwkv_k_a'])
    hd = lambda t: t.reshape(B, T, H, N)
    r, k, v, kk, a, decay = hd(r), hd(k), hd(v), hd(kk), hd(a), hd(decay)
    kk = kk / jnp.maximum(jnp.linalg.norm(kk, axis=-1, keepdims=True), 1e-12)
    y, S = rwkv_scan(r, decay, k, v, -kk, kk * a, wkv0)
    mu = jnp.mean(y, axis=-1, keepdims=True)
    var = jnp.mean(jnp.square(y - mu), axis=-1, keepdims=True)
    y = ((y - mu) * lax.rsqrt(var + GN_EPS)).reshape(B, T, RW_WIDTH) * lw['rwkv_ln_w'] + lw['rwkv_ln_b']
    bonus = jnp.sum(r * k * lw['rwkv_r_k'], axis=-1, keepdims=True) * v
    y = (y + bonus.reshape(B, T, RW_WIDTH)) * g
    return y.astype(p.dtype) @ lw['rwkv_w_o'], S, p[:, -1:]


def trunk_layer(x, past_k, past_v, wkv0, shift0, lw):
    B, T, _ = x.shape
    h = x + 0.5 * swiglu(rms_norm(x, lw['ffn1_norm']), lw['ffn1_w_gate'], lw['ffn1_w_up'], lw['ffn1_w_down'])
    n = rms_norm(h, lw['mix_norm'])
    p = n @ lw['w_in']
    q, k, v, p_rw, gate_sb, gate_rw = jnp.split(p, P_SPLITS, axis=-1)
    q = rms_norm(q.reshape(B, T, SB_HEADS, SB_HEAD_DIM), lw['sb_q_norm'])
    k = rms_norm(k.reshape(B, T, SB_HEADS, SB_HEAD_DIM), lw['sb_k_norm'])
    v = v.reshape(B, T, SB_HEADS, SB_HEAD_DIM)
    o_sb = stick_breaking(q, k, v, past_k, past_v).reshape(B, T, SB_WIDTH) @ lw['sb_w_o']
    o_rw, wkv, shift = rwkv7_mixer(p_rw, shift0, wkv0, lw)
    merged = jax.nn.sigmoid(gate_sb) * o_sb + jax.nn.sigmoid(gate_rw) * o_rw
    h = h + merged @ lw['w_out']
    h = h + 0.5 * swiglu(rms_norm(h, lw['ffn2_norm']), lw['ffn2_w_gate'], lw['ffn2_w_up'], lw['ffn2_w_down'])
    return h, k, v, wkv.astype(x.dtype), shift


def setup_inputs(seed: int = 0) -> dict:
    key = jax.random.key(seed)
    ks = iter(jax.random.split(key, 48))

    def nrm(shape, scale):
        return jax.random.normal(next(ks), shape, jnp.float32) * scale

    def dense(fan_in, fan_out, gain=1.0):
        return nrm((DEPTH, fan_in, fan_out), gain * fan_in ** -0.5)

    def gain_vec(n):
        return 1.0 + nrm((DEPTH, n), 0.05)

    return {
        'x_prompt': nrm((BATCH, SEQ, D_MODEL), 1.0),
        'x_sample': nrm((DEC_BATCH, DEC_SEQ, D_MODEL), 1.0),
        'cache_sb_k': nrm((DEPTH, DEC_BATCH, PAST_LEN, SB_HEADS, SB_HEAD_DIM), 1.0),
        'cache_sb_v': nrm((DEPTH, DEC_BATCH, PAST_LEN, SB_HEADS, SB_HEAD_DIM), 1.0),
        'state_rwkv_wkv': nrm((DEPTH, DEC_BATCH, RW_HEADS, RW_HEAD_DIM, RW_HEAD_DIM), 0.1),
        'state_rwkv_shift': nrm((DEPTH, DEC_BATCH, 1, RW_COLS), 1.0),
        'ffn1_norm': gain_vec(D_MODEL),
        'ffn1_w_gate': dense(D_MODEL, D_FF),
        'ffn1_w_up': dense(D_MODEL, D_FF),
        'ffn1_w_down': dense(D_FF, D_MODEL),
        'mix_norm': gain_vec(D_MODEL),
        'w_in': dense(D_MODEL, IN_COLS),
        'sb_q_norm': gain_vec(SB_HEAD_DIM),
        'sb_k_norm': gain_vec(SB_HEAD_DIM),
        'sb_w_o': dense(SB_WIDTH, D_MODEL),
        'rwkv_mu': jax.random.uniform(next(ks), (DEPTH, RW_COLS), jnp.float32),
        'rwkv_w0': nrm((DEPTH, RW_WIDTH), 0.5),
        'rwkv_w_w2': dense(DECAY_LORA, RW_WIDTH, 0.5),
        'rwkv_a0': nrm((DEPTH, RW_WIDTH), 0.1),
        'rwkv_w_a2': dense(ICLR_LORA, RW_WIDTH, 0.5),
        'rwkv_w_g2': dense(GATE_LORA, RW_WIDTH),
        'rwkv_k_k': 0.85 + nrm((DEPTH, RW_WIDTH), 0.05),
        'rwkv_k_a': gain_vec(RW_WIDTH),
        'rwkv_r_k': nrm((DEPTH, RW_HEADS, RW_HEAD_DIM), 0.1),
        'rwkv_ln_w': gain_vec(RW_WIDTH),
        'rwkv_ln_b': nrm((DEPTH, RW_WIDTH), 0.02),
        'rwkv_w_o': dense(RW_WIDTH, D_MODEL),
        'w_out': dense(D_MODEL, D_MODEL),
        'ffn2_norm': gain_vec(D_MODEL),
        'ffn2_w_gate': dense(D_MODEL, D_FF),
        'ffn2_w_up': dense(D_MODEL, D_FF),
        'ffn2_w_down': dense(D_FF, D_MODEL),
    }


def reference(x_prompt, x_sample, cache_sb_k, cache_sb_v, state_rwkv_wkv, state_rwkv_shift,
              ffn1_norm, ffn1_w_gate, ffn1_w_up, ffn1_w_down, mix_norm, w_in,
              sb_q_norm, sb_k_norm, sb_w_o,
              rwkv_mu, rwkv_w0, rwkv_w_w2, rwkv_a0, rwkv_w_a2, rwkv_w_g2, rwkv_k_k, rwkv_k_a,
              rwkv_r_k, rwkv_ln_w, rwkv_ln_b, rwkv_w_o, w_out,
              ffn2_norm, ffn2_w_gate, ffn2_w_up, ffn2_w_down):
    yp, ys = x_prompt, x_sample
    bp = x_prompt.shape[0]
    kp_l, vp_l, wkvp_l, shp_l = [], [], [], []
    ks_l, vs_l, wkvs_l, shs_l = [], [], [], []
    for l in range(DEPTH):
        lw = {
            'ffn1_norm': ffn1_norm[l], 'ffn1_w_gate': ffn1_w_gate[l], 'ffn1_w_up': ffn1_w_up[l], 'ffn1_w_down': ffn1_w_down[l],
            'mix_norm': mix_norm[l], 'w_in': w_in[l],
            'sb_q_norm': sb_q_norm[l], 'sb_k_norm': sb_k_norm[l], 'sb_w_o': sb_w_o[l],
            'rwkv_mu': rwkv_mu[l], 'rwkv_w0': rwkv_w0[l], 'rwkv_w_w2': rwkv_w_w2[l], 'rwkv_a0': rwkv_a0[l],
            'rwkv_w_a2': rwkv_w_a2[l], 'rwkv_w_g2': rwkv_w_g2[l], 'rwkv_k_k': rwkv_k_k[l], 'rwkv_k_a': rwkv_k_a[l],
            'rwkv_r_k': rwkv_r_k[l], 'rwkv_ln_w': rwkv_ln_w[l], 'rwkv_ln_b': rwkv_ln_b[l], 'rwkv_w_o': rwkv_w_o[l],
            'w_out': w_out[l],
            'ffn2_norm': ffn2_norm[l], 'ffn2_w_gate': ffn2_w_gate[l], 'ffn2_w_up': ffn2_w_up[l], 'ffn2_w_down': ffn2_w_down[l],
        }
        wkv_zero = jnp.zeros((bp, RW_HEADS, RW_HEAD_DIM, RW_HEAD_DIM), jnp.float32)
        shift_zero = jnp.zeros((bp, 1, RW_COLS), yp.dtype)
        yp, kp, vp, wkvp, shp = trunk_layer(yp, None, None, wkv_zero, shift_zero, lw)
        ys, kn, vn, wkvn, shn = trunk_layer(ys, cache_sb_k[l], cache_sb_v[l], state_rwkv_wkv[l], state_rwkv_shift[l], lw)
        kp_l.append(kp); vp_l.append(vp); wkvp_l.append(wkvp); shp_l.append(shp)
        ks_l.append(kn); vs_l.append(vn); wkvs_l.append(wkvn); shs_l.append(shn)
    new_sb_k_prompt = jnp.stack(kp_l)
    new_sb_v_prompt = jnp.stack(vp_l)
    new_wkv_prompt = jnp.stack(wkvp_l)
    new_shift_prompt = jnp.stack(shp_l)
    new_sb_k_sample = jnp.stack(ks_l)
    new_sb_v_sample = jnp.stack(vs_l)
    new_wkv_sample = jnp.stack(wkvs_l)
    new_shift_sample = jnp.stack(shs_l)
    return (yp, ys, new_sb_k_prompt, new_sb_v_prompt, new_wkv_prompt, new_shift_prompt,
            new_sb_k_sample, new_sb_v_sample, new_wkv_sample, new_shift_sample)
```

```python
import functools

import jax
import jax.numpy as jnp
from jax import lax
from jax.experimental import pallas as pl
from jax.experimental.pallas import tpu as pltpu

F32 = jnp.float32
BF16 = jnp.bfloat16

HEAD_DIM = 64
LANES = 128
NORM_EPS = 1e-6
GN_EPS = 64e-5
RW_CHUNK = 64
ATT_TK = 128

MIB = 1024 * 1024


def _nt(x, y):
    return lax.dot_general(x, y, (((1,), (1,)), ((), ())), preferred_element_type=F32)


def _tn(x, y):
    return lax.dot_general(x, y, (((0,), (0,)), ((), ())), preferred_element_type=F32)


def _mm(x, y):
    return jnp.dot(x, y, preferred_element_type=F32)


def _split2(x):
    hi = x.astype(BF16)
    lo = (x - hi.astype(F32)).astype(BF16)
    return hi, lo


def _head_sum(x, bd):
    hi, lo = _split2(x)
    return _mm(hi, bd) + _mm(lo, bd)


def _rms(x, g):
    ms = jnp.mean(x * x, axis=-1, keepdims=True)
    return x * lax.rsqrt(ms + NORM_EPS) * g


def _params(sem, vmem_mib):
    return pltpu.CompilerParams(dimension_semantics=sem, vmem_limit_bytes=vmem_mib * MIB)


def _ffn_body(x_ref, g_ref, wg_ref, wu_ref, wd_ref, o_ref, n_ref, acc_ref):
    f = pl.program_id(1)

    @pl.when(f == 0)
    def _():
        n_ref[...] = _rms(x_ref[...], g_ref[...]).astype(BF16)
        acc_ref[...] = jnp.zeros_like(acc_ref)

    n = n_ref[...]
    g = _mm(n, wg_ref[...])
    u = _mm(n, wu_ref[...])
    a = (g * jax.nn.sigmoid(g) * u).astype(BF16)
    acc_ref[...] += _mm(a, wd_ref[...])

    @pl.when(f == pl.num_programs(1) - 1)
    def _():
        o_ref[...] = x_ref[...] + 0.5 * acc_ref[...]


def _ffn(x, g, wg, wu, wd, *, tm=512, tf=512):
    n, d = x.shape
    ff = wg.shape[1]
    tm, tf = min(tm, n), min(tf, ff)
    return pl.pallas_call(
        _ffn_body,
        out_shape=jax.ShapeDtypeStruct((n, d), F32),
        grid=(n // tm, ff // tf),
        in_specs=[
            pl.BlockSpec((tm, d), lambda i, f: (i, 0)),
            pl.BlockSpec((1, d), lambda i, f: (0, 0)),
            pl.BlockSpec((d, tf), lambda i, f: (0, f)),
            pl.BlockSpec((d, tf), lambda i, f: (0, f)),
            pl.BlockSpec((tf, d), lambda i, f: (f, 0)),
        ],
        out_specs=pl.BlockSpec((tm, d), lambda i, f: (i, 0)),
        scratch_shapes=[pltpu.VMEM((tm, d), BF16), pltpu.VMEM((tm, d), F32)],
        compiler_params=_params(("parallel", "arbitrary"), 48),
        name="ffn",
    )(x, g, wg, wu, wd)


def _mix_body(h_ref, g_ref, w_ref, hg_ref, bd_ref, o_ref, n_ref, *, n_norm_tiles):
    j = pl.program_id(1)

    @pl.when(j == 0)
    def _():
        n_ref[...] = _rms(h_ref[...], g_ref[...]).astype(BF16)

    p = _mm(n_ref[...], w_ref[...])

    @pl.when(j < n_norm_tiles)
    def _():
        ms = _head_sum(p * p, bd_ref[...]) * (1.0 / HEAD_DIM)
        o_ref[...] = p * lax.rsqrt(ms + NORM_EPS) * hg_ref[...]

    @pl.when(j >= n_norm_tiles)
    def _():
        o_ref[...] = p


def _mix(h, g, w, hgain, bd, *, n_norm_cols, tm=1024, tn=512):
    n, d = h.shape
    cols = w.shape[1]
    tm = min(tm, n)
    return pl.pallas_call(
        functools.partial(_mix_body, n_norm_tiles=n_norm_cols // tn),
        out_shape=jax.ShapeDtypeStruct((n, cols), F32),
        grid=(n // tm, cols // tn),
        in_specs=[
            pl.BlockSpec((tm, d), lambda i, j: (i, 0)),
            pl.BlockSpec((1, d), lambda i, j: (0, 0)),
            pl.BlockSpec((d, tn), lambda i, j: (0, j)),
            pl.BlockSpec((1, tn), lambda i, j: (0, j)),
            pl.BlockSpec((tn, tn), lambda i, j: (0, 0)),
        ],
        out_specs=pl.BlockSpec((tm, tn), lambda i, j: (i, j)),
        scratch_shapes=[pltpu.VMEM((tm, d), BF16)],
        compiler_params=_params(("parallel", "arbitrary"), 48),
        name="mix",
    )(h, g, w, hgain, bd)


def _attn_body(*refs, tq, tk, t_new, t_pad, n_past_blocks):
    if n_past_blocks:
        q_ref, k_ref, v_ref, pk_ref, pv_ref, u2_ref, o_ref, kb, vb, carry, acc = refs
    else:
        q_ref, k_ref, v_ref, u2_ref, o_ref, kb, vb, carry, acc = refs
        pk_ref = pv_ref = None
    i = pl.program_id(2)

    @pl.when(i == 0)
    def _():
        if t_pad > t_new:
            kb[...] = jnp.zeros_like(kb)
            vb[...] = jnp.zeros_like(vb)
        kb[0:t_new, :] = k_ref[0].astype(BF16)
        vb[0:t_new, :] = v_ref[0].astype(BF16)

    q = q_ref[0] * (HEAD_DIM ** -0.5)
    lane = lax.broadcasted_iota(jnp.int32, (tq, LANES), 1)
    first = lane < HEAD_DIM
    zero = jnp.zeros_like(q)
    qh = (jnp.where(first, q, zero).astype(BF16), jnp.where(first, zero, q).astype(BF16))
    u2 = u2_ref[...]
    carry[...] = jnp.zeros_like(carry)
    acc[...] = jnp.zeros_like(acc)

    def block(kblk, vblk, mask):
        for h in range(2):
            z = _nt(qh[h], kblk)
            lp = jnp.log(1.0 + jnp.exp(-jnp.abs(z)))
            log_beta = jnp.minimum(z, 0.0) - lp
            log_keep = log_beta - z
            if mask is not None:
                log_keep = jnp.where(mask, log_keep, 0.0)
            hi, lo = _split2(log_keep)
            res = _mm(hi, u2) + _mm(lo, u2)
            tail = res[:, :tk] + carry[h]
            w = jnp.exp(log_beta + tail)
            if mask is not None:
                w = jnp.where(mask, w, 0.0)
            acc[h] += _mm(w.astype(BF16), vblk)
            carry[h] += res[:, tk:]

    q0 = i * tq
    row = lax.broadcasted_iota(jnp.int32, (tq, tk), 0)
    col = lax.broadcasted_iota(jnp.int32, (tq, tk), 1)
    n_diag = max(tq // tk, 1)
    for m in reversed(range(n_diag)):
        k0 = pl.multiple_of(q0 + m * tk, tk)
        mask = (col + m * tk) < row
        block(kb[pl.ds(k0, tk), :], vb[pl.ds(k0, tk), :], mask)

    def new_step(it, _):
        k0 = pl.multiple_of(q0 - (it + 1) * tk, tk)
        block(kb[pl.ds(k0, tk), :], vb[pl.ds(k0, tk), :], None)
        return 0

    lax.fori_loop(0, q0 // tk, new_step, 0)

    if n_past_blocks:
        def past_step(it, _):
            k0 = pl.multiple_of((n_past_blocks - 1 - it) * tk, tk)
            block(pk_ref[0, pl.ds(k0, tk), :].astype(BF16), pv_ref[0, pl.ds(k0, tk), :].astype(BF16), None)
            return 0

        lax.fori_loop(0, n_past_blocks, past_step, 0)

    o_ref[0] = jnp.where(first, acc[0], acc[1])


def _attention(p3, past_k, past_v, u2, *, n_pairs, q_blk0, k_blk0, v_blk0, tq):
    b, t, _ = p3.shape
    tk = ATT_TK
    tq = min(tq, t)
    t_pad = max(t, tk)
    has_past = past_k is not None
    n_past_blocks = past_k.shape[1] // tk if has_past else 0
    in_specs = [
        pl.BlockSpec((1, tq, LANES), lambda bi, p, i: (bi, i, q_blk0 + p)),
        pl.BlockSpec((1, t, LANES), lambda bi, p, i: (bi, 0, k_blk0 + p)),
        pl.BlockSpec((1, t, LANES), lambda bi, p, i: (bi, 0, v_blk0 + p)),
    ]
    args = [p3, p3, p3]
    if has_past:
        pp = past_k.shape[1]
        in_specs += [pl.BlockSpec((1, pp, LANES), lambda bi, p, i: (bi, 0, p))] * 2
        args += [past_k, past_v]
    in_specs.append(pl.BlockSpec(u2.shape, lambda bi, p, i: (0, 0)))
    args.append(u2)
    return pl.pallas_call(
        functools.partial(_attn_body, tq=tq, tk=tk, t_new=t, t_pad=t_pad, n_past_blocks=n_past_blocks),
        out_shape=jax.ShapeDtypeStruct((b, t, n_pairs * LANES), F32),
        grid=(b, n_pairs, t // tq),
        in_specs=in_specs,
        out_specs=pl.BlockSpec((1, tq, LANES), lambda bi, p, i: (bi, i, p)),
        scratch_shapes=[
            pltpu.VMEM((t_pad, LANES), BF16),
            pltpu.VMEM((t_pad, LANES), BF16),
            pltpu.VMEM((2, tq, LANES), F32),
            pltpu.VMEM((2, tq, LANES), F32),
        ],
        compiler_params=_params(("parallel", "parallel", "arbitrary"), 48),
        name="sb_attention",
    )(*args)


def _prep_body(prkv_ref, plora_ref, s_rkv_ref, s_lora_ref, mu_rkv_ref, mu_lora_ref, w0_ref, a0_ref, kk_ref, ka_ref,
               ww2_ref, wa2_ref, wg2_ref, bd_ref,
               r_o, wl_o, k_o, v_o, av_o, bv_o, g_o, c_rkv, c_lora, *, tc, width):
    t = pl.program_id(1)

    @pl.when(t == 0)
    def _():
        c_rkv[0:1, :] = s_rkv_ref[0]
        c_lora[0:1, :] = s_lora_ref[0]

    def token_mix(p, prev, mu):
        row = lax.broadcasted_iota(jnp.int32, p.shape, 0)
        shifted = jnp.where(row == 0, prev, pltpu.roll(p, 1, 0))
        return p + (shifted - p) * mu

    def rkv_seg(s):
        cs = slice(s * width, (s + 1) * width)
        p = prkv_ref[0, :, cs]
        x = token_mix(p, c_rkv[0:1, cs], mu_rkv_ref[:, cs])
        c_rkv[0:1, cs] = p[tc - 1:tc, :]
        return x

    pl_ = plora_ref[0]
    xl = token_mix(pl_, c_lora[0:1, :], mu_lora_ref[...])
    c_lora[0:1, :] = pl_[tc - 1:tc, :]

    r_o[0] = rkv_seg(0)
    v_o[0] = rkv_seg(2)
    xk = rkv_seg(1)

    dec = w0_ref[...] + _mm(jnp.tanh(xl).astype(BF16), ww2_ref[...])
    nd = -dec
    softplus = jnp.maximum(nd, 0.0) + jnp.log(1.0 + jnp.exp(-jnp.abs(nd)))
    w_log = -softplus - 0.5
    wl_o[0] = -jnp.exp(w_log)
    a = jax.nn.sigmoid(a0_ref[...] + _mm(xl.astype(BF16), wa2_ref[...]))
    g_o[0] = _mm(jax.nn.sigmoid(xl).astype(BF16), wg2_ref[...])
    kk = xk * kk_ref[...]
    k_o[0] = xk * (1.0 + (a - 1.0) * ka_ref[...])
    norm = jnp.sqrt(_head_sum(kk * kk, bd_ref[...]))
    kk = kk / jnp.maximum(norm, 1e-12)
    av_o[0] = -kk
    bv_o[0] = kk * a


def _rwkv_prep(p3, s_rkv, s_lora, mu_rkv, mu_lora, w0, a0, k_k, k_a, ww2, wa2, wg2, bd, *, rkv_blk, lora_blk, tc):
    b, t, _ = p3.shape
    width = w0.shape[1]
    lw = mu_lora.shape[1]
    tc = min(tc, t)
    const = lambda shape: pl.BlockSpec(shape, lambda bi, ti: (0,) * len(shape))
    out_spec = pl.BlockSpec((1, tc, width), lambda bi, ti: (bi, ti, 0))
    return pl.pallas_call(
        functools.partial(_prep_body, tc=tc, width=width),
        out_shape=[jax.ShapeDtypeStruct((b, t, width), F32)] * 7,
        grid=(b, t // tc),
        in_specs=[
            pl.BlockSpec((1, tc, 3 * width), lambda bi, ti: (bi, ti, rkv_blk)),
            pl.BlockSpec((1, tc, lw), lambda bi, ti: (bi, ti, lora_blk)),
            pl.BlockSpec((1, 1, 3 * width), lambda bi, ti: (bi, 0, 0)),
            pl.BlockSpec((1, 1, lw), lambda bi, ti: (bi, 0, 0)),
            const((1, 3 * width)), const((1, lw)),
            const((1, width)), const((1, width)), const((1, width)), const((1, width)),
            const((lw, width)), const((lw, width)), const((lw, width)),
            const((width, width)),
        ],
        out_specs=[out_spec] * 7,
        scratch_shapes=[pltpu.VMEM((8, 3 * width), F32), pltpu.VMEM((8, lw), F32)],
        compiler_params=_params(("parallel", "arbitrary"), 48),
        name="rwkv_prep",
    )(p3, p3, s_rkv, s_lora, mu_rkv, mu_lora, w0, a0, k_k, k_a, ww2, wa2, wg2, bd)


def _scan_body(r_ref, wl_ref, k_ref, v_ref, a_ref, b_ref, s0_ref, tri_ref, y_ref, sout_ref, s_scr, *, n_chunks):
    c_len = RW_CHUNK
    t = pl.program_id(2)

    @pl.when(t == 0)
    def _():
        s_scr[...] = s0_ref[0, 0]

    row = lax.broadcasted_iota(jnp.int32, (c_len, c_len), 0)
    col = lax.broadcasted_iota(jnp.int32, (c_len, c_len), 1)
    strict = row > col
    incl = row >= col
    lane = lax.broadcasted_iota(jnp.int32, (c_len, LANES), 1)
    first = lane < HEAD_DIM
    brow = lax.broadcasted_iota(jnp.int32, (LANES, LANES), 0)
    bcol = lax.broadcasted_iota(jnp.int32, (LANES, LANES), 1)
    same_head = (brow // HEAD_DIM) == (bcol // HEAD_DIM)
    eye = brow == bcol
    tri = tri_ref[...]
    bf = lambda x: x.astype(BF16)

    for c in range(n_chunks):
        sl = slice(c * c_len, (c + 1) * c_len)
        wl = wl_ref[0, sl, :]
        hi = wl.astype(BF16)
        rem = wl - hi.astype(F32)
        mid = rem.astype(BF16)
        lo = (rem - mid.astype(F32)).astype(BF16)
        cum = _mm(tri, hi) + _mm(tri, mid) + _mm(tri, lo)
        tot = cum[c_len - 1:c_len, :]
        e_neg = jnp.exp(-cum)
        e_end = jnp.exp(tot - cum)
        av, bv, kv, vv = a_ref[0, sl, :], b_ref[0, sl, :], k_ref[0, sl, :], v_ref[0, sl, :]
        at = av * jnp.exp(cum - wl)
        rt = r_ref[0, sl, :] * jnp.exp(cum)
        bt = bf(bv * e_neg)
        kt = bf(kv * e_neg)
        bh = bv * e_end
        kh = kv * e_end
        v16 = bf(vv)
        at16 = bf(at)
        zero = jnp.zeros_like(at)

        per_head = []
        for h in range(2):
            sel = first if h == 0 else jnp.logical_not(first)
            ar = jnp.concatenate([jnp.where(sel, at, zero), jnp.where(sel, rt, zero)], axis=0).astype(BF16)
            mb = _nt(ar, bt)
            mk = _nt(ar, kt)
            m_ab = jnp.where(strict, mb[:c_len], 0.0)
            p_rb = bf(jnp.where(incl, mb[c_len:], 0.0))
            m_ak = bf(jnp.where(strict, mk[:c_len], 0.0))
            p_rk = bf(jnp.where(incl, mk[c_len:], 0.0))
            tm = jnp.where(row == col, 1.0, 0.0) + jnp.where((row // 2) == (col // 2), m_ab, 0.0)
            s = 2
            while s < c_len:
                off = jnp.logical_and((row // (2 * s)) == (col // (2 * s)), (row // s) != (col // s))
                m_off = bf(jnp.where(off, m_ab, 0.0))
                tm = tm + _mm(bf(_mm(bf(tm), m_off)), bf(tm))
                s *= 2
            t16 = bf(tm)
            mv = _mm(m_ak, v16)
            w1 = _mm(t16, at16)
            w2 = _mm(t16, bf(mv))
            w1_16, w2_16 = bf(w1), bf(w2)
            qc = rt + _mm(p_rb, w1_16)
            y1 = _mm(p_rb, w2_16) + _mm(p_rk, v16)
            per_head.append((w1, w2, qc, y1))

        w1, w2, qc, y1 = (jnp.where(first, x0, x1) for x0, x1 in zip(*per_head))
        s_old = s_scr[...]
        ac_t = jnp.where(same_head, _tn(bf(w1), bf(bh)), 0.0) + jnp.where(eye, jnp.exp(tot), 0.0)
        dc_t = jnp.where(same_head, _tn(bf(jnp.concatenate([w2, vv], axis=0)),
                                        bf(jnp.concatenate([bh, kh], axis=0))), 0.0)
        y_ref[0, sl, :] = _nt(bf(qc), bf(s_old)) + y1
        s_scr[...] = jnp.dot(s_old, ac_t, preferred_element_type=F32, precision=lax.Precision.HIGHEST) + dc_t

    @pl.when(t == pl.num_programs(2) - 1)
    def _():
        sout_ref[0, 0] = s_scr[...]


def _rwkv_scan(r, wl, k, v, av, bv, s0_bd, tri, *, tc):
    b, t, width = r.shape
    n_pairs = width // LANES
    tc = min(tc, t)
    seq = pl.BlockSpec((1, tc, LANES), lambda bi, p, ti: (bi, ti, p))
    state = pl.BlockSpec((1, 1, LANES, LANES), lambda bi, p, ti: (bi, p, 0, 0))
    return pl.pallas_call(
        functools.partial(_scan_body, n_chunks=tc // RW_CHUNK),
        out_shape=[jax.ShapeDtypeStruct((b, t, width), F32), jax.ShapeDtypeStruct(s0_bd.shape, F32)],
        grid=(b, n_pairs, t // tc),
        in_specs=[seq] * 6 + [state, pl.BlockSpec(tri.shape, lambda bi, p, ti: (0, 0))],
        out_specs=[seq, state],
        scratch_shapes=[pltpu.VMEM((LANES, LANES), F32)],
        compiler_params=_params(("parallel", "parallel", "arbitrary"), 32),
        name="rwkv_scan",
    )(r, wl, k, v, av, bv, s0_bd, tri)


def _out_body(h_ref, osb_ref, y_ref, r_ref, k_ref, v_ref, g_ref, gs_ref, gr_ref, lnw_ref, lnb_ref, rk_ref, bd_ref,
              wso_ref, wro_ref, wout_ref, o_ref, m_ref):
    j = pl.program_id(1)

    @pl.when(j == 0)
    def _():
        bd = bd_ref[...]
        y = y_ref[...]
        mu = _head_sum(y, bd) * (1.0 / HEAD_DIM)
        d = y - mu
        var = _head_sum(d * d, bd) * (1.0 / HEAD_DIM)
        yn = d * lax.rsqrt(var + GN_EPS) * lnw_ref[...] + lnb_ref[...]
        bonus = _head_sum(r_ref[...] * k_ref[...] * rk_ref[...], bd) * v_ref[...]
        yy = ((yn + bonus) * g_ref[...]).astype(BF16)
        o_sb = _mm(osb_ref[...].astype(BF16), wso_ref[...])
        o_rw = _mm(yy, wro_ref[...])
        merged = jax.nn.sigmoid(gs_ref[...]) * o_sb + jax.nn.sigmoid(gr_ref[...]) * o_rw
        m_ref[...] = merged.astype(BF16)

    o_ref[...] = h_ref[...] + _mm(m_ref[...], wout_ref[...])


def _merge_out(h, o_sb, y, r, k, v, g, p2, lnw, lnb, rk, bd, wso, wro, wout, *, gs_blk, gr_blk, tm=256, tn=512):
    n, d = h.shape
    width = o_sb.shape[1]
    tm, tn = min(tm, n), min(tn, d)
    tok = pl.BlockSpec((tm, width), lambda i, j: (i, 0))
    const = lambda shape: pl.BlockSpec(shape, lambda i, j: (0,) * len(shape))
    return pl.pallas_call(
        _out_body,
        out_shape=jax.ShapeDtypeStruct((n, d), F32),
        grid=(n // tm, d // tn),
        in_specs=[
            pl.BlockSpec((tm, tn), lambda i, j: (i, j)),
            tok, tok, tok, tok, tok, tok,
            pl.BlockSpec((tm, d), lambda i, j: (i, gs_blk)),
            pl.BlockSpec((tm, d), lambda i, j: (i, gr_blk)),
            const((1, width)), const((1, width)), const((1, width)),
            const((width, width)),
            const((width, d)), const((width, d)),
            pl.BlockSpec((d, tn), lambda i, j: (0, j)),
        ],
        out_specs=pl.BlockSpec((tm, tn), lambda i, j: (i, j)),
        scratch_shapes=[pltpu.VMEM((tm, d), BF16)],
        compiler_params=_params(("parallel", "arbitrary"), 56),
        name="merge_out",
    )(h, o_sb, y, r, k, v, g, p2, p2, lnw, lnb, rk, bd, wso, wro, wout)


def _layer(x, past_k, past_v, wkv0, shift0, w, *, tq, tc):
    b, t, d = x.shape
    n = b * t
    width = w["w0"].shape[1]
    h1 = _ffn(x.reshape(n, d), w["ffn1_norm"], w["ffn1_wg"], w["ffn1_wu"], w["ffn1_wd"])
    p2 = _mix(h1, w["mix_norm"], w["w_in"], w["head_gain"], w["bd"][:512, :512], n_norm_cols=2 * width)
    p3 = p2.reshape(b, t, -1)
    n_pairs = width // LANES
    o_sb = _attention(p3, past_k, past_v, w["u2"], n_pairs=n_pairs, q_blk0=0, k_blk0=n_pairs, v_blk0=2 * n_pairs, tq=tq)

    lora_w = w["mu_lora"].shape[1]
    lora_blk = (6 * width + 2 * d) // lora_w
    lora_cols = w["lora_cols"]
    s_rkv = shift0[:, :, :3 * width]
    s_lora = jnp.pad(shift0[:, :, 3 * width:], ((0, 0), (0, 0), (0, lora_w - lora_cols)))
    r, wl, k, v, av, bv, g = _rwkv_prep(
        p3, s_rkv, s_lora, w["mu_rkv"], w["mu_lora"], w["w0"], w["a0"], w["k_k"], w["k_a"],
        w["ww2"], w["wa2"], w["wg2"], w["bd"], rkv_blk=1, lora_blk=lora_blk, tc=tc)

    s0 = wkv0.reshape(b, n_pairs, 2, HEAD_DIM, HEAD_DIM)
    z = jnp.zeros_like(s0[:, :, 0])
    s0_bd = jnp.concatenate([jnp.concatenate([s0[:, :, 0], z], axis=-1),
                             jnp.concatenate([z, s0[:, :, 1]], axis=-1)], axis=-2)
    y, s_bd = _rwkv_scan(r, wl, k, v, av, bv, s0_bd, w["tri"], tc=tc)
    wkv = jnp.stack([s_bd[:, :, :HEAD_DIM, :HEAD_DIM], s_bd[:, :, HEAD_DIM:, HEAD_DIM:]], axis=2)
    wkv = wkv.reshape(b, 2 * n_pairs, HEAD_DIM, HEAD_DIM)

    flat = lambda a: a.reshape(n, width)
    gs_blk = (6 * width) // d
    h2 = _merge_out(h1, flat(o_sb), flat(y), flat(r), flat(k), flat(v), flat(g), p2,
                    w["ln_w"], w["ln_b"], w["r_k"], w["bd"], w["sb_wo"], w["rw_wo"], w["w_out"],
                    gs_blk=gs_blk, gr_blk=gs_blk + 1)
    out = _ffn(h2, w["ffn2_norm"], w["ffn2_wg"], w["ffn2_wu"], w["ffn2_wd"])

    heads = width // HEAD_DIM
    k_new = p3[:, :, width:2 * width].reshape(b, t, heads, HEAD_DIM)
    v_new = p3[:, :, 2 * width:3 * width].reshape(b, t, heads, HEAD_DIM)
    shift = jnp.concatenate([p3[:, t - 1:, 3 * width:6 * width],
                             p3[:, t - 1:, 6 * width + 2 * d:6 * width + 2 * d + lora_cols]], axis=-1)
    return out.reshape(b, t, d), k_new, v_new, wkv, shift


def _layer_weights(l, ffn1_norm, ffn1_w_gate, ffn1_w_up, ffn1_w_down, mix_norm, w_in, sb_q_norm, sb_k_norm, sb_w_o,
                   rwkv_mu, rwkv_w0, rwkv_w_w2, rwkv_a0, rwkv_w_a2, rwkv_w_g2, rwkv_k_k, rwkv_k_a, rwkv_r_k,
                   rwkv_ln_w, rwkv_ln_b, rwkv_w_o, w_out, ffn2_norm, ffn2_w_gate, ffn2_w_up, ffn2_w_down):
    d = w_in.shape[1]
    width = rwkv_w0.shape[1]
    heads = width // HEAD_DIM
    n_decay, n_iclr, n_gate = rwkv_w_w2.shape[1], rwkv_w_a2.shape[1], rwkv_w_g2.shape[1]
    lora_cols = n_decay + n_iclr + n_gate
    lora_w = -(-lora_cols // 512) * 512
    row = lambda a: a.reshape(1, -1).astype(F32)
    wi = w_in[l]
    w_in_p = jnp.concatenate([
        wi[:, :6 * width], wi[:, 6 * width + lora_cols:], wi[:, 6 * width:6 * width + lora_cols],
        jnp.zeros((d, lora_w - lora_cols), wi.dtype)], axis=1).astype(BF16)
    total = w_in_p.shape[1]
    head_gain = jnp.concatenate([jnp.tile(sb_q_norm[l], heads), jnp.tile(sb_k_norm[l], heads),
                                 jnp.ones((total - 2 * width,), F32)]).reshape(1, total)
    mu = rwkv_mu[l]

    def lora_pad(wm, r0):
        return jnp.zeros((lora_w, width), F32).at[r0:r0 + wm.shape[0]].set(wm).astype(BF16)

    hid = jnp.arange(width) // HEAD_DIM
    bd = (hid[:, None] == hid[None, :]).astype(BF16)
    tk = ATT_TK
    ki = jnp.arange(tk)
    u2 = jnp.concatenate([(ki[:, None] > ki[None, :]).astype(BF16), jnp.ones((tk, LANES), BF16)], axis=1)
    ci = jnp.arange(RW_CHUNK)
    tri = (ci[:, None] >= ci[None, :]).astype(BF16)
    return {
        "ffn1_norm": row(ffn1_norm[l]), "ffn1_wg": ffn1_w_gate[l].astype(BF16), "ffn1_wu": ffn1_w_up[l].astype(BF16),
        "ffn1_wd": ffn1_w_down[l].astype(BF16),
        "ffn2_norm": row(ffn2_norm[l]), "ffn2_wg": ffn2_w_gate[l].astype(BF16), "ffn2_wu": ffn2_w_up[l].astype(BF16),
        "ffn2_wd": ffn2_w_down[l].astype(BF16),
        "mix_norm": row(mix_norm[l]), "w_in": w_in_p, "head_gain": head_gain, "bd": bd, "u2": u2, "tri": tri,
        "mu_rkv": row(mu[:3 * width]), "mu_lora": row(jnp.pad(mu[3 * width:], (0, lora_w - lora_cols))),
        "lora_cols": lora_cols,
        "w0": row(rwkv_w0[l]), "a0": row(rwkv_a0[l]), "k_k": row(rwkv_k_k[l]), "k_a": row(rwkv_k_a[l]),
        "ww2": lora_pad(rwkv_w_w2[l], 0), "wa2": lora_pad(rwkv_w_a2[l], n_decay),
        "wg2": lora_pad(rwkv_w_g2[l], n_decay + n_iclr),
        "ln_w": row(rwkv_ln_w[l]), "ln_b": row(rwkv_ln_b[l]), "r_k": row(rwkv_r_k[l]),
        "sb_wo": sb_w_o[l].astype(BF16), "rw_wo": rwkv_w_o[l].astype(BF16), "w_out": w_out[l].astype(BF16),
    }


def kernel(x_prompt, x_sample, cache_sb_k, cache_sb_v, state_rwkv_wkv, state_rwkv_shift, ffn1_norm, ffn1_w_gate, ffn1_w_up, ffn1_w_down, mix_norm, w_in, sb_q_norm, sb_k_norm, sb_w_o, rwkv_mu, rwkv_w0, rwkv_w_w2, rwkv_a0, rwkv_w_a2, rwkv_w_g2, rwkv_k_k, rwkv_k_a, rwkv_r_k, rwkv_ln_w, rwkv_ln_b, rwkv_w_o, w_out, ffn2_norm, ffn2_w_gate, ffn2_w_up, ffn2_w_down):
    depth = w_in.shape[0]
    yp, ys = x_prompt, x_sample
    bp = x_prompt.shape[0]
    width = rwkv_w0.shape[1]
    heads = width // HEAD_DIM
    rw_cols = state_rwkv_shift.shape[-1]
    outs = [[] for _ in range(8)]
    for l in range(depth):
        w = _layer_weights(l, ffn1_norm, ffn1_w_gate, ffn1_w_up, ffn1_w_down, mix_norm, w_in, sb_q_norm, sb_k_norm,
                           sb_w_o, rwkv_mu, rwkv_w0, rwkv_w_w2, rwkv_a0, rwkv_w_a2, rwkv_w_g2, rwkv_k_k, rwkv_k_a,
                           rwkv_r_k, rwkv_ln_w, rwkv_ln_b, rwkv_w_o, w_out, ffn2_norm, ffn2_w_gate, ffn2_w_up,
                           ffn2_w_down)
        wkv_zero = jnp.zeros((bp, heads, HEAD_DIM, HEAD_DIM), F32)
        shift_zero = jnp.zeros((bp, 1, rw_cols), F32)
        yp, kp, vp, wkvp, shp = _layer(yp, None, None, wkv_zero, shift_zero, w, tq=256, tc=256)
        ck, cv = cache_sb_k[l], cache_sb_v[l]
        ck = ck.reshape(ck.shape[0], ck.shape[1], width)
        cv = cv.reshape(cv.shape[0], cv.shape[1], width)
        ys, kn, vn, wkvn, shn = _layer(ys, ck, cv, state_rwkv_wkv[l], state_rwkv_shift[l], w, tq=64, tc=64)
        for lst, val in zip(outs, (kp, vp, wkvp, shp, kn, vn, wkvn, shn)):
            lst.append(val)
    return (yp, ys) + tuple(jnp.stack(o) for o in outs)
```

```python
import functools

import jax
import jax.numpy as jnp
from jax import lax
from jax.experimental import pallas as pl
from jax.experimental.pallas import tpu as pltpu

F32 = jnp.float32
BF16 = jnp.bfloat16

HEAD_DIM = 64
LANES = 128
NORM_EPS = 1e-6
GN_EPS = 64e-5
RW_CHUNK = 64
ATT_TK = 256

MIB = 1024 * 1024


def _nt(x, y):
    return lax.dot_general(x, y, (((1,), (1,)), ((), ())), preferred_element_type=F32)


def _tn(x, y):
    return lax.dot_general(x, y, (((0,), (0,)), ((), ())), preferred_element_type=F32)


def _mm(x, y):
    return jnp.dot(x, y, preferred_element_type=F32)


def _split2(x):
    hi = x.astype(BF16)
    lo = (x - hi.astype(F32)).astype(BF16)
    return hi, lo


def _head_sum(x, bd):
    hi, lo = _split2(x)
    return _mm(hi, bd) + _mm(lo, bd)


def _rms(x, g):
    ms = jnp.mean(x * x, axis=-1, keepdims=True)
    return x * lax.rsqrt(ms + NORM_EPS) * g


def _params(sem, vmem_mib):
    return pltpu.CompilerParams(dimension_semantics=sem, vmem_limit_bytes=vmem_mib * MIB)


def _ffn_body(x_ref, g_ref, wg_ref, wu_ref, wd_ref, o_ref, n_ref, acc_ref):
    f = pl.program_id(1)

    @pl.when(f == 0)
    def _():
        n_ref[...] = _rms(x_ref[...], g_ref[...]).astype(BF16)
        acc_ref[...] = jnp.zeros_like(acc_ref)

    n = n_ref[...]
    g = _mm(n, wg_ref[...])
    u = _mm(n, wu_ref[...])
    a = (g * jax.nn.sigmoid(g) * u).astype(BF16)
    acc_ref[...] += _mm(a, wd_ref[...])

    @pl.when(f == pl.num_programs(1) - 1)
    def _():
        o_ref[...] = x_ref[...] + 0.5 * acc_ref[...]


def _ffn(x, g, wg, wu, wd, *, tm=512, tf=512):
    n, d = x.shape
    ff = wg.shape[1]
    tm, tf = min(tm, n), min(tf, ff)
    return pl.pallas_call(
        _ffn_body,
        out_shape=jax.ShapeDtypeStruct((n, d), F32),
        grid=(n // tm, ff // tf),
        in_specs=[
            pl.BlockSpec((tm, d), lambda i, f: (i, 0)),
            pl.BlockSpec((1, d), lambda i, f: (0, 0)),
            pl.BlockSpec((d, tf), lambda i, f: (0, f)),
            pl.BlockSpec((d, tf), lambda i, f: (0, f)),
            pl.BlockSpec((tf, d), lambda i, f: (f, 0)),
        ],
        out_specs=pl.BlockSpec((tm, d), lambda i, f: (i, 0)),
        scratch_shapes=[pltpu.VMEM((tm, d), BF16), pltpu.VMEM((tm, d), F32)],
        compiler_params=_params(("parallel", "arbitrary"), 48),
        name="ffn",
    )(x, g, wg, wu, wd)


def _mix_body(h_ref, g_ref, w_ref, hg_ref, bd_ref, o_ref, n_ref, *, n_norm_tiles):
    j = pl.program_id(1)

    @pl.when(j == 0)
    def _():
        n_ref[...] = _rms(h_ref[...], g_ref[...]).astype(BF16)

    p = _mm(n_ref[...], w_ref[...])

    @pl.when(j < n_norm_tiles)
    def _():
        ms = _head_sum(p * p, bd_ref[...]) * (1.0 / HEAD_DIM)
        o_ref[...] = p * lax.rsqrt(ms + NORM_EPS) * hg_ref[...]

    @pl.when(j >= n_norm_tiles)
    def _():
        o_ref[...] = p


def _mix(h, g, w, hgain, bd, *, n_norm_cols, tm=1024, tn=512):
    n, d = h.shape
    cols = w.shape[1]
    tm = min(tm, n)
    return pl.pallas_call(
        functools.partial(_mix_body, n_norm_tiles=n_norm_cols // tn),
        out_shape=jax.ShapeDtypeStruct((n, cols), F32),
        grid=(n // tm, cols // tn),
        in_specs=[
            pl.BlockSpec((tm, d), lambda i, j: (i, 0)),
            pl.BlockSpec((1, d), lambda i, j: (0, 0)),
            pl.BlockSpec((d, tn), lambda i, j: (0, j)),
            pl.BlockSpec((1, tn), lambda i, j: (0, j)),
            pl.BlockSpec((tn, tn), lambda i, j: (0, 0)),
        ],
        out_specs=pl.BlockSpec((tm, tn), lambda i, j: (i, j)),
        scratch_shapes=[pltpu.VMEM((tm, d), BF16)],
        compiler_params=_params(("parallel", "arbitrary"), 48),
        name="mix",
    )(h, g, w, hgain, bd)


def _attn_body(*refs, tq, tk, t_new, t_pad, n_past_blocks, past_unroll):
    if n_past_blocks:
        q_ref, k_ref, v_ref, pk_ref, pv_ref, u2_ref, o_ref, kb, vb, qs, carry, acc = refs
    else:
        q_ref, k_ref, v_ref, u2_ref, o_ref, kb, vb, qs, carry, acc = refs
        pk_ref = pv_ref = None
    i = pl.program_id(2)

    @pl.when(i == 0)
    def _():
        if t_pad > t_new:
            kb[...] = jnp.zeros_like(kb)
            vb[...] = jnp.zeros_like(vb)
        kb[0:t_new, :] = k_ref[0].astype(BF16)
        vb[0:t_new, :] = v_ref[0].astype(BF16)

    q = q_ref[0] * (HEAD_DIM ** -0.5)
    lane = lax.broadcasted_iota(jnp.int32, (tq, LANES), 1)
    first = lane < HEAD_DIM
    zero = jnp.zeros_like(q)
    qs[0] = jnp.where(first, q, zero).astype(BF16)
    qs[1] = jnp.where(first, zero, q).astype(BF16)
    carry[...] = jnp.zeros_like(carry)
    acc[...] = jnp.zeros_like(acc)

    def span(blocks):
        u = u2_ref[...]
        zs = [[_nt(qs[h], kblk) for kblk, _, _ in blocks] for h in range(2)]
        parts = [[], []]
        for h in range(2):
            for z, (_, _, mask) in zip(zs[h], blocks):
                lp = jnp.log(1.0 + jnp.exp(-jnp.abs(z)))
                log_beta = jnp.minimum(z, 0.0) - lp
                log_keep = log_beta - z
                if mask is not None:
                    log_keep = jnp.where(mask, log_keep, 0.0)
                hi, lo = _split2(log_keep)
                tail = _mm(hi, u) + _mm(lo, u)
                parts[h].append((log_beta, tail, jnp.sum(log_keep, axis=-1, keepdims=True)))
        for h in range(2):
            run = carry[h]
            pv = None
            for (log_beta, tail, total), (_, vblk, mask) in zip(parts[h], blocks):
                w = jnp.exp(log_beta + (tail + run))
                if mask is not None:
                    w = jnp.where(mask, w, 0.0)
                d = _mm(w.astype(BF16), vblk)
                pv = d if pv is None else pv + d
                run = run + total
            carry[h] = run
            acc[h] += pv

    q0 = i * tq
    row = lax.broadcasted_iota(jnp.int32, (tq, tk), 0)
    col = lax.broadcasted_iota(jnp.int32, (tq, tk), 1)
    n_diag = max(tq // tk, 1)
    diag = []
    for m in reversed(range(n_diag)):
        k0 = pl.multiple_of(q0 + m * tk, tk)
        diag.append((kb[pl.ds(k0, tk), :], vb[pl.ds(k0, tk), :], (col + m * tk) < row))
    span(diag)

    def new_step(it, _):
        blocks = []
        for m in range(n_diag):
            k0 = pl.multiple_of(q0 - (it * n_diag + m + 1) * tk, tk)
            blocks.append((kb[pl.ds(k0, tk), :], vb[pl.ds(k0, tk), :], None))
        span(blocks)
        return 0

    lax.fori_loop(0, q0 // (tk * n_diag), new_step, 0)

    if n_past_blocks:
        def past_step(it, _):
            blocks = []
            for m in range(past_unroll):
                k0 = pl.multiple_of((n_past_blocks - 1 - it * past_unroll - m) * tk, tk)
                blocks.append((pk_ref[0, pl.ds(k0, tk), :].astype(BF16), pv_ref[0, pl.ds(k0, tk), :].astype(BF16), None))
            span(blocks)
            return 0

        lax.fori_loop(0, n_past_blocks // past_unroll, past_step, 0)

    o_ref[0] = jnp.where(first, acc[0], acc[1])


def _attention(p3, past_k, past_v, u2, *, n_pairs, q_blk0, k_blk0, v_blk0, tq):
    b, t, _ = p3.shape
    tk = ATT_TK
    tq = min(tq, t)
    t_pad = max(t, tk)
    has_past = past_k is not None
    n_past_blocks = past_k.shape[1] // tk if has_past else 0
    in_specs = [
        pl.BlockSpec((1, tq, LANES), lambda bi, p, i: (bi, i, q_blk0 + p)),
        pl.BlockSpec((1, t, LANES), lambda bi, p, i: (bi, 0, k_blk0 + p)),
        pl.BlockSpec((1, t, LANES), lambda bi, p, i: (bi, 0, v_blk0 + p)),
    ]
    args = [p3, p3, p3]
    if has_past:
        pp = past_k.shape[1]
        in_specs += [pl.BlockSpec((1, pp, LANES), lambda bi, p, i: (bi, 0, p))] * 2
        args += [past_k, past_v]
    in_specs.append(pl.BlockSpec(u2.shape, lambda bi, p, i: (0, 0)))
    args.append(u2)
    return pl.pallas_call(
        functools.partial(_attn_body, tq=tq, tk=tk, t_new=t, t_pad=t_pad, n_past_blocks=n_past_blocks,
                          past_unroll=min(4, max(n_past_blocks, 1))),
        out_shape=jax.ShapeDtypeStruct((b, t, n_pairs * LANES), F32),
        grid=(b, n_pairs, t // tq),
        in_specs=in_specs,
        out_specs=pl.BlockSpec((1, tq, LANES), lambda bi, p, i: (bi, i, p)),
        scratch_shapes=[
            pltpu.VMEM((t_pad, LANES), BF16),
            pltpu.VMEM((t_pad, LANES), BF16),
            pltpu.VMEM((2, tq, LANES), BF16),
            pltpu.VMEM((2, tq, 1), F32),
            pltpu.VMEM((2, tq, LANES), F32),
        ],
        compiler_params=_params(("parallel", "parallel", "arbitrary"), 48),
        name="sb_attention",
    )(*args)


def _prep_body(prkv_ref, plora_ref, s_rkv_ref, s_lora_ref, mu_rkv_ref, mu_lora_ref, w0_ref, a0_ref, kk_ref, ka_ref,
               ww2_ref, wa2_ref, wg2_ref, bd_ref,
               r_o, wl_o, k_o, v_o, av_o, bv_o, g_o, c_rkv, c_lora, *, tc, width):
    t = pl.program_id(1)

    @pl.when(t == 0)
    def _():
        c_rkv[0:1, :] = s_rkv_ref[0]
        c_lora[0:1, :] = s_lora_ref[0]

    def token_mix(p, prev, mu):
        row = lax.broadcasted_iota(jnp.int32, p.shape, 0)
        shifted = jnp.where(row == 0, prev, pltpu.roll(p, 1, 0))
        return p + (shifted - p) * mu

    def rkv_seg(s):
        cs = slice(s * width, (s + 1) * width)
        p = prkv_ref[0, :, cs]
        x = token_mix(p, c_rkv[0:1, cs], mu_rkv_ref[:, cs])
        c_rkv[0:1, cs] = p[tc - 1:tc, :]
        return x

    pl_ = plora_ref[0]
    xl = token_mix(pl_, c_lora[0:1, :], mu_lora_ref[...])
    c_lora[0:1, :] = pl_[tc - 1:tc, :]

    r_o[0] = rkv_seg(0)
    v_o[0] = rkv_seg(2)
    xk = rkv_seg(1)

    dec = w0_ref[...] + _mm(jnp.tanh(xl).astype(BF16), ww2_ref[...])
    nd = -dec
    softplus = jnp.maximum(nd, 0.0) + jnp.log(1.0 + jnp.exp(-jnp.abs(nd)))
    w_log = -softplus - 0.5
    wl_o[0] = -jnp.exp(w_log)
    a = jax.nn.sigmoid(a0_ref[...] + _mm(xl.astype(BF16), wa2_ref[...]))
    g_o[0] = _mm(jax.nn.sigmoid(xl).astype(BF16), wg2_ref[...])
    kk = xk * kk_ref[...]
    k_o[0] = xk * (1.0 + (a - 1.0) * ka_ref[...])
    norm = jnp.sqrt(_head_sum(kk * kk, bd_ref[...]))
    kk = kk / jnp.maximum(norm, 1e-12)
    av_o[0] = -kk
    bv_o[0] = kk * a


def _rwkv_prep(p3, s_rkv, s_lora, mu_rkv, mu_lora, w0, a0, k_k, k_a, ww2, wa2, wg2, bd, *, rkv_blk, lora_blk, tc):
    b, t, _ = p3.shape
    width = w0.shape[1]
    lw = mu_lora.shape[1]
    tc = min(tc, t)
    const = lambda shape: pl.BlockSpec(shape, lambda bi, ti: (0,) * len(shape))
    out_spec = pl.BlockSpec((1, tc, width), lambda bi, ti: (bi, ti, 0))
    return pl.pallas_call(
        functools.partial(_prep_body, tc=tc, width=width),
        out_shape=[jax.ShapeDtypeStruct((b, t, width), F32)] * 7,
        grid=(b, t // tc),
        in_specs=[
            pl.BlockSpec((1, tc, 3 * width), lambda bi, ti: (bi, ti, rkv_blk)),
            pl.BlockSpec((1, tc, lw), lambda bi, ti: (bi, ti, lora_blk)),
            pl.BlockSpec((1, 1, 3 * width), lambda bi, ti: (bi, 0, 0)),
            pl.BlockSpec((1, 1, lw), lambda bi, ti: (bi, 0, 0)),
            const((1, 3 * width)), const((1, lw)),
            const((1, width)), const((1, width)), const((1, width)), const((1, width)),
            const((lw, width)), const((lw, width)), const((lw, width)),
            const((width, width)),
        ],
        out_specs=[out_spec] * 7,
        scratch_shapes=[pltpu.VMEM((8, 3 * width), F32), pltpu.VMEM((8, lw), F32)],
        compiler_params=_params(("parallel", "arbitrary"), 48),
        name="rwkv_prep",
    )(p3, p3, s_rkv, s_lora, mu_rkv, mu_lora, w0, a0, k_k, k_a, ww2, wa2, wg2, bd)


def _scan_body(r_ref, wl_ref, k_ref, v_ref, a_ref, b_ref, s0_ref, tri_ref, y_ref, sout_ref, s_scr, *,
               n_chunks, n_pairs):
    c_len = RW_CHUNK
    t = pl.program_id(2)

    @pl.when(t == 0)
    def _():
        s_scr[...] = s0_ref[0]

    row = lax.broadcasted_iota(jnp.int32, (c_len, c_len), 0)
    col = lax.broadcasted_iota(jnp.int32, (c_len, c_len), 1)
    strict = row > col
    incl = row >= col
    lane = lax.broadcasted_iota(jnp.int32, (c_len, LANES), 1)
    first = lane < HEAD_DIM
    brow = lax.broadcasted_iota(jnp.int32, (LANES, LANES), 0)
    bcol = lax.broadcasted_iota(jnp.int32, (LANES, LANES), 1)
    same_head = (brow // HEAD_DIM) == (bcol // HEAD_DIM)
    eye = brow == bcol
    tri = tri_ref[...]
    bf = lambda x: x.astype(BF16)

    cps = [(c, p) for c in range(n_chunks) for p in range(n_pairs)]
    sls = {(c, p): (slice(c * c_len, (c + 1) * c_len), slice(p * LANES, (p + 1) * LANES)) for c, p in cps}
    tiles = {}
    for c in range(n_chunks):
        sl = slice(c * c_len, (c + 1) * c_len)
        wl = wl_ref[0, sl, :]
        hi = wl.astype(BF16)
        rem = wl - hi.astype(F32)
        mid = rem.astype(BF16)
        lo = (rem - mid.astype(F32)).astype(BF16)
        cum = _mm(tri, hi) + _mm(tri, mid) + _mm(tri, lo)
        tot = cum[c_len - 1:c_len, :]
        e_neg = jnp.exp(-cum)
        e_end = jnp.exp(tot - cum)
        av, bv, kv, vv = a_ref[0, sl, :], b_ref[0, sl, :], k_ref[0, sl, :], v_ref[0, sl, :]
        tiles[c] = dict(at=av * jnp.exp(cum - wl), rt=r_ref[0, sl, :] * jnp.exp(cum), bt=bf(bv * e_neg),
                        kt=bf(kv * e_neg), bh=bv * e_end, kh=kv * e_end, vv=vv, etot=jnp.exp(tot))
    tile = lambda name, c, p: tiles[c][name][:, p * LANES:(p + 1) * LANES]

    chains = [(c, p, h) for c, p in cps for h in range(2)]
    mbk = {}
    for c, p, h in chains:
        sel = first if h == 0 else jnp.logical_not(first)
        at, rt = tile("at", c, p), tile("rt", c, p)
        zero = jnp.zeros_like(at)
        ar = jnp.concatenate([jnp.where(sel, at, zero), jnp.where(sel, rt, zero)], axis=0).astype(BF16)
        mbk[c, p, h] = (_nt(ar, tile("bt", c, p)), _nt(ar, tile("kt", c, p)))
    m_ab, p_rb, m_ak, p_rk, tm = {}, {}, {}, {}, {}
    for ch in chains:
        mb, mk = mbk[ch]
        m_ab[ch] = jnp.where(strict, mb[:c_len], 0.0)
        p_rb[ch] = bf(jnp.where(incl, mb[c_len:], 0.0))
        m_ak[ch] = bf(jnp.where(strict, mk[:c_len], 0.0))
        p_rk[ch] = bf(jnp.where(incl, mk[c_len:], 0.0))
        tm[ch] = jnp.where(row == col, 1.0, 0.0) + jnp.where((row // 2) == (col // 2), m_ab[ch], 0.0)
    s = 2
    while s < c_len:
        off = jnp.logical_and((row // (2 * s)) == (col // (2 * s)), (row // s) != (col // s))
        half = {ch: bf(_mm(bf(tm[ch]), bf(jnp.where(off, m_ab[ch], 0.0)))) for ch in chains}
        tm = {ch: tm[ch] + _mm(half[ch], bf(tm[ch])) for ch in chains}
        s *= 2
    t16 = {ch: bf(tm[ch]) for ch in chains}
    mv = {(c, p, h): _mm(m_ak[c, p, h], bf(tile("vv", c, p))) for c, p, h in chains}
    w1 = {(c, p, h): _mm(t16[c, p, h], bf(tile("at", c, p))) for c, p, h in chains}
    w2 = {ch: _mm(t16[ch], bf(mv[ch])) for ch in chains}
    qc = {(c, p, h): tile("rt", c, p) + _mm(p_rb[c, p, h], bf(w1[c, p, h])) for c, p, h in chains}
    y1 = {(c, p, h): _mm(p_rb[c, p, h], bf(w2[c, p, h])) + _mm(p_rk[c, p, h], bf(tile("vv", c, p)))
          for c, p, h in chains}
    both = lambda d, c, p: jnp.where(first, d[c, p, 0], d[c, p, 1])
    ac_t, dc_t, qcs, y1s = {}, {}, {}, {}
    for c, p in cps:
        bh, kh, vv = tile("bh", c, p), tile("kh", c, p), tile("vv", c, p)
        w1p, w2p = both(w1, c, p), both(w2, c, p)
        a_full = jnp.where(same_head, _tn(bf(w1p), bf(bh)), 0.0) + jnp.where(eye, tile("etot", c, p), 0.0)
        ac_t[c, p] = _split2(a_full)
        dc_t[c, p] = jnp.where(same_head, _tn(bf(jnp.concatenate([w2p, vv], axis=0)),
                                              bf(jnp.concatenate([bh, kh], axis=0))), 0.0)
        qcs[c, p], y1s[c, p] = bf(both(qc, c, p)), both(y1, c, p)
    for p in range(n_pairs):
        state = s_scr[p]
        for c in range(n_chunks):
            rs, ls = sls[c, p]
            s_hi, s_lo = _split2(state)
            a_hi, a_lo = ac_t[c, p]
            y_ref[0, rs, ls] = _nt(qcs[c, p], s_hi) + y1s[c, p]
            state = _mm(s_hi, a_hi) + _mm(s_hi, a_lo) + _mm(s_lo, a_hi) + dc_t[c, p]
        s_scr[p] = state

    @pl.when(t == pl.num_programs(2) - 1)
    def _():
        sout_ref[0] = s_scr[...]


def _rwkv_scan(r, wl, k, v, av, bv, s0_bd, tri, *, tc, pairs_per_step):
    b, t, width = r.shape
    n_pairs = width // LANES
    tc = min(tc, t)
    npb = pairs_per_step
    seq = pl.BlockSpec((1, tc, npb * LANES), lambda bi, p, ti: (bi, ti, p))
    state = pl.BlockSpec((1, npb, LANES, LANES), lambda bi, p, ti: (bi, p, 0, 0))
    return pl.pallas_call(
        functools.partial(_scan_body, n_chunks=tc // RW_CHUNK, n_pairs=npb),
        out_shape=[jax.ShapeDtypeStruct((b, t, width), F32), jax.ShapeDtypeStruct(s0_bd.shape, F32)],
        grid=(b, n_pairs // npb, t // tc),
        in_specs=[seq] * 6 + [state, pl.BlockSpec(tri.shape, lambda bi, p, ti: (0, 0))],
        out_specs=[seq, state],
        scratch_shapes=[pltpu.VMEM((npb, LANES, LANES), F32)],
        compiler_params=_params(("parallel", "parallel", "arbitrary"), 32),
        name="rwkv_scan",
    )(r, wl, k, v, av, bv, s0_bd, tri)


def _out_body(h_ref, osb_ref, y_ref, r_ref, k_ref, v_ref, g_ref, gs_ref, gr_ref, lnw_ref, lnb_ref, rk_ref, bd_ref,
              wso_ref, wro_ref, wout_ref, o_ref, m_ref):
    j = pl.program_id(1)

    @pl.when(j == 0)
    def _():
        bd = bd_ref[...]
        y = y_ref[...]
        mu = _head_sum(y, bd) * (1.0 / HEAD_DIM)
        d = y - mu
        var = _head_sum(d * d, bd) * (1.0 / HEAD_DIM)
        yn = d * lax.rsqrt(var + GN_EPS) * lnw_ref[...] + lnb_ref[...]
        bonus = _head_sum(r_ref[...] * k_ref[...] * rk_ref[...], bd) * v_ref[...]
        yy = ((yn + bonus) * g_ref[...]).astype(BF16)
        o_sb = _mm(osb_ref[...].astype(BF16), wso_ref[...])
        o_rw = _mm(yy, wro_ref[...])
        merged = jax.nn.sigmoid(gs_ref[...]) * o_sb + jax.nn.sigmoid(gr_ref[...]) * o_rw
        m_ref[...] = merged.astype(BF16)

    o_ref[...] = h_ref[...] + _mm(m_ref[...], wout_ref[...])


def _merge_out(h, o_sb, y, r, k, v, g, p2, lnw, lnb, rk, bd, wso, wro, wout, *, gs_blk, gr_blk, tm=256, tn=512):
    n, d = h.shape
    width = o_sb.shape[1]
    tm, tn = min(tm, n), min(tn, d)
    tok = pl.BlockSpec((tm, width), lambda i, j: (i, 0))
    const = lambda shape: pl.BlockSpec(shape, lambda i, j: (0,) * len(shape))
    return pl.pallas_call(
        _out_body,
        out_shape=jax.ShapeDtypeStruct((n, d), F32),
        grid=(n // tm, d // tn),
        in_specs=[
            pl.BlockSpec((tm, tn), lambda i, j: (i, j)),
            tok, tok, tok, tok, tok, tok,
            pl.BlockSpec((tm, d), lambda i, j: (i, gs_blk)),
            pl.BlockSpec((tm, d), lambda i, j: (i, gr_blk)),
            const((1, width)), const((1, width)), const((1, width)),
            const((width, width)),
            const((width, d)), const((width, d)),
            pl.BlockSpec((d, tn), lambda i, j: (0, j)),
        ],
        out_specs=pl.BlockSpec((tm, tn), lambda i, j: (i, j)),
        scratch_shapes=[pltpu.VMEM((tm, d), BF16)],
        compiler_params=_params(("parallel", "arbitrary"), 56),
        name="merge_out",
    )(h, o_sb, y, r, k, v, g, p2, p2, lnw, lnb, rk, bd, wso, wro, wout)


def _layer(x, past_k, past_v, wkv0, shift0, w, *, tq, tc, scan_pairs):
    b, t, d = x.shape
    n = b * t
    width = w["w0"].shape[1]
    h1 = _ffn(x.reshape(n, d), w["ffn1_norm"], w["ffn1_wg"], w["ffn1_wu"], w["ffn1_wd"])
    p2 = _mix(h1, w["mix_norm"], w["w_in"], w["head_gain"], w["bd"][:512, :512], n_norm_cols=2 * width)
    p3 = p2.reshape(b, t, -1)
    n_pairs = width // LANES
    o_sb = _attention(p3, past_k, past_v, w["u2"], n_pairs=n_pairs, q_blk0=0, k_blk0=n_pairs, v_blk0=2 * n_pairs, tq=tq)

    lora_w = w["mu_lora"].shape[1]
    lora_blk = (6 * width + 2 * d) // lora_w
    lora_cols = w["lora_cols"]
    s_rkv = shift0[:, :, :3 * width]
    s_lora = jnp.pad(shift0[:, :, 3 * width:], ((0, 0), (0, 0), (0, lora_w - lora_cols)))
    r, wl, k, v, av, bv, g = _rwkv_prep(
        p3, s_rkv, s_lora, w["mu_rkv"], w["mu_lora"], w["w0"], w["a0"], w["k_k"], w["k_a"],
        w["ww2"], w["wa2"], w["wg2"], w["bd"], rkv_blk=1, lora_blk=lora_blk, tc=tc)

    s0 = wkv0.reshape(b, n_pairs, 2, HEAD_DIM, HEAD_DIM)
    z = jnp.zeros_like(s0[:, :, 0])
    s0_bd = jnp.concatenate([jnp.concatenate([s0[:, :, 0], z], axis=-1),
                             jnp.concatenate([z, s0[:, :, 1]], axis=-1)], axis=-2)
    y, s_bd = _rwkv_scan(r, wl, k, v, av, bv, s0_bd, w["tri"], tc=tc, pairs_per_step=scan_pairs)
    wkv = jnp.stack([s_bd[:, :, :HEAD_DIM, :HEAD_DIM], s_bd[:, :, HEAD_DIM:, HEAD_DIM:]], axis=2)
    wkv = wkv.reshape(b, 2 * n_pairs, HEAD_DIM, HEAD_DIM)

    flat = lambda a: a.reshape(n, width)
    gs_blk = (6 * width) // d
    h2 = _merge_out(h1, flat(o_sb), flat(y), flat(r), flat(k), flat(v), flat(g), p2,
                    w["ln_w"], w["ln_b"], w["r_k"], w["bd"], w["sb_wo"], w["rw_wo"], w["w_out"],
                    gs_blk=gs_blk, gr_blk=gs_blk + 1)
    out = _ffn(h2, w["ffn2_norm"], w["ffn2_wg"], w["ffn2_wu"], w["ffn2_wd"])

    heads = width // HEAD_DIM
    k_new = p3[:, :, width:2 * width].reshape(b, t, heads, HEAD_DIM)
    v_new = p3[:, :, 2 * width:3 * width].reshape(b, t, heads, HEAD_DIM)
    shift = jnp.concatenate([p3[:, t - 1:, 3 * width:6 * width],
                             p3[:, t - 1:, 6 * width + 2 * d:6 * width + 2 * d + lora_cols]], axis=-1)
    return out.reshape(b, t, d), k_new, v_new, wkv, shift


def _layer_weights(l, ffn1_norm, ffn1_w_gate, ffn1_w_up, ffn1_w_down, mix_norm, w_in, sb_q_norm, sb_k_norm, sb_w_o,
                   rwkv_mu, rwkv_w0, rwkv_w_w2, rwkv_a0, rwkv_w_a2, rwkv_w_g2, rwkv_k_k, rwkv_k_a, rwkv_r_k,
                   rwkv_ln_w, rwkv_ln_b, rwkv_w_o, w_out, ffn2_norm, ffn2_w_gate, ffn2_w_up, ffn2_w_down):
    d = w_in.shape[1]
    width = rwkv_w0.shape[1]
    heads = width // HEAD_DIM
    n_decay, n_iclr, n_gate = rwkv_w_w2.shape[1], rwkv_w_a2.shape[1], rwkv_w_g2.shape[1]
    lora_cols = n_decay + n_iclr + n_gate
    lora_w = -(-lora_cols // 512) * 512
    row = lambda a: a.reshape(1, -1).astype(F32)
    wi = w_in[l]
    w_in_p = jnp.concatenate([
        wi[:, :6 * width], wi[:, 6 * width + lora_cols:], wi[:, 6 * width:6 * width + lora_cols],
        jnp.zeros((d, lora_w - lora_cols), wi.dtype)], axis=1).astype(BF16)
    total = w_in_p.shape[1]
    head_gain = jnp.concatenate([jnp.tile(sb_q_norm[l], heads), jnp.tile(sb_k_norm[l], heads),
                                 jnp.ones((total - 2 * width,), F32)]).reshape(1, total)
    mu = rwkv_mu[l]

    def lora_pad(wm, r0):
        return jnp.zeros((lora_w, width), F32).at[r0:r0 + wm.shape[0]].set(wm).astype(BF16)

    hid = jnp.arange(width) // HEAD_DIM
    bd = (hid[:, None] == hid[None, :]).astype(BF16)
    tk = ATT_TK
    ki = jnp.arange(tk)
    u2 = (ki[:, None] > ki[None, :]).astype(BF16)
    ci = jnp.arange(RW_CHUNK)
    tri = (ci[:, None] >= ci[None, :]).astype(BF16)
    return {
        "ffn1_norm": row(ffn1_norm[l]), "ffn1_wg": ffn1_w_gate[l].astype(BF16), "ffn1_wu": ffn1_w_up[l].astype(BF16),
        "ffn1_wd": ffn1_w_down[l].astype(BF16),
        "ffn2_norm": row(ffn2_norm[l]), "ffn2_wg": ffn2_w_gate[l].astype(BF16), "ffn2_wu": ffn2_w_up[l].astype(BF16),
        "ffn2_wd": ffn2_w_down[l].astype(BF16),
        "mix_norm": row(mix_norm[l]), "w_in": w_in_p, "head_gain": head_gain, "bd": bd, "u2": u2, "tri": tri,
        "mu_rkv": row(mu[:3 * width]), "mu_lora": row(jnp.pad(mu[3 * width:], (0, lora_w - lora_cols))),
        "lora_cols": lora_cols,
        "w0": row(rwkv_w0[l]), "a0": row(rwkv_a0[l]), "k_k": row(rwkv_k_k[l]), "k_a": row(rwkv_k_a[l]),
        "ww2": lora_pad(rwkv_w_w2[l], 0), "wa2": lora_pad(rwkv_w_a2[l], n_decay),
        "wg2": lora_pad(rwkv_w_g2[l], n_decay + n_iclr),
        "ln_w": row(rwkv_ln_w[l]), "ln_b": row(rwkv_ln_b[l]), "r_k": row(rwkv_r_k[l]),
        "sb_wo": sb_w_o[l].astype(BF16), "rw_wo": rwkv_w_o[l].astype(BF16), "w_out": w_out[l].astype(BF16),
    }


def kernel(x_prompt, x_sample, cache_sb_k, cache_sb_v, state_rwkv_wkv, state_rwkv_shift, ffn1_norm, ffn1_w_gate, ffn1_w_up, ffn1_w_down, mix_norm, w_in, sb_q_norm, sb_k_norm, sb_w_o, rwkv_mu, rwkv_w0, rwkv_w_w2, rwkv_a0, rwkv_w_a2, rwkv_w_g2, rwkv_k_k, rwkv_k_a, rwkv_r_k, rwkv_ln_w, rwkv_ln_b, rwkv_w_o, w_out, ffn2_norm, ffn2_w_gate, ffn2_w_up, ffn2_w_down):
    depth = w_in.shape[0]
    yp, ys = x_prompt, x_sample
    bp = x_prompt.shape[0]
    width = rwkv_w0.shape[1]
    heads = width // HEAD_DIM
    rw_cols = state_rwkv_shift.shape[-1]
    outs = [[] for _ in range(8)]
    for l in range(depth):
        w = _layer_weights(l, ffn1_norm, ffn1_w_gate, ffn1_w_up, ffn1_w_down, mix_norm, w_in, sb_q_norm, sb_k_norm,
                           sb_w_o, rwkv_mu, rwkv_w0, rwkv_w_w2, rwkv_a0, rwkv_w_a2, rwkv_w_g2, rwkv_k_k, rwkv_k_a,
                           rwkv_r_k, rwkv_ln_w, rwkv_ln_b, rwkv_w_o, w_out, ffn2_norm, ffn2_w_gate, ffn2_w_up,
                           ffn2_w_down)
        wkv_zero = jnp.zeros((bp, heads, HEAD_DIM, HEAD_DIM), F32)
        shift_zero = jnp.zeros((bp, 1, rw_cols), F32)
        yp, kp, vp, wkvp, shp = _layer(yp, None, None, wkv_zero, shift_zero, w, tq=512, tc=256, scan_pairs=2)
        ck, cv = cache_sb_k[l], cache_sb_v[l]
        ck = ck.reshape(ck.shape[0], ck.shape[1], width)
        cv = cv.reshape(cv.shape[0], cv.shape[1], width)
        ys, kn, vn, wkvn, shn = _layer(ys, ck, cv, state_rwkv_wkv[l], state_rwkv_shift[l], w, tq=64, tc=64, scan_pairs=8)
        for lst, val in zip(outs, (kp, vp, wkvp, shp, kn, vn, wkvn, shn)):
            lst.append(val)
    return (yp, ys) + tuple(jnp.stack(o) for o in outs)
```

```python
import functools

import jax
import jax.numpy as jnp
from jax import lax
from jax.experimental import pallas as pl
from jax.experimental.pallas import tpu as pltpu

F32 = jnp.float32
BF16 = jnp.bfloat16

HEAD_DIM = 64
LANES = 128
NORM_EPS = 1e-6
GN_EPS = 64e-5
RW_CHUNK = 64
ATT_TK = 256

MIB = 1024 * 1024
LOG2_E = 1.4426950408889634
SIGN_BIT = 0x80000000


def _nt(x, y):
    return lax.dot_general(x, y, (((1,), (1,)), ((), ())), preferred_element_type=F32)


def _tn(x, y):
    return lax.dot_general(x, y, (((0,), (0,)), ((), ())), preferred_element_type=F32)


def _mm(x, y):
    return jnp.dot(x, y, preferred_element_type=F32)


def _neg_abs(x):
    return lax.bitcast_convert_type(lax.bitcast_convert_type(x, jnp.uint32) | jnp.uint32(SIGN_BIT), F32)


def _split2(x):
    hi = x.astype(BF16)
    lo = (x - hi.astype(F32)).astype(BF16)
    return hi, lo


def _head_sum(x, bd):
    hi, lo = _split2(x)
    return _mm(hi, bd) + _mm(lo, bd)


def _head_sum2(x, member):
    hi, lo = _split2(x)
    shi, slo = _split2(_mm(hi, member) + _mm(lo, member))
    return _nt(shi, member) + _nt(slo, member)


def _rms(x, g):
    ms = jnp.mean(x * x, axis=-1, keepdims=True)
    return x * lax.rsqrt(ms + NORM_EPS) * g


def _params(sem, vmem_mib):
    return pltpu.CompilerParams(dimension_semantics=sem, vmem_limit_bytes=vmem_mib * MIB)


def _ffn_body(x_ref, g_ref, wg_ref, wu_ref, wd_ref, o_ref, n_ref, acc_ref):
    f = pl.program_id(1)

    @pl.when(f == 0)
    def _():
        n_ref[...] = _rms(x_ref[...], g_ref[...]).astype(BF16)
        acc_ref[...] = jnp.zeros_like(acc_ref)

    n = n_ref[...]
    g = _mm(n, wg_ref[...])
    u = _mm(n, wu_ref[...])
    a = (g * jax.nn.sigmoid(g) * u).astype(BF16)
    acc_ref[...] += _mm(a, wd_ref[...])

    @pl.when(f == pl.num_programs(1) - 1)
    def _():
        o_ref[...] = x_ref[...] + 0.5 * acc_ref[...]


def _ffn(x, g, wg, wu, wd, *, tm=512, tf=512):
    n, d = x.shape
    ff = wg.shape[1]
    tm, tf = min(tm, n), min(tf, ff)
    return pl.pallas_call(
        _ffn_body,
        out_shape=jax.ShapeDtypeStruct((n, d), F32),
        grid=(n // tm, ff // tf),
        in_specs=[
            pl.BlockSpec((tm, d), lambda i, f: (i, 0)),
            pl.BlockSpec((1, d), lambda i, f: (0, 0)),
            pl.BlockSpec((d, tf), lambda i, f: (0, f)),
            pl.BlockSpec((d, tf), lambda i, f: (0, f)),
            pl.BlockSpec((tf, d), lambda i, f: (f, 0)),
        ],
        out_specs=pl.BlockSpec((tm, d), lambda i, f: (i, 0)),
        scratch_shapes=[pltpu.VMEM((tm, d), BF16), pltpu.VMEM((tm, d), F32)],
        compiler_params=_params(("parallel", "arbitrary"), 48),
        name="ffn",
    )(x, g, wg, wu, wd)


def _mix_body(h_ref, g_ref, w_ref, hg_ref, bd_ref, o_ref, n_ref, *, n_norm_tiles):
    j = pl.program_id(1)

    @pl.when(j == 0)
    def _():
        n_ref[...] = _rms(h_ref[...], g_ref[...]).astype(BF16)

    p = _mm(n_ref[...], w_ref[...])

    @pl.when(j < n_norm_tiles)
    def _():
        ms = _head_sum(p * p, bd_ref[...]) * (1.0 / HEAD_DIM)
        o_ref[...] = p * lax.rsqrt(ms + NORM_EPS) * hg_ref[...]

    @pl.when(j >= n_norm_tiles)
    def _():
        o_ref[...] = p


def _mix(h, g, w, hgain, bd, *, n_norm_cols, tm=1024, tn=512):
    n, d = h.shape
    cols = w.shape[1]
    tm = min(tm, n)
    return pl.pallas_call(
        functools.partial(_mix_body, n_norm_tiles=n_norm_cols // tn),
        out_shape=jax.ShapeDtypeStruct((n, cols), F32),
        grid=(n // tm, cols // tn),
        in_specs=[
            pl.BlockSpec((tm, d), lambda i, j: (i, 0)),
            pl.BlockSpec((1, d), lambda i, j: (0, 0)),
            pl.BlockSpec((d, tn), lambda i, j: (0, j)),
            pl.BlockSpec((1, tn), lambda i, j: (0, j)),
            pl.BlockSpec((tn, tn), lambda i, j: (0, 0)),
        ],
        out_specs=pl.BlockSpec((tm, tn), lambda i, j: (i, j)),
        scratch_shapes=[pltpu.VMEM((tm, d), BF16)],
        compiler_params=_params(("parallel", "arbitrary"), 48),
        name="mix",
    )(h, g, w, hgain, bd)


def _attn_body(*refs, tq, tk, t_new, n_new_blocks, n_past_blocks, past_unroll, new_unroll, chains):
    if n_past_blocks:
        q_ref, k_ref, v_ref, pk_ref, pv_ref, ut_ref, o_ref, kb, vt, qs, carry, acc = refs
    else:
        q_ref, k_ref, v_ref, ut_ref, o_ref, kb, vt, qs, carry, acc = refs
        pk_ref = pv_ref = None
    i = pl.program_id(2)

    def pad_rows(x):
        if x.shape[0] == tk:
            return x
        return jnp.concatenate([x, jnp.zeros((tk - x.shape[0], x.shape[1]), x.dtype)], axis=0)

    @pl.when(i == 0)
    def _():
        if n_new_blocks == 1:
            kb[...] = pad_rows(k_ref[0]).astype(BF16)
            vt[0] = pad_rows(v_ref[0]).T.astype(BF16)
        else:
            def fill(blk, _):
                rows = pl.ds(pl.multiple_of(blk * tk, tk), tk)
                kb[rows, :] = k_ref[0, rows, :].astype(BF16)
                vt[blk] = v_ref[0, rows, :].T.astype(BF16)
                return 0

            lax.fori_loop(0, n_new_blocks, fill, 0)

    q = q_ref[0] * (HEAD_DIM ** -0.5 * LOG2_E)
    n_heads = q.shape[1] // HEAD_DIM
    head_of_lane = lax.broadcasted_iota(jnp.int32, q.shape, 1) // HEAD_DIM
    zero = jnp.zeros_like(q)
    for h in range(n_heads):
        qs[h * tq:(h + 1) * tq, :] = jnp.where(head_of_lane == h, q, zero).astype(BF16)
    carry[...] = jnp.zeros_like(carry)
    acc[...] = jnp.zeros_like(acc)

    def span(blocks):
        ut = ut_ref[...]
        items = [(ci, blk) for blk in blocks for ci in range(len(chains))]
        run = {ci: carry[:, c0:c1] for ci, (c0, c1, _, _) in enumerate(chains)}
        pv = {}
        zs, parts = {}, {}

        def scores(n):
            ci, (kblk, _, _) = items[n]
            c0, c1, _, _ = chains[ci]
            zs[n] = _nt(kblk, qs[c0:c1, :])

        def keep_sums(n):
            _, (_, _, mask) = items[n]
            z = zs.pop(n)
            lp = jnp.log(1.0 + jnp.exp2(_neg_abs(z))) * LOG2_E
            log_beta = jnp.minimum(z, 0.0) - lp
            log_keep = log_beta - z
            if mask is not None:
                log_keep = jnp.where(mask, log_keep, 0.0)
            hi, lo = _split2(log_keep)
            parts[n] = (log_beta, _mm(ut, hi) + _mm(ut, lo))

        def weigh(n):
            ci, (_, vtblk, mask) = items[n]
            _, _, r0, r1 = chains[ci]
            log_beta, ext = parts.pop(n)
            w = jnp.exp2(log_beta + (ext[:tk] + run[ci][0:1]))
            if mask is not None:
                w = jnp.where(mask, w, 0.0)
            d = _mm(vtblk[r0:r1], w.astype(BF16))
            pv[ci] = d if ci not in pv else pv[ci] + d
            run[ci] = run[ci] + ext[tk:tk + 8]

        for stage in (scores, keep_sums, weigh):
            for n in range(len(items)):
                stage(n)
        for ci, (c0, c1, r0, r1) in enumerate(chains):
            carry[:, c0:c1] = run[ci]
            acc[r0:r1, c0:c1] += pv[ci]

    q0 = i * tq
    cw = chains[0][1] - chains[0][0]
    row = lax.broadcasted_iota(jnp.int32, (tk, cw), 0)
    col = lax.broadcasted_iota(jnp.int32, (tk, cw), 1) % tq
    n_diag = max(tq // tk, 1)
    diag = []
    for m in reversed(range(n_diag)):
        blk = q0 // tk + m
        diag.append((kb[pl.ds(pl.multiple_of(blk * tk, tk), tk), :], vt[blk], (row + m * tk) < col))
    span(diag)

    def new_blocks(last, count):
        blocks = []
        for m in range(count):
            blk = last - m
            blocks.append((kb[pl.ds(pl.multiple_of(blk * tk, tk), tk), :], vt[blk], None))
        span(blocks)

    n_left = q0 // tk
    n_main = n_left // new_unroll

    def main_step(it, _):
        new_blocks(n_left - 1 - it * new_unroll, new_unroll)
        return 0

    lax.fori_loop(0, n_main, main_step, 0)
    if new_unroll > n_diag:
        def rest_step(it, _):
            new_blocks(n_left - 1 - n_main * new_unroll - it * n_diag, n_diag)
            return 0

        lax.fori_loop(0, (n_left - n_main * new_unroll) // n_diag, rest_step, 0)

    if n_past_blocks:
        def past_step(it, _):
            blocks = []
            for m in range(past_unroll):
                k0 = pl.multiple_of((n_past_blocks - 1 - it * past_unroll - m) * tk, tk)
                blocks.append((pk_ref[0, pl.ds(k0, tk), :].astype(BF16),
                               pv_ref[0, pl.ds(k0, tk), :].T.astype(BF16), None))
            span(blocks)
            return 0

        lax.fori_loop(0, n_past_blocks // past_unroll, past_step, 0)

    out_t = acc[...].T
    out = out_t[:tq]
    for h in range(1, n_heads):
        out = jnp.where(head_of_lane == h, out_t[h * tq:(h + 1) * tq], out)
    o_ref[0] = out


def _attention(p3, past_k, past_v, u2, *, n_pairs, q_blk0, k_blk0, v_blk0, tq):
    b, t, _ = p3.shape
    tk = ATT_TK
    tq = min(tq, t)
    n_new_blocks = -(-t // tk)
    if tq % LANES == 0:
        npb = 1
        chains = ((0, tq, 0, HEAD_DIM), (tq, 2 * tq, HEAD_DIM, 2 * HEAD_DIM))
    else:
        npb = max(1, (2 * LANES) // (2 * tq))
        chains = ((0, 2 * npb * tq, 0, npb * LANES),)
    lw = npb * LANES
    has_past = past_k is not None
    n_past_blocks = past_k.shape[1] // tk if has_past else 0
    in_specs = [
        pl.BlockSpec((1, tq, lw), lambda bi, p, i: (bi, i, q_blk0 // npb + p)),
        pl.BlockSpec((1, t, lw), lambda bi, p, i: (bi, 0, k_blk0 // npb + p)),
        pl.BlockSpec((1, t, lw), lambda bi, p, i: (bi, 0, v_blk0 // npb + p)),
    ]
    args = [p3, p3, p3]
    if has_past:
        pp = past_k.shape[1]
        in_specs += [pl.BlockSpec((1, pp, lw), lambda bi, p, i: (bi, 0, p))] * 2
        args += [past_k, past_v]
    in_specs.append(pl.BlockSpec(u2.shape, lambda bi, p, i: (0, 0)))
    args.append(u2)
    return pl.pallas_call(
        functools.partial(_attn_body, tq=tq, tk=tk, t_new=t, n_new_blocks=n_new_blocks, n_past_blocks=n_past_blocks,
                          past_unroll=min(4, max(n_past_blocks, 1)), new_unroll=2 * max(tq // tk, 1), chains=chains),
        out_shape=jax.ShapeDtypeStruct((b, t, n_pairs * LANES), F32),
        grid=(b, n_pairs // npb, t // tq),
        in_specs=in_specs,
        out_specs=pl.BlockSpec((1, tq, lw), lambda bi, p, i: (bi, i, p)),
        scratch_shapes=[
            pltpu.VMEM((n_new_blocks * tk, lw), BF16),
            pltpu.VMEM((n_new_blocks, lw, tk), BF16),
            pltpu.VMEM((2 * npb * tq, lw), BF16),
            pltpu.VMEM((8, 2 * npb * tq), F32),
            pltpu.VMEM((lw, 2 * npb * tq), F32),
        ],
        compiler_params=_params(("parallel", "parallel", "arbitrary"), 48),
        name="sb_attention",
    )(*args)


def _prep_body(prkv_ref, plora_ref, s_rkv_ref, s_lora_ref, mu_rkv_ref, mu_lora_ref, w0_ref, a0_ref, kk_ref, ka_ref,
               ww2_ref, wa2_ref, wg2_ref, bd_ref,
               r_o, wl_o, k_o, v_o, av_o, bv_o, g_o, c_rkv, c_lora, *, tc, width):
    t = pl.program_id(1)

    @pl.when(t == 0)
    def _():
        c_rkv[0:1, :] = s_rkv_ref[0]
        c_lora[0:1, :] = s_lora_ref[0]

    def token_mix(p, prev, mu):
        row = lax.broadcasted_iota(jnp.int32, p.shape, 0)
        shifted = jnp.where(row == 0, prev, pltpu.roll(p, 1, 0))
        return p + (shifted - p) * mu

    def rkv_seg(s):
        cs = slice(s * width, (s + 1) * width)
        p = prkv_ref[0, :, cs]
        x = token_mix(p, c_rkv[0:1, cs], mu_rkv_ref[:, cs])
        c_rkv[0:1, cs] = p[tc - 1:tc, :]
        return x

    pl_ = plora_ref[0]
    xl = token_mix(pl_, c_lora[0:1, :], mu_lora_ref[...])
    c_lora[0:1, :] = pl_[tc - 1:tc, :]

    r_o[0] = rkv_seg(0)
    v_o[0] = rkv_seg(2)
    xk = rkv_seg(1)

    dec = w0_ref[...] + _mm(jnp.tanh(xl).astype(BF16), ww2_ref[...])
    nd = -dec
    softplus = jnp.maximum(nd, 0.0) + jnp.log(1.0 + jnp.exp(-jnp.abs(nd)))
    w_log = -softplus - 0.5
    wl_o[0] = -jnp.exp(w_log)
    a = jax.nn.sigmoid(a0_ref[...] + _mm(xl.astype(BF16), wa2_ref[...]))
    g_o[0] = _mm(jax.nn.sigmoid(xl).astype(BF16), wg2_ref[...])
    kk = xk * kk_ref[...]
    k_o[0] = xk * (1.0 + (a - 1.0) * ka_ref[...])
    norm = jnp.sqrt(_head_sum2(kk * kk, bd_ref[...]))
    kk = kk / jnp.maximum(norm, 1e-12)
    av_o[0] = -kk
    bv_o[0] = kk * a


def _rwkv_prep(p3, s_rkv, s_lora, mu_rkv, mu_lora, w0, a0, k_k, k_a, ww2, wa2, wg2, bd, *, rkv_blk, lora_blk, tc):
    b, t, _ = p3.shape
    width = w0.shape[1]
    lw = mu_lora.shape[1]
    tc = min(tc, t)
    const = lambda shape: pl.BlockSpec(shape, lambda bi, ti: (0,) * len(shape))
    out_spec = pl.BlockSpec((1, tc, width), lambda bi, ti: (bi, ti, 0))
    return pl.pallas_call(
        functools.partial(_prep_body, tc=tc, width=width),
        out_shape=[jax.ShapeDtypeStruct((b, t, width), F32)] * 7,
        grid=(b, t // tc),
        in_specs=[
            pl.BlockSpec((1, tc, 3 * width), lambda bi, ti: (bi, ti, rkv_blk)),
            pl.BlockSpec((1, tc, lw), lambda bi, ti: (bi, ti, lora_blk)),
            pl.BlockSpec((1, 1, 3 * width), lambda bi, ti: (bi, 0, 0)),
            pl.BlockSpec((1, 1, lw), lambda bi, ti: (bi, 0, 0)),
            const((1, 3 * width)), const((1, lw)),
            const((1, width)), const((1, width)), const((1, width)), const((1, width)),
            const((lw, width)), const((lw, width)), const((lw, width)),
            const((width, LANES)),
        ],
        out_specs=[out_spec] * 7,
        scratch_shapes=[pltpu.VMEM((8, 3 * width), F32), pltpu.VMEM((8, lw), F32)],
        compiler_params=_params(("parallel", "arbitrary"), 48),
        name="rwkv_prep",
    )(p3, p3, s_rkv, s_lora, mu_rkv, mu_lora, w0, a0, k_k, k_a, ww2, wa2, wg2, bd)


def _scan_body(r_ref, wl_ref, k_ref, v_ref, a_ref, b_ref, s0_ref, tri_ref, y_ref, sout_ref, s_scr, *,
               n_chunks, n_pairs):
    c_len = RW_CHUNK
    t = pl.program_id(2)

    @pl.when(t == 0)
    def _():
        s_scr[...] = s0_ref[0]

    row = lax.broadcasted_iota(jnp.int32, (c_len, c_len), 0)
    col = lax.broadcasted_iota(jnp.int32, (c_len, c_len), 1)
    strict = row > col
    incl = row >= col
    lane = lax.broadcasted_iota(jnp.int32, (c_len, LANES), 1)
    first = lane < HEAD_DIM
    brow = lax.broadcasted_iota(jnp.int32, (LANES, LANES), 0)
    bcol = lax.broadcasted_iota(jnp.int32, (LANES, LANES), 1)
    same_head = (brow // HEAD_DIM) == (bcol // HEAD_DIM)
    eye = brow == bcol
    tri = tri_ref[...]
    bf = lambda x: x.astype(BF16)

    cps = [(c, p) for c in range(n_chunks) for p in range(n_pairs)]
    sls = {(c, p): (slice(c * c_len, (c + 1) * c_len), slice(p * LANES, (p + 1) * LANES)) for c, p in cps}
    tiles = {}
    for c in range(n_chunks):
        sl = slice(c * c_len, (c + 1) * c_len)
        wl = wl_ref[0, sl, :]
        hi = wl.astype(BF16)
        rem = wl - hi.astype(F32)
        mid = rem.astype(BF16)
        lo = (rem - mid.astype(F32)).astype(BF16)
        cum = _mm(tri, hi) + _mm(tri, mid) + _mm(tri, lo)
        tot = cum[c_len - 1:c_len, :]
        e_neg = jnp.exp(-cum)
        e_end = jnp.exp(tot - cum)
        av, bv, kv, vv = a_ref[0, sl, :], b_ref[0, sl, :], k_ref[0, sl, :], v_ref[0, sl, :]
        tiles[c] = dict(at=av * jnp.exp(cum - wl), rt=r_ref[0, sl, :] * jnp.exp(cum), bt=bf(bv * e_neg),
                        kt=bf(kv * e_neg), bh=bv * e_end, kh=kv * e_end, vv=vv, etot=jnp.exp(tot))
    tile = lambda name, c, p: tiles[c][name][:, p * LANES:(p + 1) * LANES]

    chains = [(c, p, h) for c, p in cps for h in range(2)]
    mbk = {}
    for c, p, h in chains:
        sel = first if h == 0 else jnp.logical_not(first)
        at, rt = tile("at", c, p), tile("rt", c, p)
        zero = jnp.zeros_like(at)
        ar = jnp.concatenate([jnp.where(sel, at, zero), jnp.where(sel, rt, zero)], axis=0).astype(BF16)
        mbk[c, p, h] = (_nt(ar, tile("bt", c, p)), _nt(ar, tile("kt", c, p)))
    m_ab, p_rb, m_ak, p_rk, tm = {}, {}, {}, {}, {}
    for ch in chains:
        mb, mk = mbk[ch]
        m_ab[ch] = jnp.where(strict, mb[:c_len], 0.0)
        p_rb[ch] = bf(jnp.where(incl, mb[c_len:], 0.0))
        m_ak[ch] = bf(jnp.where(strict, mk[:c_len], 0.0))
        p_rk[ch] = bf(jnp.where(incl, mk[c_len:], 0.0))
        tm[ch] = jnp.where(row == col, 1.0, 0.0) + jnp.where((row // 2) == (col // 2), m_ab[ch], 0.0)
    s = 2
    while s < c_len:
        off = jnp.logical_and((row // (2 * s)) == (col // (2 * s)), (row // s) != (col // s))
        half = {ch: bf(_mm(bf(tm[ch]), bf(jnp.where(off, m_ab[ch], 0.0)))) for ch in chains}
        tm = {ch: tm[ch] + _mm(half[ch], bf(tm[ch])) for ch in chains}
        s *= 2
    t16 = {ch: bf(tm[ch]) for ch in chains}
    mv = {(c, p, h): _mm(m_ak[c, p, h], bf(tile("vv", c, p))) for c, p, h in chains}
    w1 = {(c, p, h): _mm(t16[c, p, h], bf(tile("at", c, p))) for c, p, h in chains}
    w2 = {ch: _mm(t16[ch], bf(mv[ch])) for ch in chains}
    qc = {(c, p, h): tile("rt", c, p) + _mm(p_rb[c, p, h], bf(w1[c, p, h])) for c, p, h in chains}
    y1 = {(c, p, h): _mm(p_rb[c, p, h], bf(w2[c, p, h])) + _mm(p_rk[c, p, h], bf(tile("vv", c, p)))
          for c, p, h in chains}
    both = lambda d, c, p: jnp.where(first, d[c, p, 0], d[c, p, 1])
    ac_t, dc_t, qcs, y1s = {}, {}, {}, {}
    for c, p in cps:
        bh, kh, vv = tile("bh", c, p), tile("kh", c, p), tile("vv", c, p)
        w1p, w2p = both(w1, c, p), both(w2, c, p)
        a_full = jnp.where(same_head, _tn(bf(w1p), bf(bh)), 0.0) + jnp.where(eye, tile("etot", c, p), 0.0)
        ac_t[c, p] = _split2(a_full)
        dc_t[c, p] = jnp.where(same_head, _tn(bf(jnp.concatenate([w2p, vv], axis=0)),
                                              bf(jnp.concatenate([bh, kh], axis=0))), 0.0)
        qcs[c, p], y1s[c, p] = bf(both(qc, c, p)), both(y1, c, p)
    for p in range(n_pairs):
        state = s_scr[p]
        for c in range(n_chunks):
            rs, ls = sls[c, p]
            s_hi, s_lo = _split2(state)
            a_hi, a_lo = ac_t[c, p]
            y_ref[0, rs, ls] = _nt(qcs[c, p], s_hi) + y1s[c, p]
            state = _mm(s_hi, a_hi) + _mm(s_hi, a_lo) + _mm(s_lo, a_hi) + dc_t[c, p]
        s_scr[p] = state

    @pl.when(t == pl.num_programs(2) - 1)
    def _():
        sout_ref[0] = s_scr[...]


def _rwkv_scan(r, wl, k, v, av, bv, s0_bd, tri, *, tc, pairs_per_step):
    b, t, width = r.shape
    n_pairs = width // LANES
    tc = min(tc, t)
    npb = pairs_per_step
    seq = pl.BlockSpec((1, tc, npb * LANES), lambda bi, p, ti: (bi, ti, p))
    state = pl.BlockSpec((1, npb, LANES, LANES), lambda bi, p, ti: (bi, p, 0, 0))
    return pl.pallas_call(
        functools.partial(_scan_body, n_chunks=tc // RW_CHUNK, n_pairs=npb),
        out_shape=[jax.ShapeDtypeStruct((b, t, width), F32), jax.ShapeDtypeStruct(s0_bd.shape, F32)],
        grid=(b, n_pairs // npb, t // tc),
        in_specs=[seq] * 6 + [state, pl.BlockSpec(tri.shape, lambda bi, p, ti: (0, 0))],
        out_specs=[seq, state],
        scratch_shapes=[pltpu.VMEM((npb, LANES, LANES), F32)],
        compiler_params=_params(("parallel", "parallel", "arbitrary"), 32),
        name="rwkv_scan",
    )(r, wl, k, v, av, bv, s0_bd, tri)


def _out_body(h_ref, osb_ref, y_ref, r_ref, k_ref, v_ref, g_ref, gs_ref, gr_ref, lnw_ref, lnb_ref, rk_ref, bd_ref,
              wso_ref, wro_ref, wout_ref, o_ref, m_ref):
    j = pl.program_id(1)

    @pl.when(j == 0)
    def _():
        bd = bd_ref[...]
        y = y_ref[...]
        mu = _head_sum2(y, bd) * (1.0 / HEAD_DIM)
        d = y - mu
        var = _head_sum2(d * d, bd) * (1.0 / HEAD_DIM)
        yn = d * lax.rsqrt(var + GN_EPS) * lnw_ref[...] + lnb_ref[...]
        bonus = _head_sum2(r_ref[...] * k_ref[...] * rk_ref[...], bd) * v_ref[...]
        yy = ((yn + bonus) * g_ref[...]).astype(BF16)
        o_sb = _mm(osb_ref[...].astype(BF16), wso_ref[...])
        o_rw = _mm(yy, wro_ref[...])
        merged = jax.nn.sigmoid(gs_ref[...]) * o_sb + jax.nn.sigmoid(gr_ref[...]) * o_rw
        m_ref[...] = merged.astype(BF16)

    o_ref[...] = h_ref[...] + _mm(m_ref[...], wout_ref[...])


def _merge_out(h, o_sb, y, r, k, v, g, p2, lnw, lnb, rk, bd, wso, wro, wout, *, gs_blk, gr_blk, tm=256, tn=512):
    n, d = h.shape
    width = o_sb.shape[1]
    tm, tn = min(tm, n), min(tn, d)
    tok = pl.BlockSpec((tm, width), lambda i, j: (i, 0))
    const = lambda shape: pl.BlockSpec(shape, lambda i, j: (0,) * len(shape))
    return pl.pallas_call(
        _out_body,
        out_shape=jax.ShapeDtypeStruct((n, d), F32),
        grid=(n // tm, d // tn),
        in_specs=[
            pl.BlockSpec((tm, tn), lambda i, j: (i, j)),
            tok, tok, tok, tok, tok, tok,
            pl.BlockSpec((tm, d), lambda i, j: (i, gs_blk)),
            pl.BlockSpec((tm, d), lambda i, j: (i, gr_blk)),
            const((1, width)), const((1, width)), const((1, width)),
            const((width, LANES)),
            const((width, d)), const((width, d)),
            pl.BlockSpec((d, tn), lambda i, j: (0, j)),
        ],
        out_specs=pl.BlockSpec((tm, tn), lambda i, j: (i, j)),
        scratch_shapes=[pltpu.VMEM((tm, d), BF16)],
        compiler_params=_params(("parallel", "arbitrary"), 56),
        name="merge_out",
    )(h, o_sb, y, r, k, v, g, p2, p2, lnw, lnb, rk, bd, wso, wro, wout)


def _layer(x, past_k, past_v, wkv0, shift0, w, *, tq, tc, scan_pairs):
    b, t, d = x.shape
    n = b * t
    width = w["w0"].shape[1]
    h1 = _ffn(x.reshape(n, d), w["ffn1_norm"], w["ffn1_wg"], w["ffn1_wu"], w["ffn1_wd"])
    p2 = _mix(h1, w["mix_norm"], w["w_in"], w["head_gain"], w["bd"][:512, :512], n_norm_cols=2 * width)
    p3 = p2.reshape(b, t, -1)
    n_pairs = width // LANES
    o_sb = _attention(p3, past_k, past_v, w["u2"], n_pairs=n_pairs, q_blk0=0, k_blk0=n_pairs, v_blk0=2 * n_pairs, tq=tq)

    lora_w = w["mu_lora"].shape[1]
    lora_blk = (6 * width + 2 * d) // lora_w
    lora_cols = w["lora_cols"]
    s_rkv = shift0[:, :, :3 * width]
    s_lora = jnp.pad(shift0[:, :, 3 * width:], ((0, 0), (0, 0), (0, lora_w - lora_cols)))
    r, wl, k, v, av, bv, g = _rwkv_prep(
        p3, s_rkv, s_lora, w["mu_rkv"], w["mu_lora"], w["w0"], w["a0"], w["k_k"], w["k_a"],
        w["ww2"], w["wa2"], w["wg2"], w["member"], rkv_blk=1, lora_blk=lora_blk, tc=tc)

    s0 = wkv0.reshape(b, n_pairs, 2, HEAD_DIM, HEAD_DIM)
    z = jnp.zeros_like(s0[:, :, 0])
    s0_bd = jnp.concatenate([jnp.concatenate([s0[:, :, 0], z], axis=-1),
                             jnp.concatenate([z, s0[:, :, 1]], axis=-1)], axis=-2)
    y, s_bd = _rwkv_scan(r, wl, k, v, av, bv, s0_bd, w["tri"], tc=tc, pairs_per_step=scan_pairs)
    wkv = jnp.stack([s_bd[:, :, :HEAD_DIM, :HEAD_DIM], s_bd[:, :, HEAD_DIM:, HEAD_DIM:]], axis=2)
    wkv = wkv.reshape(b, 2 * n_pairs, HEAD_DIM, HEAD_DIM)

    flat = lambda a: a.reshape(n, width)
    gs_blk = (6 * width) // d
    h2 = _merge_out(h1, flat(o_sb), flat(y), flat(r), flat(k), flat(v), flat(g), p2,
                    w["ln_w"], w["ln_b"], w["r_k"], w["member"], w["sb_wo"], w["rw_wo"], w["w_out"],
                    gs_blk=gs_blk, gr_blk=gs_blk + 1)
    out = _ffn(h2, w["ffn2_norm"], w["ffn2_wg"], w["ffn2_wu"], w["ffn2_wd"])

    heads = width // HEAD_DIM
    k_new = p3[:, :, width:2 * width].reshape(b, t, heads, HEAD_DIM)
    v_new = p3[:, :, 2 * width:3 * width].reshape(b, t, heads, HEAD_DIM)
    shift = jnp.concatenate([p3[:, t - 1:, 3 * width:6 * width],
                             p3[:, t - 1:, 6 * width + 2 * d:6 * width + 2 * d + lora_cols]], axis=-1)
    return out.reshape(b, t, d), k_new, v_new, wkv, shift


def _layer_weights(l, ffn1_norm, ffn1_w_gate, ffn1_w_up, ffn1_w_down, mix_norm, w_in, sb_q_norm, sb_k_norm, sb_w_o,
                   rwkv_mu, rwkv_w0, rwkv_w_w2, rwkv_a0, rwkv_w_a2, rwkv_w_g2, rwkv_k_k, rwkv_k_a, rwkv_r_k,
                   rwkv_ln_w, rwkv_ln_b, rwkv_w_o, w_out, ffn2_norm, ffn2_w_gate, ffn2_w_up, ffn2_w_down):
    d = w_in.shape[1]
    width = rwkv_w0.shape[1]
    heads = width // HEAD_DIM
    n_decay, n_iclr, n_gate = rwkv_w_w2.shape[1], rwkv_w_a2.shape[1], rwkv_w_g2.shape[1]
    lora_cols = n_decay + n_iclr + n_gate
    lora_w = -(-lora_cols // 512) * 512
    row = lambda a: a.reshape(1, -1).astype(F32)
    wi = w_in[l]
    w_in_p = jnp.concatenate([
        wi[:, :6 * width], wi[:, 6 * width + lora_cols:], wi[:, 6 * width:6 * width + lora_cols],
        jnp.zeros((d, lora_w - lora_cols), wi.dtype)], axis=1).astype(BF16)
    total = w_in_p.shape[1]
    head_gain = jnp.concatenate([jnp.tile(sb_q_norm[l], heads), jnp.tile(sb_k_norm[l], heads),
                                 jnp.ones((total - 2 * width,), F32)]).reshape(1, total)
    mu = rwkv_mu[l]

    def lora_pad(wm, r0):
        return jnp.zeros((lora_w, width), F32).at[r0:r0 + wm.shape[0]].set(wm).astype(BF16)

    hid = jnp.arange(width) // HEAD_DIM
    bd = (hid[:, None] == hid[None, :]).astype(BF16)
    member = (hid[:, None] == jnp.arange(LANES)[None, :]).astype(BF16)
    tk = ATT_TK
    ki = jnp.arange(tk)
    u2 = jnp.concatenate([(ki[None, :] > ki[:, None]).astype(BF16), jnp.ones((16, tk), BF16)], axis=0)
    ci = jnp.arange(RW_CHUNK)
    tri = (ci[:, None] >= ci[None, :]).astype(BF16)
    return {
        "ffn1_norm": row(ffn1_norm[l]), "ffn1_wg": ffn1_w_gate[l].astype(BF16), "ffn1_wu": ffn1_w_up[l].astype(BF16),
        "ffn1_wd": ffn1_w_down[l].astype(BF16),
        "ffn2_norm": row(ffn2_norm[l]), "ffn2_wg": ffn2_w_gate[l].astype(BF16), "ffn2_wu": ffn2_w_up[l].astype(BF16),
        "ffn2_wd": ffn2_w_down[l].astype(BF16),
        "mix_norm": row(mix_norm[l]), "w_in": w_in_p, "head_gain": head_gain, "bd": bd, "member": member, "u2": u2, "tri": tri,
        "mu_rkv": row(mu[:3 * width]), "mu_lora": row(jnp.pad(mu[3 * width:], (0, lora_w - lora_cols))),
        "lora_cols": lora_cols,
        "w0": row(rwkv_w0[l]), "a0": row(rwkv_a0[l]), "k_k": row(rwkv_k_k[l]), "k_a": row(rwkv_k_a[l]),
        "ww2": lora_pad(rwkv_w_w2[l], 0), "wa2": lora_pad(rwkv_w_a2[l], n_decay),
        "wg2": lora_pad(rwkv_w_g2[l], n_decay + n_iclr),
        "ln_w": row(rwkv_ln_w[l]), "ln_b": row(rwkv_ln_b[l]), "r_k": row(rwkv_r_k[l]),
        "sb_wo": sb_w_o[l].astype(BF16), "rw_wo": rwkv_w_o[l].astype(BF16), "w_out": w_out[l].astype(BF16),
    }


def kernel(x_prompt, x_sample, cache_sb_k, cache_sb_v, state_rwkv_wkv, state_rwkv_shift, ffn1_norm, ffn1_w_gate, ffn1_w_up, ffn1_w_down, mix_norm, w_in, sb_q_norm, sb_k_norm, sb_w_o, rwkv_mu, rwkv_w0, rwkv_w_w2, rwkv_a0, rwkv_w_a2, rwkv_w_g2, rwkv_k_k, rwkv_k_a, rwkv_r_k, rwkv_ln_w, rwkv_ln_b, rwkv_w_o, w_out, ffn2_norm, ffn2_w_gate, ffn2_w_up, ffn2_w_down):
    depth = w_in.shape[0]
    yp, ys = x_prompt, x_sample
    bp = x_prompt.shape[0]
    width = rwkv_w0.shape[1]
    heads = width // HEAD_DIM
    rw_cols = state_rwkv_shift.shape[-1]
    outs = [[] for _ in range(8)]
    for l in range(depth):
        w = _layer_weights(l, ffn1_norm, ffn1_w_gate, ffn1_w_up, ffn1_w_down, mix_norm, w_in, sb_q_norm, sb_k_norm,
                           sb_w_o, rwkv_mu, rwkv_w0, rwkv_w_w2, rwkv_a0, rwkv_w_a2, rwkv_w_g2, rwkv_k_k, rwkv_k_a,
                           rwkv_r_k, rwkv_ln_w, rwkv_ln_b, rwkv_w_o, w_out, ffn2_norm, ffn2_w_gate, ffn2_w_up,
                           ffn2_w_down)
        wkv_zero = jnp.zeros((bp, heads, HEAD_DIM, HEAD_DIM), F32)
        shift_zero = jnp.zeros((bp, 1, rw_cols), F32)
        yp, kp, vp, wkvp, shp = _layer(yp, None, None, wkv_zero, shift_zero, w, tq=512, tc=256, scan_pairs=2)
        ck, cv = cache_sb_k[l], cache_sb_v[l]
        ck = ck.reshape(ck.shape[0], ck.shape[1], width)
        cv = cv.reshape(cv.shape[0], cv.shape[1], width)
        ys, kn, vn, wkvn, shn = _layer(ys, ck, cv, state_rwkv_wkv[l], state_rwkv_shift[l], w, tq=64, tc=64, scan_pairs=8)
        for lst, val in zip(outs, (kp, vp, wkvp, shp, kn, vn, wkvn, shn)):
            lst.append(val)
    return (yp, ys) + tuple(jnp.stack(o) for o in outs)
```

```python
import functools

import jax
import jax.numpy as jnp
from jax import lax
from jax.experimental import pallas as pl
from jax.experimental.pallas import tpu as pltpu

F32 = jnp.float32
BF16 = jnp.bfloat16

HEAD_DIM = 64
LANES = 128
NORM_EPS = 1e-6
GN_EPS = 64e-5
RW_CHUNK = 64
ATT_TK = 256

MIB = 1024 * 1024
LOG2_E = 1.4426950408889634
SIGN_BIT = 0x80000000
ATT_SKEW = 2


def _nt(x, y):
    return lax.dot_general(x, y, (((1,), (1,)), ((), ())), preferred_element_type=F32)


def _tn(x, y):
    return lax.dot_general(x, y, (((0,), (0,)), ((), ())), preferred_element_type=F32)


def _mm(x, y):
    return jnp.dot(x, y, preferred_element_type=F32)


def _neg_abs(x):
    return lax.bitcast_convert_type(lax.bitcast_convert_type(x, jnp.uint32) | jnp.uint32(SIGN_BIT), F32)


def _split2(x):
    hi = x.astype(BF16)
    lo = (x - hi.astype(F32)).astype(BF16)
    return hi, lo


def _head_sum(x, bd):
    hi, lo = _split2(x)
    return _mm(hi, bd) + _mm(lo, bd)


def _head_sum2(x, member):
    hi, lo = _split2(x)
    shi, slo = _split2(_mm(hi, member) + _mm(lo, member))
    return _nt(shi, member) + _nt(slo, member)


def _rms(x, g):
    ms = jnp.mean(x * x, axis=-1, keepdims=True)
    return x * lax.rsqrt(ms + NORM_EPS) * g


def _params(sem, vmem_mib):
    return pltpu.CompilerParams(dimension_semantics=sem, vmem_limit_bytes=vmem_mib * MIB)


def _ffn_body(x_ref, g_ref, wg_ref, wu_ref, wd_ref, o_ref, n_ref, acc_ref):
    f = pl.program_id(1)

    @pl.when(f == 0)
    def _():
        n_ref[...] = _rms(x_ref[...], g_ref[...]).astype(BF16)
        acc_ref[...] = jnp.zeros_like(acc_ref)

    n = n_ref[...]
    g = _mm(n, wg_ref[...])
    u = _mm(n, wu_ref[...])
    a = (g * jax.nn.sigmoid(g) * u).astype(BF16)
    acc_ref[...] += _mm(a, wd_ref[...])

    @pl.when(f == pl.num_programs(1) - 1)
    def _():
        o_ref[...] = x_ref[...] + 0.5 * acc_ref[...]


def _ffn(x, g, wg, wu, wd, *, tm=512, tf=512):
    n, d = x.shape
    ff = wg.shape[1]
    tm, tf = min(tm, n), min(tf, ff)
    return pl.pallas_call(
        _ffn_body,
        out_shape=jax.ShapeDtypeStruct((n, d), F32),
        grid=(n // tm, ff // tf),
        in_specs=[
            pl.BlockSpec((tm, d), lambda i, f: (i, 0)),
            pl.BlockSpec((1, d), lambda i, f: (0, 0)),
            pl.BlockSpec((d, tf), lambda i, f: (0, f)),
            pl.BlockSpec((d, tf), lambda i, f: (0, f)),
            pl.BlockSpec((tf, d), lambda i, f: (f, 0)),
        ],
        out_specs=pl.BlockSpec((tm, d), lambda i, f: (i, 0)),
        scratch_shapes=[pltpu.VMEM((tm, d), BF16), pltpu.VMEM((tm, d), F32)],
        compiler_params=_params(("parallel", "arbitrary"), 48),
        name="ffn",
    )(x, g, wg, wu, wd)


def _mix_body(h_ref, g_ref, w_ref, hg_ref, bd_ref, o_ref, n_ref, *, n_norm_tiles):
    j = pl.program_id(1)

    @pl.when(j == 0)
    def _():
        n_ref[...] = _rms(h_ref[...], g_ref[...]).astype(BF16)

    p = _mm(n_ref[...], w_ref[...])

    @pl.when(j < n_norm_tiles)
    def _():
        ms = _head_sum(p * p, bd_ref[...]) * (1.0 / HEAD_DIM)
        o_ref[...] = p * lax.rsqrt(ms + NORM_EPS) * hg_ref[...]

    @pl.when(j >= n_norm_tiles)
    def _():
        o_ref[...] = p


def _mix(h, g, w, hgain, bd, *, n_norm_cols, tm=1024, tn=512):
    n, d = h.shape
    cols = w.shape[1]
    tm = min(tm, n)
    return pl.pallas_call(
        functools.partial(_mix_body, n_norm_tiles=n_norm_cols // tn),
        out_shape=jax.ShapeDtypeStruct((n, cols), F32),
        grid=(n // tm, cols // tn),
        in_specs=[
            pl.BlockSpec((tm, d), lambda i, j: (i, 0)),
            pl.BlockSpec((1, d), lambda i, j: (0, 0)),
            pl.BlockSpec((d, tn), lambda i, j: (0, j)),
            pl.BlockSpec((1, tn), lambda i, j: (0, j)),
            pl.BlockSpec((tn, tn), lambda i, j: (0, 0)),
        ],
        out_specs=pl.BlockSpec((tm, tn), lambda i, j: (i, j)),
        scratch_shapes=[pltpu.VMEM((tm, d), BF16)],
        compiler_params=_params(("parallel", "arbitrary"), 48),
        name="mix",
    )(h, g, w, hgain, bd)


def _sb_items(items, ut, tk, run, pv):
    zs, parts = {}, {}

    def scores(n):
        _, q16, kblk, _, _ = items[n]
        zs[n] = _nt(kblk, q16)

    def keep_sums(n):
        mask = items[n][4]
        z = zs.pop(n)
        lp = jnp.log(1.0 + jnp.exp2(_neg_abs(z))) * LOG2_E
        log_beta = jnp.minimum(z, 0.0) - lp
        log_keep = log_beta - z
        if mask is not None:
            log_keep = jnp.where(mask, log_keep, 0.0)
        parts[n] = (log_beta, _mm(ut, log_keep.astype(BF16)))

    def weigh(n):
        ci, _, _, vt_rows, mask = items[n]
        log_beta, ext = parts.pop(n)
        w = jnp.exp2(log_beta + (ext[:tk] + run[ci][0:1]))
        if mask is not None:
            w = jnp.where(mask, w, 0.0)
        d = _mm(vt_rows, w.astype(BF16))
        pv[ci] = d if ci not in pv else pv[ci] + d
        run[ci] = run[ci] + ext[tk:tk + 8]

    for step in range(len(items) + 2 * ATT_SKEW):
        if step < len(items):
            scores(step)
        if 0 <= step - ATT_SKEW < len(items):
            keep_sums(step - ATT_SKEW)
        if 0 <= step - 2 * ATT_SKEW < len(items):
            weigh(step - 2 * ATT_SKEW)


def _attn_cached_body(q_ref, k_ref, v_ref, ck_ref, cv_ref, ut_ref, o_ref, qs, carry, acc, *,
                      tq, tk, n_heads, chunk_blocks):
    j = pl.program_id(1)
    per_chain = (2 * LANES) // HEAD_DIM
    n_chains = n_heads // per_chain
    cl = per_chain * HEAD_DIM
    cw = per_chain * tq
    head_of_lane = lax.broadcasted_iota(jnp.int32, (tq, cl), 1) // HEAD_DIM
    ut = ut_ref[...]

    def pad_rows(x):
        return jnp.concatenate([x, jnp.zeros((tk - x.shape[0], x.shape[1]), x.dtype)], axis=0)

    def run_items(items):
        run = {c: carry[c] for c in range(n_chains)}
        pv = {}
        _sb_items(items, ut, tk, run, pv)
        for c in range(n_chains):
            carry[c] = run[c]
            acc[c] += pv[c]

    @pl.when(j == 0)
    def _():
        carry[...] = jnp.zeros_like(carry)
        acc[...] = jnp.zeros_like(acc)
        row = lax.broadcasted_iota(jnp.int32, (tk, cw), 0)
        col = lax.broadcasted_iota(jnp.int32, (tk, cw), 1) % tq
        items = []
        for c in range(n_chains):
            lanes = slice(c * cl, (c + 1) * cl)
            q = q_ref[0, :, lanes] * (HEAD_DIM ** -0.5 * LOG2_E)
            zero = jnp.zeros_like(q)
            for h in range(per_chain):
                qs[c, h * tq:(h + 1) * tq, :] = jnp.where(head_of_lane == h, q, zero).astype(BF16)
            items.append((c, qs[c], pad_rows(k_ref[0, :, lanes]).astype(BF16),
                          pad_rows(v_ref[0, :, lanes]).T.astype(BF16), row < col))
        run_items(items)

    items = []
    for m in reversed(range(chunk_blocks)):
        for c in range(n_chains):
            def heads(ref):
                rows = [ref[0, pl.ds(m * tk * n_heads + c * per_chain + h, tk, stride=n_heads), :]
                        for h in range(per_chain)]
                return jnp.concatenate(rows, axis=1)
            items.append((c, qs[c], heads(ck_ref).astype(BF16), heads(cv_ref).T.astype(BF16), None))
    run_items(items)

    @pl.when(j == pl.num_programs(1) - 1)
    def _():
        for c in range(n_chains):
            out_t = acc[c].T
            out = out_t[:tq]
            for h in range(1, per_chain):
                out = jnp.where(head_of_lane == h, out_t[h * tq:(h + 1) * tq], out)
            o_ref[0, :, c * cl:(c + 1) * cl] = out


def _attention_cached(p3, cache_k, cache_v, ut, *, width, chunk_keys=512):
    b, t, _ = p3.shape
    tk = ATT_TK
    _, p_len, n_heads, _ = cache_k.shape
    chunk_keys = min(chunk_keys, p_len)
    ck = cache_k.reshape(b, p_len * n_heads, HEAD_DIM)
    cv = cache_v.reshape(b, p_len * n_heads, HEAD_DIM)
    n_chunks = p_len // chunk_keys
    per_chain = (2 * LANES) // HEAD_DIM
    n_chains = n_heads // per_chain
    cw = per_chain * t
    cache_spec = pl.BlockSpec((1, chunk_keys * n_heads, HEAD_DIM), lambda bi, j: (bi, n_chunks - 1 - j, 0))
    return pl.pallas_call(
        functools.partial(_attn_cached_body, tq=t, tk=tk, n_heads=n_heads, chunk_blocks=chunk_keys // tk),
        out_shape=jax.ShapeDtypeStruct((b, t, width), F32),
        grid=(b, n_chunks),
        in_specs=[
            pl.BlockSpec((1, t, width), lambda bi, j: (bi, 0, 0)),
            pl.BlockSpec((1, t, width), lambda bi, j: (bi, 0, 1)),
            pl.BlockSpec((1, t, width), lambda bi, j: (bi, 0, 2)),
            cache_spec, cache_spec,
            pl.BlockSpec(ut.shape, lambda bi, j: (0, 0)),
        ],
        out_specs=pl.BlockSpec((1, t, width), lambda bi, j: (bi, 0, 0)),
        scratch_shapes=[
            pltpu.VMEM((n_chains, cw, 2 * LANES), BF16),
            pltpu.VMEM((n_chains, 8, cw), F32),
            pltpu.VMEM((n_chains, 2 * LANES, cw), F32),
        ],
        compiler_params=_params(("parallel", "arbitrary"), 48),
        name="sb_attention_cached",
    )(p3, p3, p3, ck, cv, ut)


def _attn_body(*refs, tq, tk, t_new, n_new_blocks, n_past_blocks, past_unroll, new_unroll, chains):
    if n_past_blocks:
        q_ref, k_ref, v_ref, pk_ref, pv_ref, ut_ref, o_ref, kb, vt, qs, carry, acc = refs
    else:
        q_ref, k_ref, v_ref, ut_ref, o_ref, kb, vt, qs, carry, acc = refs
        pk_ref = pv_ref = None
    i = pl.program_id(2)

    def pad_rows(x):
        if x.shape[0] == tk:
            return x
        return jnp.concatenate([x, jnp.zeros((tk - x.shape[0], x.shape[1]), x.dtype)], axis=0)

    @pl.when(i == 0)
    def _():
        if n_new_blocks == 1:
            kb[...] = pad_rows(k_ref[0]).astype(BF16)
            vt[0] = pad_rows(v_ref[0]).T.astype(BF16)
        else:
            def fill(blk, _):
                rows = pl.ds(pl.multiple_of(blk * tk, tk), tk)
                kb[rows, :] = k_ref[0, rows, :].astype(BF16)
                vt[blk] = v_ref[0, rows, :].T.astype(BF16)
                return 0

            lax.fori_loop(0, n_new_blocks, fill, 0)

    q = q_ref[0] * (HEAD_DIM ** -0.5 * LOG2_E)
    n_heads = q.shape[1] // HEAD_DIM
    head_of_lane = lax.broadcasted_iota(jnp.int32, q.shape, 1) // HEAD_DIM
    zero = jnp.zeros_like(q)
    for h in range(n_heads):
        qs[h * tq:(h + 1) * tq, :] = jnp.where(head_of_lane == h, q, zero).astype(BF16)
    carry[...] = jnp.zeros_like(carry)
    acc[...] = jnp.zeros_like(acc)

    def span(blocks):
        q16 = [qs[c0:c1, :] for c0, c1, _, _ in chains]
        items = [(ci, q16[ci], kblk, vtblk[r0:r1], mask)
                 for kblk, vtblk, mask in blocks for ci, (_, _, r0, r1) in enumerate(chains)]
        run = {ci: carry[:, c0:c1] for ci, (c0, c1, _, _) in enumerate(chains)}
        pv = {}
        _sb_items(items, ut_ref[...], tk, run, pv)
        for ci, (c0, c1, r0, r1) in enumerate(chains):
            carry[:, c0:c1] = run[ci]
            acc[r0:r1, c0:c1] += pv[ci]

    q0 = i * tq
    cw = chains[0][1] - chains[0][0]
    row = lax.broadcasted_iota(jnp.int32, (tk, cw), 0)
    col = lax.broadcasted_iota(jnp.int32, (tk, cw), 1) % tq
    n_diag = max(tq // tk, 1)
    diag = []
    for m in reversed(range(n_diag)):
        blk = q0 // tk + m
        diag.append((kb[pl.ds(pl.multiple_of(blk * tk, tk), tk), :], vt[blk], (row + m * tk) < col))
    span(diag)

    def new_blocks(last, count):
        blocks = []
        for m in range(count):
            blk = last - m
            blocks.append((kb[pl.ds(pl.multiple_of(blk * tk, tk), tk), :], vt[blk], None))
        span(blocks)

    n_left = q0 // tk
    n_main = n_left // new_unroll

    def main_step(it, _):
        new_blocks(n_left - 1 - it * new_unroll, new_unroll)
        return 0

    lax.fori_loop(0, n_main, main_step, 0)
    if new_unroll > n_diag:
        def rest_step(it, _):
            new_blocks(n_left - 1 - n_main * new_unroll - it * n_diag, n_diag)
            return 0

        lax.fori_loop(0, (n_left - n_main * new_unroll) // n_diag, rest_step, 0)

    if n_past_blocks:
        def past_step(it, _):
            blocks = []
            for m in range(past_unroll):
                k0 = pl.multiple_of((n_past_blocks - 1 - it * past_unroll - m) * tk, tk)
                blocks.append((pk_ref[0, pl.ds(k0, tk), :].astype(BF16),
                               pv_ref[0, pl.ds(k0, tk), :].T.astype(BF16), None))
            span(blocks)
            return 0

        lax.fori_loop(0, n_past_blocks // past_unroll, past_step, 0)

    out_t = acc[...].T
    out = out_t[:tq]
    for h in range(1, n_heads):
        out = jnp.where(head_of_lane == h, out_t[h * tq:(h + 1) * tq], out)
    o_ref[0] = out


def _attention(p3, past_k, past_v, u2, *, n_pairs, q_blk0, k_blk0, v_blk0, tq):
    b, t, _ = p3.shape
    tk = ATT_TK
    tq = min(tq, t)
    n_new_blocks = -(-t // tk)
    if tq % LANES == 0:
        npb = 1
        chains = ((0, tq, 0, HEAD_DIM), (tq, 2 * tq, HEAD_DIM, 2 * HEAD_DIM))
    else:
        npb = max(1, (2 * LANES) // (2 * tq))
        chains = ((0, 2 * npb * tq, 0, npb * LANES),)
    lw = npb * LANES
    has_past = past_k is not None
    n_past_blocks = past_k.shape[1] // tk if has_past else 0
    in_specs = [
        pl.BlockSpec((1, tq, lw), lambda bi, p, i: (bi, i, q_blk0 // npb + p)),
        pl.BlockSpec((1, t, lw), lambda bi, p, i: (bi, 0, k_blk0 // npb + p)),
        pl.BlockSpec((1, t, lw), lambda bi, p, i: (bi, 0, v_blk0 // npb + p)),
    ]
    args = [p3, p3, p3]
    if has_past:
        pp = past_k.shape[1]
        in_specs += [pl.BlockSpec((1, pp, lw), lambda bi, p, i: (bi, 0, p))] * 2
        args += [past_k, past_v]
    in_specs.append(pl.BlockSpec(u2.shape, lambda bi, p, i: (0, 0)))
    args.append(u2)
    return pl.pallas_call(
        functools.partial(_attn_body, tq=tq, tk=tk, t_new=t, n_new_blocks=n_new_blocks, n_past_blocks=n_past_blocks,
                          past_unroll=min(4, max(n_past_blocks, 1)), new_unroll=2 * max(tq // tk, 1), chains=chains),
        out_shape=jax.ShapeDtypeStruct((b, t, n_pairs * LANES), F32),
        grid=(b, n_pairs // npb, t // tq),
        in_specs=in_specs,
        out_specs=pl.BlockSpec((1, tq, lw), lambda bi, p, i: (bi, i, p)),
        scratch_shapes=[
            pltpu.VMEM((n_new_blocks * tk, lw), BF16),
            pltpu.VMEM((n_new_blocks, lw, tk), BF16),
            pltpu.VMEM((2 * npb * tq, lw), BF16),
            pltpu.VMEM((8, 2 * npb * tq), F32),
            pltpu.VMEM((lw, 2 * npb * tq), F32),
        ],
        compiler_params=_params(("parallel", "parallel", "arbitrary"), 48),
        name="sb_attention",
    )(*args)


def _prep_body(prkv_ref, plora_ref, s_rkv_ref, s_lora_ref, mu_rkv_ref, mu_lora_ref, w0_ref, a0_ref, kk_ref, ka_ref,
               ww2_ref, wa2_ref, wg2_ref, bd_ref,
               r_o, wl_o, k_o, v_o, av_o, bv_o, g_o, c_rkv, c_lora, *, tc, width):
    t = pl.program_id(1)

    @pl.when(t == 0)
    def _():
        c_rkv[0:1, :] = s_rkv_ref[0]
        c_lora[0:1, :] = s_lora_ref[0]

    def token_mix(p, prev, mu):
        row = lax.broadcasted_iota(jnp.int32, p.shape, 0)
        shifted = jnp.where(row == 0, prev, pltpu.roll(p, 1, 0))
        return p + (shifted - p) * mu

    def rkv_seg(s):
        cs = slice(s * width, (s + 1) * width)
        p = prkv_ref[0, :, cs]
        x = token_mix(p, c_rkv[0:1, cs], mu_rkv_ref[:, cs])
        c_rkv[0:1, cs] = p[tc - 1:tc, :]
        return x

    pl_ = plora_ref[0]
    xl = token_mix(pl_, c_lora[0:1, :], mu_lora_ref[...])
    c_lora[0:1, :] = pl_[tc - 1:tc, :]

    r_o[0] = rkv_seg(0)
    v_o[0] = rkv_seg(2)
    xk = rkv_seg(1)

    dec = w0_ref[...] + _mm(jnp.tanh(xl).astype(BF16), ww2_ref[...])
    nd = -dec
    softplus = jnp.maximum(nd, 0.0) + jnp.log(1.0 + jnp.exp(-jnp.abs(nd)))
    w_log = -softplus - 0.5
    wl_o[0] = -jnp.exp(w_log)
    a = jax.nn.sigmoid(a0_ref[...] + _mm(xl.astype(BF16), wa2_ref[...]))
    g_o[0] = _mm(jax.nn.sigmoid(xl).astype(BF16), wg2_ref[...])
    kk = xk * kk_ref[...]
    k_o[0] = xk * (1.0 + (a - 1.0) * ka_ref[...])
    norm = jnp.sqrt(_head_sum2(kk * kk, bd_ref[...]))
    kk = kk / jnp.maximum(norm, 1e-12)
    av_o[0] = -kk
    bv_o[0] = kk * a


def _rwkv_prep(p3, s_rkv, s_lora, mu_rkv, mu_lora, w0, a0, k_k, k_a, ww2, wa2, wg2, bd, *, rkv_blk, lora_blk, tc):
    b, t, _ = p3.shape
    width = w0.shape[1]
    lw = mu_lora.shape[1]
    tc = min(tc, t)
    const = lambda shape: pl.BlockSpec(shape, lambda bi, ti: (0,) * len(shape))
    out_spec = pl.BlockSpec((1, tc, width), lambda bi, ti: (bi, ti, 0))
    return pl.pallas_call(
        functools.partial(_prep_body, tc=tc, width=width),
        out_shape=[jax.ShapeDtypeStruct((b, t, width), F32)] * 7,
        grid=(b, t // tc),
        in_specs=[
            pl.BlockSpec((1, tc, 3 * width), lambda bi, ti: (bi, ti, rkv_blk)),
            pl.BlockSpec((1, tc, lw), lambda bi, ti: (bi, ti, lora_blk)),
            pl.BlockSpec((1, 1, 3 * width), lambda bi, ti: (bi, 0, 0)),
            pl.BlockSpec((1, 1, lw), lambda bi, ti: (bi, 0, 0)),
            const((1, 3 * width)), const((1, lw)),
            const((1, width)), const((1, width)), const((1, width)), const((1, width)),
            const((lw, width)), const((lw, width)), const((lw, width)),
            const((width, LANES)),
        ],
        out_specs=[out_spec] * 7,
        scratch_shapes=[pltpu.VMEM((8, 3 * width), F32), pltpu.VMEM((8, lw), F32)],
        compiler_params=_params(("parallel", "arbitrary"), 48),
        name="rwkv_prep",
    )(p3, p3, s_rkv, s_lora, mu_rkv, mu_lora, w0, a0, k_k, k_a, ww2, wa2, wg2, bd)


def _scan_body(r_ref, wl_ref, k_ref, v_ref, a_ref, b_ref, s0_ref, tri_ref, y_ref, sout_ref, s_scr, *,
               n_chunks, n_pairs):
    c_len = RW_CHUNK
    t = pl.program_id(2)

    @pl.when(t == 0)
    def _():
        s_scr[...] = s0_ref[0]

    row = lax.broadcasted_iota(jnp.int32, (c_len, c_len), 0)
    col = lax.broadcasted_iota(jnp.int32, (c_len, c_len), 1)
    strict = row > col
    incl = row >= col
    lane = lax.broadcasted_iota(jnp.int32, (c_len, LANES), 1)
    first = lane < HEAD_DIM
    brow = lax.broadcasted_iota(jnp.int32, (LANES, LANES), 0)
    bcol = lax.broadcasted_iota(jnp.int32, (LANES, LANES), 1)
    same_head = (brow // HEAD_DIM) == (bcol // HEAD_DIM)
    eye = brow == bcol
    tri = tri_ref[...]
    bf = lambda x: x.astype(BF16)

    cps = [(c, p) for c in range(n_chunks) for p in range(n_pairs)]
    sls = {(c, p): (slice(c * c_len, (c + 1) * c_len), slice(p * LANES, (p + 1) * LANES)) for c, p in cps}
    tiles = {}
    for c in range(n_chunks):
        sl = slice(c * c_len, (c + 1) * c_len)
        wl = wl_ref[0, sl, :]
        hi = wl.astype(BF16)
        rem = wl - hi.astype(F32)
        mid = rem.astype(BF16)
        lo = (rem - mid.astype(F32)).astype(BF16)
        cum = _mm(tri, hi) + _mm(tri, mid) + _mm(tri, lo)
        tot = cum[c_len - 1:c_len, :]
        e_neg = jnp.exp(-cum)
        e_end = jnp.exp(tot - cum)
        av, bv, kv, vv = a_ref[0, sl, :], b_ref[0, sl, :], k_ref[0, sl, :], v_ref[0, sl, :]
        tiles[c] = dict(at=av * jnp.exp(cum - wl), rt=r_ref[0, sl, :] * jnp.exp(cum), bt=bf(bv * e_neg),
                        kt=bf(kv * e_neg), bh=bv * e_end, kh=kv * e_end, vv=vv, etot=jnp.exp(tot))
    tile = lambda name, c, p: tiles[c][name][:, p * LANES:(p + 1) * LANES]

    chains = [(c, p, h) for c, p in cps for h in range(2)]
    mbk = {}
    for c, p, h in chains:
        sel = first if h == 0 else jnp.logical_not(first)
        at, rt = tile("at", c, p), tile("rt", c, p)
        zero = jnp.zeros_like(at)
        ar = jnp.concatenate([jnp.where(sel, at, zero), jnp.where(sel, rt, zero)], axis=0).astype(BF16)
        mbk[c, p, h] = (_nt(ar, tile("bt", c, p)), _nt(ar, tile("kt", c, p)))
    m_ab, p_rb, m_ak, p_rk, tm = {}, {}, {}, {}, {}
    for ch in chains:
        mb, mk = mbk[ch]
        m_ab[ch] = jnp.where(strict, mb[:c_len], 0.0)
        p_rb[ch] = bf(jnp.where(incl, mb[c_len:], 0.0))
        m_ak[ch] = bf(jnp.where(strict, mk[:c_len], 0.0))
        p_rk[ch] = bf(jnp.where(incl, mk[c_len:], 0.0))
        tm[ch] = jnp.where(row == col, 1.0, 0.0) + jnp.where((row // 2) == (col // 2), m_ab[ch], 0.0)
    s = 2
    while s < c_len:
        off = jnp.logical_and((row // (2 * s)) == (col // (2 * s)), (row // s) != (col // s))
        half = {ch: bf(_mm(bf(tm[ch]), bf(jnp.where(off, m_ab[ch], 0.0)))) for ch in chains}
        tm = {ch: tm[ch] + _mm(half[ch], bf(tm[ch])) for ch in chains}
        s *= 2
    t16 = {ch: bf(tm[ch]) for ch in chains}
    mv = {(c, p, h): _mm(m_ak[c, p, h], bf(tile("vv", c, p))) for c, p, h in chains}
    w1 = {(c, p, h): _mm(t16[c, p, h], bf(tile("at", c, p))) for c, p, h in chains}
    w2 = {ch: _mm(t16[ch], bf(mv[ch])) for ch in chains}
    qc = {(c, p, h): tile("rt", c, p) + _mm(p_rb[c, p, h], bf(w1[c, p, h])) for c, p, h in chains}
    y1 = {(c, p, h): _mm(p_rb[c, p, h], bf(w2[c, p, h])) + _mm(p_rk[c, p, h], bf(tile("vv", c, p)))
          for c, p, h in chains}
    both = lambda d, c, p: jnp.where(first, d[c, p, 0], d[c, p, 1])
    ac_t, dc_t, qcs, y1s = {}, {}, {}, {}
    for c, p in cps:
        bh, kh, vv = tile("bh", c, p), tile("kh", c, p), tile("vv", c, p)
        w1p, w2p = both(w1, c, p), both(w2, c, p)
        a_full = jnp.where(same_head, _tn(bf(w1p), bf(bh)), 0.0) + jnp.where(eye, tile("etot", c, p), 0.0)
        ac_t[c, p] = _split2(a_full)
        dc_t[c, p] = jnp.where(same_head, _tn(bf(jnp.concatenate([w2p, vv], axis=0)),
                                              bf(jnp.concatenate([bh, kh], axis=0))), 0.0)
        qcs[c, p], y1s[c, p] = bf(both(qc, c, p)), both(y1, c, p)
    for p in range(n_pairs):
        state = s_scr[p]
        for c in range(n_chunks):
            rs, ls = sls[c, p]
            s_hi, s_lo = _split2(state)
            a_hi, a_lo = ac_t[c, p]
            y_ref[0, rs, ls] = _nt(qcs[c, p], s_hi) + y1s[c, p]
            state = _mm(s_hi, a_hi) + _mm(s_hi, a_lo) + _mm(s_lo, a_hi) + dc_t[c, p]
        s_scr[p] = state

    @pl.when(t == pl.num_programs(2) - 1)
    def _():
        sout_ref[0] = s_scr[...]


def _rwkv_scan(r, wl, k, v, av, bv, s0_bd, tri, *, tc, pairs_per_step):
    b, t, width = r.shape
    n_pairs = width // LANES
    tc = min(tc, t)
    npb = pairs_per_step
    seq = pl.BlockSpec((1, tc, npb * LANES), lambda bi, p, ti: (bi, ti, p))
    state = pl.BlockSpec((1, npb, LANES, LANES), lambda bi, p, ti: (bi, p, 0, 0))
    return pl.pallas_call(
        functools.partial(_scan_body, n_chunks=tc // RW_CHUNK, n_pairs=npb),
        out_shape=[jax.ShapeDtypeStruct((b, t, width), F32), jax.ShapeDtypeStruct(s0_bd.shape, F32)],
        grid=(b, n_pairs // npb, t // tc),
        in_specs=[seq] * 6 + [state, pl.BlockSpec(tri.shape, lambda bi, p, ti: (0, 0))],
        out_specs=[seq, state],
        scratch_shapes=[pltpu.VMEM((npb, LANES, LANES), F32)],
        compiler_params=_params(("parallel", "parallel", "arbitrary"), 32),
        name="rwkv_scan",
    )(r, wl, k, v, av, bv, s0_bd, tri)


def _out_body(h_ref, osb_ref, y_ref, r_ref, k_ref, v_ref, g_ref, gs_ref, gr_ref, lnw_ref, lnb_ref, rk_ref, bd_ref,
              wso_ref, wro_ref, wout_ref, o_ref, m_ref):
    j = pl.program_id(1)

    @pl.when(j == 0)
    def _():
        bd = bd_ref[...]
        y = y_ref[...]
        mu = _head_sum2(y, bd) * (1.0 / HEAD_DIM)
        d = y - mu
        var = _head_sum2(d * d, bd) * (1.0 / HEAD_DIM)
        yn = d * lax.rsqrt(var + GN_EPS) * lnw_ref[...] + lnb_ref[...]
        bonus = _head_sum2(r_ref[...] * k_ref[...] * rk_ref[...], bd) * v_ref[...]
        yy = ((yn + bonus) * g_ref[...]).astype(BF16)
        o_sb = _mm(osb_ref[...].astype(BF16), wso_ref[...])
        o_rw = _mm(yy, wro_ref[...])
        merged = jax.nn.sigmoid(gs_ref[...]) * o_sb + jax.nn.sigmoid(gr_ref[...]) * o_rw
        m_ref[...] = merged.astype(BF16)

    o_ref[...] = h_ref[...] + _mm(m_ref[...], wout_ref[...])


def _merge_out(h, o_sb, y, r, k, v, g, p2, lnw, lnb, rk, bd, wso, wro, wout, *, gs_blk, gr_blk, tm=256, tn=512):
    n, d = h.shape
    width = o_sb.shape[1]
    tm, tn = min(tm, n), min(tn, d)
    tok = pl.BlockSpec((tm, width), lambda i, j: (i, 0))
    const = lambda shape: pl.BlockSpec(shape, lambda i, j: (0,) * len(shape))
    return pl.pallas_call(
        _out_body,
        out_shape=jax.ShapeDtypeStruct((n, d), F32),
        grid=(n // tm, d // tn),
        in_specs=[
            pl.BlockSpec((tm, tn), lambda i, j: (i, j)),
            tok, tok, tok, tok, tok, tok,
            pl.BlockSpec((tm, d), lambda i, j: (i, gs_blk)),
            pl.BlockSpec((tm, d), lambda i, j: (i, gr_blk)),
            const((1, width)), const((1, width)), const((1, width)),
            const((width, LANES)),
            const((width, d)), const((width, d)),
            pl.BlockSpec((d, tn), lambda i, j: (0, j)),
        ],
        out_specs=pl.BlockSpec((tm, tn), lambda i, j: (i, j)),
        scratch_shapes=[pltpu.VMEM((tm, d), BF16)],
        compiler_params=_params(("parallel", "arbitrary"), 56),
        name="merge_out",
    )(h, o_sb, y, r, k, v, g, p2, p2, lnw, lnb, rk, bd, wso, wro, wout)


def _layer(x, past_k, past_v, wkv0, shift0, w, *, tq, tc, scan_pairs):
    b, t, d = x.shape
    n = b * t
    width = w["w0"].shape[1]
    h1 = _ffn(x.reshape(n, d), w["ffn1_norm"], w["ffn1_wg"], w["ffn1_wu"], w["ffn1_wd"])
    p2 = _mix(h1, w["mix_norm"], w["w_in"], w["head_gain"], w["bd"][:512, :512], n_norm_cols=2 * width)
    p3 = p2.reshape(b, t, -1)
    n_pairs = width // LANES
    if past_k is None:
        o_sb = _attention(p3, None, None, w["u2"], n_pairs=n_pairs, q_blk0=0, k_blk0=n_pairs, v_blk0=2 * n_pairs,
                          tq=tq)
    else:
        o_sb = _attention_cached(p3, past_k, past_v, w["u2"], width=width)

    lora_w = w["mu_lora"].shape[1]
    lora_blk = (6 * width + 2 * d) // lora_w
    lora_cols = w["lora_cols"]
    s_rkv = shift0[:, :, :3 * width]
    s_lora = jnp.pad(shift0[:, :, 3 * width:], ((0, 0), (0, 0), (0, lora_w - lora_cols)))
    r, wl, k, v, av, bv, g = _rwkv_prep(
        p3, s_rkv, s_lora, w["mu_rkv"], w["mu_lora"], w["w0"], w["a0"], w["k_k"], w["k_a"],
        w["ww2"], w["wa2"], w["wg2"], w["member"], rkv_blk=1, lora_blk=lora_blk, tc=tc)

    s0 = wkv0.reshape(b, n_pairs, 2, HEAD_DIM, HEAD_DIM)
    z = jnp.zeros_like(s0[:, :, 0])
    s0_bd = jnp.concatenate([jnp.concatenate([s0[:, :, 0], z], axis=-1),
                             jnp.concatenate([z, s0[:, :, 1]], axis=-1)], axis=-2)
    y, s_bd = _rwkv_scan(r, wl, k, v, av, bv, s0_bd, w["tri"], tc=tc, pairs_per_step=scan_pairs)
    wkv = jnp.stack([s_bd[:, :, :HEAD_DIM, :HEAD_DIM], s_bd[:, :, HEAD_DIM:, HEAD_DIM:]], axis=2)
    wkv = wkv.reshape(b, 2 * n_pairs, HEAD_DIM, HEAD_DIM)

    flat = lambda a: a.reshape(n, width)
    gs_blk = (6 * width) // d
    h2 = _merge_out(h1, flat(o_sb), flat(y), flat(r), flat(k), flat(v), flat(g), p2,
                    w["ln_w"], w["ln_b"], w["r_k"], w["member"], w["sb_wo"], w["rw_wo"], w["w_out"],
                    gs_blk=gs_blk, gr_blk=gs_blk + 1)
    out = _ffn(h2, w["ffn2_norm"], w["ffn2_wg"], w["ffn2_wu"], w["ffn2_wd"])

    heads = width // HEAD_DIM
    k_new = p3[:, :, width:2 * width].reshape(b, t, heads, HEAD_DIM)
    v_new = p3[:, :, 2 * width:3 * width].reshape(b, t, heads, HEAD_DIM)
    shift = jnp.concatenate([p3[:, t - 1:, 3 * width:6 * width],
                             p3[:, t - 1:, 6 * width + 2 * d:6 * width + 2 * d + lora_cols]], axis=-1)
    return out.reshape(b, t, d), k_new, v_new, wkv, shift


def _layer_weights(l, ffn1_norm, ffn1_w_gate, ffn1_w_up, ffn1_w_down, mix_norm, w_in, sb_q_norm, sb_k_norm, sb_w_o,
                   rwkv_mu, rwkv_w0, rwkv_w_w2, rwkv_a0, rwkv_w_a2, rwkv_w_g2, rwkv_k_k, rwkv_k_a, rwkv_r_k,
                   rwkv_ln_w, rwkv_ln_b, rwkv_w_o, w_out, ffn2_norm, ffn2_w_gate, ffn2_w_up, ffn2_w_down):
    d = w_in.shape[1]
    width = rwkv_w0.shape[1]
    heads = width // HEAD_DIM
    n_decay, n_iclr, n_gate = rwkv_w_w2.shape[1], rwkv_w_a2.shape[1], rwkv_w_g2.shape[1]
    lora_cols = n_decay + n_iclr + n_gate
    lora_w = -(-lora_cols // 512) * 512
    row = lambda a: a.reshape(1, -1).astype(F32)
    wi = w_in[l]
    w_in_p = jnp.concatenate([
        wi[:, :6 * width], wi[:, 6 * width + lora_cols:], wi[:, 6 * width:6 * width + lora_cols],
        jnp.zeros((d, lora_w - lora_cols), wi.dtype)], axis=1).astype(BF16)
    total = w_in_p.shape[1]
    head_gain = jnp.concatenate([jnp.tile(sb_q_norm[l], heads), jnp.tile(sb_k_norm[l], heads),
                                 jnp.ones((total - 2 * width,), F32)]).reshape(1, total)
    mu = rwkv_mu[l]

    def lora_pad(wm, r0):
        return jnp.zeros((lora_w, width), F32).at[r0:r0 + wm.shape[0]].set(wm).astype(BF16)

    hid = jnp.arange(width) // HEAD_DIM
    bd = (hid[:, None] == hid[None, :]).astype(BF16)
    member = (hid[:, None] == jnp.arange(LANES)[None, :]).astype(BF16)
    tk = ATT_TK
    ki = jnp.arange(tk)
    u2 = jnp.concatenate([(ki[None, :] > ki[:, None]).astype(BF16), jnp.ones((16, tk), BF16)], axis=0)
    ci = jnp.arange(RW_CHUNK)
    tri = (ci[:, None] >= ci[None, :]).astype(BF16)
    return {
        "ffn1_norm": row(ffn1_norm[l]), "ffn1_wg": ffn1_w_gate[l].astype(BF16), "ffn1_wu": ffn1_w_up[l].astype(BF16),
        "ffn1_wd": ffn1_w_down[l].astype(BF16),
        "ffn2_norm": row(ffn2_norm[l]), "ffn2_wg": ffn2_w_gate[l].astype(BF16), "ffn2_wu": ffn2_w_up[l].astype(BF16),
        "ffn2_wd": ffn2_w_down[l].astype(BF16),
        "mix_norm": row(mix_norm[l]), "w_in": w_in_p, "head_gain": head_gain, "bd": bd, "member": member, "u2": u2, "tri": tri,
        "mu_rkv": row(mu[:3 * width]), "mu_lora": row(jnp.pad(mu[3 * width:], (0, lora_w - lora_cols))),
        "lora_cols": lora_cols,
        "w0": row(rwkv_w0[l]), "a0": row(rwkv_a0[l]), "k_k": row(rwkv_k_k[l]), "k_a": row(rwkv_k_a[l]),
        "ww2": lora_pad(rwkv_w_w2[l], 0), "wa2": lora_pad(rwkv_w_a2[l], n_decay),
        "wg2": lora_pad(rwkv_w_g2[l], n_decay + n_iclr),
        "ln_w": row(rwkv_ln_w[l]), "ln_b": row(rwkv_ln_b[l]), "r_k": row(rwkv_r_k[l]),
        "sb_wo": sb_w_o[l].astype(BF16), "rw_wo": rwkv_w_o[l].astype(BF16), "w_out": w_out[l].astype(BF16),
    }


def kernel(x_prompt, x_sample, cache_sb_k, cache_sb_v, state_rwkv_wkv, state_rwkv_shift, ffn1_norm, ffn1_w_gate, ffn1_w_up, ffn1_w_down, mix_norm, w_in, sb_q_norm, sb_k_norm, sb_w_o, rwkv_mu, rwkv_w0, rwkv_w_w2, rwkv_a0, rwkv_w_a2, rwkv_w_g2, rwkv_k_k, rwkv_k_a, rwkv_r_k, rwkv_ln_w, rwkv_ln_b, rwkv_w_o, w_out, ffn2_norm, ffn2_w_gate, ffn2_w_up, ffn2_w_down):
    depth = w_in.shape[0]
    yp, ys = x_prompt, x_sample
    bp = x_prompt.shape[0]
    width = rwkv_w0.shape[1]
    heads = width // HEAD_DIM
    rw_cols = state_rwkv_shift.shape[-1]
    outs = [[] for _ in range(8)]
    for l in range(depth):
        w = _layer_weights(l, ffn1_norm, ffn1_w_gate, ffn1_w_up, ffn1_w_down, mix_norm, w_in, sb_q_norm, sb_k_norm,
                           sb_w_o, rwkv_mu, rwkv_w0, rwkv_w_w2, rwkv_a0, rwkv_w_a2, rwkv_w_g2, rwkv_k_k, rwkv_k_a,
                           rwkv_r_k, rwkv_ln_w, rwkv_ln_b, rwkv_w_o, w_out, ffn2_norm, ffn2_w_gate, ffn2_w_up,
                           ffn2_w_down)
        wkv_zero = jnp.zeros((bp, heads, HEAD_DIM, HEAD_DIM), F32)
        shift_zero = jnp.zeros((bp, 1, rw_cols), F32)
        yp, kp, vp, wkvp, shp = _layer(yp, None, None, wkv_zero, shift_zero, w, tq=512, tc=256, scan_pairs=2)
        ys, kn, vn, wkvn, shn = _layer(ys, cache_sb_k[l], cache_sb_v[l], state_rwkv_wkv[l], state_rwkv_shift[l], w, tq=64, tc=64, scan_pairs=8)
        for lst, val in zip(outs, (kp, vp, wkvp, shp, kn, vn, wkvn, shn)):
            lst.append(val)
    return (yp, ys) + tuple(jnp.stack(o) for o in outs)
```

```python
import functools

import jax
import jax.numpy as jnp
from jax import lax
from jax.experimental import pallas as pl
from jax.experimental.pallas import tpu as pltpu

F32 = jnp.float32
BF16 = jnp.bfloat16

HEAD_DIM = 64
LANES = 128
NORM_EPS = 1e-6
GN_EPS = 64e-5
RW_CHUNK = 64
ATT_TK = 256

MIB = 1024 * 1024
LOG2_E = 1.4426950408889634
SIGN_BIT = 0x80000000
ATT_SKEW = 2


def _nt(x, y):
    return lax.dot_general(x, y, (((1,), (1,)), ((), ())), preferred_element_type=F32)


def _tn(x, y):
    return lax.dot_general(x, y, (((0,), (0,)), ((), ())), preferred_element_type=F32)


def _mm(x, y):
    return jnp.dot(x, y, preferred_element_type=F32)


def _neg_abs(x):
    return lax.bitcast_convert_type(lax.bitcast_convert_type(x, jnp.uint32) | jnp.uint32(SIGN_BIT), F32)


def _split2(x):
    hi = x.astype(BF16)
    lo = (x - hi.astype(F32)).astype(BF16)
    return hi, lo


def _head_sum(x, bd):
    hi, lo = _split2(x)
    return _mm(hi, bd) + _mm(lo, bd)


def _head_sum2(x, member):
    hi, lo = _split2(x)
    shi, slo = _split2(_mm(hi, member) + _mm(lo, member))
    return _nt(shi, member) + _nt(slo, member)


def _rms(x, g):
    ms = jnp.mean(x * x, axis=-1, keepdims=True)
    return x * lax.rsqrt(ms + NORM_EPS) * g


def _params(sem, vmem_mib):
    return pltpu.CompilerParams(dimension_semantics=sem, vmem_limit_bytes=vmem_mib * MIB)


def _ffn_body(x_ref, g_ref, wg_ref, wu_ref, wd_ref, o_ref, n_ref, acc_ref):
    f = pl.program_id(1)

    @pl.when(f == 0)
    def _():
        n_ref[...] = _rms(x_ref[...], g_ref[...]).astype(BF16)
        acc_ref[...] = jnp.zeros_like(acc_ref)

    n = n_ref[...]
    g = _mm(n, wg_ref[...])
    u = _mm(n, wu_ref[...])
    a = (g * jax.nn.sigmoid(g) * u).astype(BF16)
    acc_ref[...] += _mm(a, wd_ref[...])

    @pl.when(f == pl.num_programs(1) - 1)
    def _():
        o_ref[...] = x_ref[...] + 0.5 * acc_ref[...]


def _ffn(x, g, wg, wu, wd, *, tm=512, tf=512):
    n, d = x.shape
    ff = wg.shape[1]
    tm, tf = min(tm, n), min(tf, ff)
    return pl.pallas_call(
        _ffn_body,
        out_shape=jax.ShapeDtypeStruct((n, d), F32),
        grid=(n // tm, ff // tf),
        in_specs=[
            pl.BlockSpec((tm, d), lambda i, f: (i, 0)),
            pl.BlockSpec((1, d), lambda i, f: (0, 0)),
            pl.BlockSpec((d, tf), lambda i, f: (0, f)),
            pl.BlockSpec((d, tf), lambda i, f: (0, f)),
            pl.BlockSpec((tf, d), lambda i, f: (f, 0)),
        ],
        out_specs=pl.BlockSpec((tm, d), lambda i, f: (i, 0)),
        scratch_shapes=[pltpu.VMEM((tm, d), BF16), pltpu.VMEM((tm, d), F32)],
        compiler_params=_params(("parallel", "arbitrary"), 48),
        name="ffn",
    )(x, g, wg, wu, wd)


def _mix_body(h_ref, g_ref, w_ref, hg_ref, bd_ref, o_ref, n_ref, *, n_norm_tiles):
    j = pl.program_id(1)

    @pl.when(j == 0)
    def _():
        n_ref[...] = _rms(h_ref[...], g_ref[...]).astype(BF16)

    p = _mm(n_ref[...], w_ref[...])

    @pl.when(j < n_norm_tiles)
    def _():
        ms = _head_sum(p * p, bd_ref[...]) * (1.0 / HEAD_DIM)
        o_ref[...] = p * lax.rsqrt(ms + NORM_EPS) * hg_ref[...]

    @pl.when(j >= n_norm_tiles)
    def _():
        o_ref[...] = p


def _mix(h, g, w, hgain, bd, *, n_norm_cols, tm=1024, tn=512):
    n, d = h.shape
    cols = w.shape[1]
    tm = min(tm, n)
    return pl.pallas_call(
        functools.partial(_mix_body, n_norm_tiles=n_norm_cols // tn),
        out_shape=jax.ShapeDtypeStruct((n, cols), F32),
        grid=(n // tm, cols // tn),
        in_specs=[
            pl.BlockSpec((tm, d), lambda i, j: (i, 0)),
            pl.BlockSpec((1, d), lambda i, j: (0, 0)),
            pl.BlockSpec((d, tn), lambda i, j: (0, j)),
            pl.BlockSpec((1, tn), lambda i, j: (0, j)),
            pl.BlockSpec((tn, tn), lambda i, j: (0, 0)),
        ],
        out_specs=pl.BlockSpec((tm, tn), lambda i, j: (i, j)),
        scratch_shapes=[pltpu.VMEM((tm, d), BF16)],
        compiler_params=_params(("parallel", "arbitrary"), 48),
        name="mix",
    )(h, g, w, hgain, bd)


def _sb_items(items, ut, tk, run, pv):
    zs, parts = {}, {}

    def scores(n):
        _, q16, kblk, _, _ = items[n]
        zs[n] = _nt(kblk, q16)

    def keep_sums(n):
        mask = items[n][4]
        z = zs.pop(n)
        lp = jnp.log(1.0 + jnp.exp2(_neg_abs(z))) * LOG2_E
        log_beta = jnp.minimum(z, 0.0) - lp
        log_keep = log_beta - z
        if mask is not None:
            log_keep = jnp.where(mask, log_keep, 0.0)
        parts[n] = (log_beta, _mm(ut, log_keep.astype(BF16)))

    def weigh(n):
        ci, _, _, vt_rows, mask = items[n]
        log_beta, ext = parts.pop(n)
        w = jnp.exp2(log_beta + (ext[:tk] + run[ci][0:1]))
        if mask is not None:
            w = jnp.where(mask, w, 0.0)
        d = _mm(vt_rows, w.astype(BF16))
        pv[ci] = d if ci not in pv else pv[ci] + d
        run[ci] = run[ci] + ext[tk:tk + 8]

    for step in range(len(items) + 2 * ATT_SKEW):
        if step < len(items):
            scores(step)
        if 0 <= step - ATT_SKEW < len(items):
            keep_sums(step - ATT_SKEW)
        if 0 <= step - 2 * ATT_SKEW < len(items):
            weigh(step - 2 * ATT_SKEW)


def _attn_cached_body(q_ref, k_ref, v_ref, ck_ref, cv_ref, ut_ref, o_ref, qs, carry, acc, *,
                      tq, tk, n_heads, chunk_blocks):
    j = pl.program_id(1)
    per_chain = (2 * LANES) // HEAD_DIM
    n_chains = n_heads // per_chain
    cl = per_chain * HEAD_DIM
    cw = per_chain * tq
    head_of_lane = lax.broadcasted_iota(jnp.int32, (tq, cl), 1) // HEAD_DIM
    ut = ut_ref[...]

    def pad_rows(x):
        return jnp.concatenate([x, jnp.zeros((tk - x.shape[0], x.shape[1]), x.dtype)], axis=0)

    def run_items(items):
        run = {c: carry[c] for c in range(n_chains)}
        pv = {}
        _sb_items(items, ut, tk, run, pv)
        for c in range(n_chains):
            carry[c] = run[c]
            acc[c] += pv[c]

    @pl.when(j == 0)
    def _():
        carry[...] = jnp.zeros_like(carry)
        acc[...] = jnp.zeros_like(acc)
        row = lax.broadcasted_iota(jnp.int32, (tk, cw), 0)
        col = lax.broadcasted_iota(jnp.int32, (tk, cw), 1) % tq
        items = []
        for c in range(n_chains):
            lanes = slice(c * cl, (c + 1) * cl)
            q = q_ref[0, :, lanes] * (HEAD_DIM ** -0.5 * LOG2_E)
            zero = jnp.zeros_like(q)
            for h in range(per_chain):
                qs[c, h * tq:(h + 1) * tq, :] = jnp.where(head_of_lane == h, q, zero).astype(BF16)
            items.append((c, qs[c], pad_rows(k_ref[0, :, lanes]).astype(BF16),
                          pad_rows(v_ref[0, :, lanes]).T.astype(BF16), row < col))
        run_items(items)

    items = []
    for m in reversed(range(chunk_blocks)):
        for c in range(n_chains):
            def heads(ref):
                flat = ref.reshape(ref.shape[1] * n_heads, HEAD_DIM)
                rows = [flat[pl.ds(m * tk * n_heads + c * per_chain + h, tk, stride=n_heads), :]
                        for h in range(per_chain)]
                return jnp.concatenate(rows, axis=1)
            items.append((c, qs[c], heads(ck_ref).astype(BF16), heads(cv_ref).T.astype(BF16), None))
    run_items(items)

    @pl.when(j == pl.num_programs(1) - 1)
    def _():
        for c in range(n_chains):
            out_t = acc[c].T
            out = out_t[:tq]
            for h in range(1, per_chain):
                out = jnp.where(head_of_lane == h, out_t[h * tq:(h + 1) * tq], out)
            o_ref[0, :, c * cl:(c + 1) * cl] = out


def _attention_cached(p3, cache_k, cache_v, ut, *, width, chunk_keys=512):
    b, t, _ = p3.shape
    tk = ATT_TK
    _, p_len, n_heads, _ = cache_k.shape
    chunk_keys = min(chunk_keys, p_len)
    n_chunks = p_len // chunk_keys
    per_chain = (2 * LANES) // HEAD_DIM
    n_chains = n_heads // per_chain
    cw = per_chain * t
    cache_spec = pl.BlockSpec((1, chunk_keys, n_heads, HEAD_DIM), lambda bi, j: (bi, n_chunks - 1 - j, 0, 0))
    return pl.pallas_call(
        functools.partial(_attn_cached_body, tq=t, tk=tk, n_heads=n_heads, chunk_blocks=chunk_keys // tk),
        out_shape=jax.ShapeDtypeStruct((b, t, width), F32),
        grid=(b, n_chunks),
        in_specs=[
            pl.BlockSpec((1, t, width), lambda bi, j: (bi, 0, 0)),
            pl.BlockSpec((1, t, width), lambda bi, j: (bi, 0, 1)),
            pl.BlockSpec((1, t, width), lambda bi, j: (bi, 0, 2)),
            cache_spec, cache_spec,
            pl.BlockSpec(ut.shape, lambda bi, j: (0, 0)),
        ],
        out_specs=pl.BlockSpec((1, t, width), lambda bi, j: (bi, 0, 0)),
        scratch_shapes=[
            pltpu.VMEM((n_chains, cw, 2 * LANES), BF16),
            pltpu.VMEM((n_chains, 8, cw), F32),
            pltpu.VMEM((n_chains, 2 * LANES, cw), F32),
        ],
        compiler_params=_params(("parallel", "arbitrary"), 48),
        name="sb_attention_cached",
    )(p3, p3, p3, cache_k, cache_v, ut)


def _attn_body(*refs, tq, tk, t_new, n_new_blocks, n_past_blocks, past_unroll, new_unroll, chains):
    if n_past_blocks:
        q_ref, k_ref, v_ref, pk_ref, pv_ref, ut_ref, o_ref, kb, vt, qs, carry, acc = refs
    else:
        q_ref, k_ref, v_ref, ut_ref, o_ref, kb, vt, qs, carry, acc = refs
        pk_ref = pv_ref = None
    i = pl.program_id(2)

    def pad_rows(x):
        if x.shape[0] == tk:
            return x
        return jnp.concatenate([x, jnp.zeros((tk - x.shape[0], x.shape[1]), x.dtype)], axis=0)

    @pl.when(i == 0)
    def _():
        if n_new_blocks == 1:
            kb[...] = pad_rows(k_ref[0]).astype(BF16)
            vt[0] = pad_rows(v_ref[0]).T.astype(BF16)
        else:
            def fill(blk, _):
                rows = pl.ds(pl.multiple_of(blk * tk, tk), tk)
                kb[rows, :] = k_ref[0, rows, :].astype(BF16)
                vt[blk] = v_ref[0, rows, :].T.astype(BF16)
                return 0

            lax.fori_loop(0, n_new_blocks, fill, 0)

    q = q_ref[0] * (HEAD_DIM ** -0.5 * LOG2_E)
    n_heads = q.shape[1] // HEAD_DIM
    head_of_lane = lax.broadcasted_iota(jnp.int32, q.shape, 1) // HEAD_DIM
    zero = jnp.zeros_like(q)
    for h in range(n_heads):
        qs[h * tq:(h + 1) * tq, :] = jnp.where(head_of_lane == h, q, zero).astype(BF16)
    carry[...] = jnp.zeros_like(carry)
    acc[...] = jnp.zeros_like(acc)

    def span(blocks):
        q16 = [qs[c0:c1, :] for c0, c1, _, _ in chains]
        items = [(ci, q16[ci], kblk, vtblk[r0:r1], mask)
                 for kblk, vtblk, mask in blocks for ci, (_, _, r0, r1) in enumerate(chains)]
        run = {ci: carry[:, c0:c1] for ci, (c0, c1, _, _) in enumerate(chains)}
        pv = {}
        _sb_items(items, ut_ref[...], tk, run, pv)
        for ci, (c0, c1, r0, r1) in enumerate(chains):
            carry[:, c0:c1] = run[ci]
            acc[r0:r1, c0:c1] += pv[ci]

    q0 = i * tq
    cw = chains[0][1] - chains[0][0]
    row = lax.broadcasted_iota(jnp.int32, (tk, cw), 0)
    col = lax.broadcasted_iota(jnp.int32, (tk, cw), 1) % tq
    n_diag = max(tq // tk, 1)
    diag = []
    for m in reversed(range(n_diag)):
        blk = q0 // tk + m
        diag.append((kb[pl.ds(pl.multiple_of(blk * tk, tk), tk), :], vt[blk], (row + m * tk) < col))
    span(diag)

    def new_blocks(last, count):
        blocks = []
        for m in range(count):
            blk = last - m
            blocks.append((kb[pl.ds(pl.multiple_of(blk * tk, tk), tk), :], vt[blk], None))
        span(blocks)

    n_left = q0 // tk
    n_main = n_left // new_unroll

    def main_step(it, _):
        new_blocks(n_left - 1 - it * new_unroll, new_unroll)
        return 0

    lax.fori_loop(0, n_main, main_step, 0)
    if new_unroll > n_diag:
        def rest_step(it, _):
            new_blocks(n_left - 1 - n_main * new_unroll - it * n_diag, n_diag)
            return 0

        lax.fori_loop(0, (n_left - n_main * new_unroll) // n_diag, rest_step, 0)

    if n_past_blocks:
        def past_step(it, _):
            blocks = []
            for m in range(past_unroll):
                k0 = pl.multiple_of((n_past_blocks - 1 - it * past_unroll - m) * tk, tk)
                blocks.append((pk_ref[0, pl.ds(k0, tk), :].astype(BF16),
                               pv_ref[0, pl.ds(k0, tk), :].T.astype(BF16), None))
            span(blocks)
            return 0

        lax.fori_loop(0, n_past_blocks // past_unroll, past_step, 0)

    out_t = acc[...].T
    out = out_t[:tq]
    for h in range(1, n_heads):
        out = jnp.where(head_of_lane == h, out_t[h * tq:(h + 1) * tq], out)
    o_ref[0] = out


def _attention(p3, past_k, past_v, u2, *, n_pairs, q_blk0, k_blk0, v_blk0, tq):
    b, t, _ = p3.shape
    tk = ATT_TK
    tq = min(tq, t)
    n_new_blocks = -(-t // tk)
    if tq % LANES == 0:
        npb = 1
        chains = ((0, tq, 0, HEAD_DIM), (tq, 2 * tq, HEAD_DIM, 2 * HEAD_DIM))
    else:
        npb = max(1, (2 * LANES) // (2 * tq))
        chains = ((0, 2 * npb * tq, 0, npb * LANES),)
    lw = npb * LANES
    has_past = past_k is not None
    n_past_blocks = past_k.shape[1] // tk if has_past else 0
    in_specs = [
        pl.BlockSpec((1, tq, lw), lambda bi, p, i: (bi, i, q_blk0 // npb + p)),
        pl.BlockSpec((1, t, lw), lambda bi, p, i: (bi, 0, k_blk0 // npb + p)),
        pl.BlockSpec((1, t, lw), lambda bi, p, i: (bi, 0, v_blk0 // npb + p)),
    ]
    args = [p3, p3, p3]
    if has_past:
        pp = past_k.shape[1]
        in_specs += [pl.BlockSpec((1, pp, lw), lambda bi, p, i: (bi, 0, p))] * 2
        args += [past_k, past_v]
    in_specs.append(pl.BlockSpec(u2.shape, lambda bi, p, i: (0, 0)))
    args.append(u2)
    return pl.pallas_call(
        functools.partial(_attn_body, tq=tq, tk=tk, t_new=t, n_new_blocks=n_new_blocks, n_past_blocks=n_past_blocks,
                          past_unroll=min(4, max(n_past_blocks, 1)), new_unroll=2 * max(tq // tk, 1), chains=chains),
        out_shape=jax.ShapeDtypeStruct((b, t, n_pairs * LANES), F32),
        grid=(b, n_pairs // npb, t // tq),
        in_specs=in_specs,
        out_specs=pl.BlockSpec((1, tq, lw), lambda bi, p, i: (bi, i, p)),
        scratch_shapes=[
            pltpu.VMEM((n_new_blocks * tk, lw), BF16),
            pltpu.VMEM((n_new_blocks, lw, tk), BF16),
            pltpu.VMEM((2 * npb * tq, lw), BF16),
            pltpu.VMEM((8, 2 * npb * tq), F32),
            pltpu.VMEM((lw, 2 * npb * tq), F32),
        ],
        compiler_params=_params(("parallel", "parallel", "arbitrary"), 48),
        name="sb_attention",
    )(*args)


def _prep_body(prkv_ref, plora_ref, s_rkv_ref, s_lora_ref, mu_rkv_ref, mu_lora_ref, w0_ref, a0_ref, kk_ref, ka_ref,
               ww2_ref, wa2_ref, wg2_ref, bd_ref,
               r_o, wl_o, k_o, v_o, av_o, bv_o, g_o, c_rkv, c_lora, *, tc, width):
    t = pl.program_id(1)

    @pl.when(t == 0)
    def _():
        c_rkv[0:1, :] = s_rkv_ref[0]
        c_lora[0:1, :] = s_lora_ref[0]

    def token_mix(p, prev, mu):
        row = lax.broadcasted_iota(jnp.int32, p.shape, 0)
        shifted = jnp.where(row == 0, prev, pltpu.roll(p, 1, 0))
        return p + (shifted - p) * mu

    def rkv_seg(s):
        cs = slice(s * width, (s + 1) * width)
        p = prkv_ref[0, :, cs]
        x = token_mix(p, c_rkv[0:1, cs], mu_rkv_ref[:, cs])
        c_rkv[0:1, cs] = p[tc - 1:tc, :]
        return x

    pl_ = plora_ref[0]
    xl = token_mix(pl_, c_lora[0:1, :], mu_lora_ref[...])
    c_lora[0:1, :] = pl_[tc - 1:tc, :]

    r_o[0] = rkv_seg(0)
    v_o[0] = rkv_seg(2)
    xk = rkv_seg(1)

    dec = w0_ref[...] + _mm(jnp.tanh(xl).astype(BF16), ww2_ref[...])
    nd = -dec
    softplus = jnp.maximum(nd, 0.0) + jnp.log(1.0 + jnp.exp(-jnp.abs(nd)))
    w_log = -softplus - 0.5
    wl_o[0] = -jnp.exp(w_log)
    a = jax.nn.sigmoid(a0_ref[...] + _mm(xl.astype(BF16), wa2_ref[...]))
    g_o[0] = _mm(jax.nn.sigmoid(xl).astype(BF16), wg2_ref[...])
    kk = xk * kk_ref[...]
    k_o[0] = xk * (1.0 + (a - 1.0) * ka_ref[...])
    norm = jnp.sqrt(_head_sum2(kk * kk, bd_ref[...]))
    kk = kk / jnp.maximum(norm, 1e-12)
    av_o[0] = -kk
    bv_o[0] = kk * a


def _rwkv_prep(p3, s_rkv, s_lora, mu_rkv, mu_lora, w0, a0, k_k, k_a, ww2, wa2, wg2, bd, *, rkv_blk, lora_blk, tc):
    b, t, _ = p3.shape
    width = w0.shape[1]
    lw = mu_lora.shape[1]
    tc = min(tc, t)
    const = lambda shape: pl.BlockSpec(shape, lambda bi, ti: (0,) * len(shape))
    out_spec = pl.BlockSpec((1, tc, width), lambda bi, ti: (bi, ti, 0))
    return pl.pallas_call(
        functools.partial(_prep_body, tc=tc, width=width),
        out_shape=[jax.ShapeDtypeStruct((b, t, width), F32)] * 7,
        grid=(b, t // tc),
        in_specs=[
            pl.BlockSpec((1, tc, 3 * width), lambda bi, ti: (bi, ti, rkv_blk)),
            pl.BlockSpec((1, tc, lw), lambda bi, ti: (bi, ti, lora_blk)),
            pl.BlockSpec((1, 1, 3 * width), lambda bi, ti: (bi, 0, 0)),
            pl.BlockSpec((1, 1, lw), lambda bi, ti: (bi, 0, 0)),
            const((1, 3 * width)), const((1, lw)),
            const((1, width)), const((1, width)), const((1, width)), const((1, width)),
            const((lw, width)), const((lw, width)), const((lw, width)),
            const((width, LANES)),
        ],
        out_specs=[out_spec] * 7,
        scratch_shapes=[pltpu.VMEM((8, 3 * width), F32), pltpu.VMEM((8, lw), F32)],
        compiler_params=_params(("parallel", "arbitrary"), 48),
        name="rwkv_prep",
    )(p3, p3, s_rkv, s_lora, mu_rkv, mu_lora, w0, a0, k_k, k_a, ww2, wa2, wg2, bd)


def _scan_body(r_ref, wl_ref, k_ref, v_ref, a_ref, b_ref, s0_ref, tri_ref, y_ref, sout_ref, s_scr, *,
               n_chunks, n_pairs):
    c_len = RW_CHUNK
    t = pl.program_id(2)

    @pl.when(t == 0)
    def _():
        s_scr[...] = s0_ref[0]

    row = lax.broadcasted_iota(jnp.int32, (c_len, c_len), 0)
    col = lax.broadcasted_iota(jnp.int32, (c_len, c_len), 1)
    strict = row > col
    incl = row >= col
    lane = lax.broadcasted_iota(jnp.int32, (c_len, LANES), 1)
    first = lane < HEAD_DIM
    brow = lax.broadcasted_iota(jnp.int32, (LANES, LANES), 0)
    bcol = lax.broadcasted_iota(jnp.int32, (LANES, LANES), 1)
    same_head = (brow // HEAD_DIM) == (bcol // HEAD_DIM)
    eye = brow == bcol
    tri = tri_ref[...]
    bf = lambda x: x.astype(BF16)

    cps = [(c, p) for c in range(n_chunks) for p in range(n_pairs)]
    sls = {(c, p): (slice(c * c_len, (c + 1) * c_len), slice(p * LANES, (p + 1) * LANES)) for c, p in cps}
    tiles = {}
    for c in range(n_chunks):
        sl = slice(c * c_len, (c + 1) * c_len)
        wl = wl_ref[0, sl, :]
        hi = wl.astype(BF16)
        rem = wl - hi.astype(F32)
        mid = rem.astype(BF16)
        lo = (rem - mid.astype(F32)).astype(BF16)
        cum = _mm(tri, hi) + _mm(tri, mid) + _mm(tri, lo)
        tot = cum[c_len - 1:c_len, :]
        e_neg = jnp.exp(-cum)
        e_end = jnp.exp(tot - cum)
        av, bv, kv, vv = a_ref[0, sl, :], b_ref[0, sl, :], k_ref[0, sl, :], v_ref[0, sl, :]
        tiles[c] = dict(at=av * jnp.exp(cum - wl), rt=r_ref[0, sl, :] * jnp.exp(cum), bt=bf(bv * e_neg),
                        kt=bf(kv * e_neg), bh=bv * e_end, kh=kv * e_end, vv=vv, etot=jnp.exp(tot))
    tile = lambda name, c, p: tiles[c][name][:, p * LANES:(p + 1) * LANES]

    chains = [(c, p, h) for c, p in cps for h in range(2)]
    mbk = {}
    for c, p, h in chains:
        sel = first if h == 0 else jnp.logical_not(first)
        at, rt = tile("at", c, p), tile("rt", c, p)
        zero = jnp.zeros_like(at)
        ar = jnp.concatenate([jnp.where(sel, at, zero), jnp.where(sel, rt, zero)], axis=0).astype(BF16)
        mbk[c, p, h] = (_nt(ar, tile("bt", c, p)), _nt(ar, tile("kt", c, p)))
    m_ab, p_rb, m_ak, p_rk, tm = {}, {}, {}, {}, {}
    for ch in chains:
        mb, mk = mbk[ch]
        m_ab[ch] = jnp.where(strict, mb[:c_len], 0.0)
        p_rb[ch] = bf(jnp.where(incl, mb[c_len:], 0.0))
        m_ak[ch] = bf(jnp.where(strict, mk[:c_len], 0.0))
        p_rk[ch] = bf(jnp.where(incl, mk[c_len:], 0.0))
        tm[ch] = jnp.where(row == col, 1.0, 0.0) + jnp.where((row // 2) == (col // 2), m_ab[ch], 0.0)
    s = 2
    while s < c_len:
        off = jnp.logical_and((row // (2 * s)) == (col // (2 * s)), (row // s) != (col // s))
        half = {ch: bf(_mm(bf(tm[ch]), bf(jnp.where(off, m_ab[ch], 0.0)))) for ch in chains}
        tm = {ch: tm[ch] + _mm(half[ch], bf(tm[ch])) for ch in chains}
        s *= 2
    t16 = {ch: bf(tm[ch]) for ch in chains}
    mv = {(c, p, h): _mm(m_ak[c, p, h], bf(tile("vv", c, p))) for c, p, h in chains}
    w1 = {(c, p, h): _mm(t16[c, p, h], bf(tile("at", c, p))) for c, p, h in chains}
    w2 = {ch: _mm(t16[ch], bf(mv[ch])) for ch in chains}
    qc = {(c, p, h): tile("rt", c, p) + _mm(p_rb[c, p, h], bf(w1[c, p, h])) for c, p, h in chains}
    y1 = {(c, p, h): _mm(p_rb[c, p, h], bf(w2[c, p, h])) + _mm(p_rk[c, p, h], bf(tile("vv", c, p)))
          for c, p, h in chains}
    both = lambda d, c, p: jnp.where(first, d[c, p, 0], d[c, p, 1])
    ac_t, dc_t, qcs, y1s = {}, {}, {}, {}
    for c, p in cps:
        bh, kh, vv = tile("bh", c, p), tile("kh", c, p), tile("vv", c, p)
        w1p, w2p = both(w1, c, p), both(w2, c, p)
        a_full = jnp.where(same_head, _tn(bf(w1p), bf(bh)), 0.0) + jnp.where(eye, tile("etot", c, p), 0.0)
        ac_t[c, p] = _split2(a_full)
        dc_t[c, p] = jnp.where(same_head, _tn(bf(jnp.concatenate([w2p, vv], axis=0)),
                                              bf(jnp.concatenate([bh, kh], axis=0))), 0.0)
        qcs[c, p], y1s[c, p] = bf(both(qc, c, p)), both(y1, c, p)
    for p in range(n_pairs):
        state = s_scr[p]
        for c in range(n_chunks):
            rs, ls = sls[c, p]
            s_hi, s_lo = _split2(state)
            a_hi, a_lo = ac_t[c, p]
            y_ref[0, rs, ls] = _nt(qcs[c, p], s_hi) + y1s[c, p]
            state = _mm(s_hi, a_hi) + _mm(s_hi, a_lo) + _mm(s_lo, a_hi) + dc_t[c, p]
        s_scr[p] = state

    @pl.when(t == pl.num_programs(2) - 1)
    def _():
        sout_ref[0] = s_scr[...]


def _rwkv_scan(r, wl, k, v, av, bv, s0_bd, tri, *, tc, pairs_per_step):
    b, t, width = r.shape
    n_pairs = width // LANES
    tc = min(tc, t)
    npb = pairs_per_step
    seq = pl.BlockSpec((1, tc, npb * LANES), lambda bi, p, ti: (bi, ti, p))
    state = pl.BlockSpec((1, npb, LANES, LANES), lambda bi, p, ti: (bi, p, 0, 0))
    return pl.pallas_call(
        functools.partial(_scan_body, n_chunks=tc // RW_CHUNK, n_pairs=npb),
        out_shape=[jax.ShapeDtypeStruct((b, t, width), F32), jax.ShapeDtypeStruct(s0_bd.shape, F32)],
        grid=(b, n_pairs // npb, t // tc),
        in_specs=[seq] * 6 + [state, pl.BlockSpec(tri.shape, lambda bi, p, ti: (0, 0))],
        out_specs=[seq, state],
        scratch_shapes=[pltpu.VMEM((npb, LANES, LANES), F32)],
        compiler_params=_params(("parallel", "parallel", "arbitrary"), 32),
        name="rwkv_scan",
    )(r, wl, k, v, av, bv, s0_bd, tri)


def _out_body(h_ref, osb_ref, y_ref, r_ref, k_ref, v_ref, g_ref, gs_ref, gr_ref, lnw_ref, lnb_ref, rk_ref, bd_ref,
              wso_ref, wro_ref, wout_ref, o_ref, m_ref):
    j = pl.program_id(1)

    @pl.when(j == 0)
    def _():
        bd = bd_ref[...]
        y = y_ref[...]
        mu = _head_sum2(y, bd) * (1.0 / HEAD_DIM)
        d = y - mu
        var = _head_sum2(d * d, bd) * (1.0 / HEAD_DIM)
        yn = d * lax.rsqrt(var + GN_EPS) * lnw_ref[...] + lnb_ref[...]
        bonus = _head_sum2(r_ref[...] * k_ref[...] * rk_ref[...], bd) * v_ref[...]
        yy = ((yn + bonus) * g_ref[...]).astype(BF16)
        o_sb = _mm(osb_ref[...].astype(BF16), wso_ref[...])
        o_rw = _mm(yy, wro_ref[...])
        merged = jax.nn.sigmoid(gs_ref[...]) * o_sb + jax.nn.sigmoid(gr_ref[...]) * o_rw
        m_ref[...] = merged.astype(BF16)

    o_ref[...] = h_ref[...] + _mm(m_ref[...], wout_ref[...])


def _merge_out(h, o_sb, y, r, k, v, g, p2, lnw, lnb, rk, bd, wso, wro, wout, *, gs_blk, gr_blk, tm=256, tn=512):
    n, d = h.shape
    width = o_sb.shape[1]
    tm, tn = min(tm, n), min(tn, d)
    tok = pl.BlockSpec((tm, width), lambda i, j: (i, 0))
    const = lambda shape: pl.BlockSpec(shape, lambda i, j: (0,) * len(shape))
    return pl.pallas_call(
        _out_body,
        out_shape=jax.ShapeDtypeStruct((n, d), F32),
        grid=(n // tm, d // tn),
        in_specs=[
            pl.BlockSpec((tm, tn), lambda i, j: (i, j)),
            tok, tok, tok, tok, tok, tok,
            pl.BlockSpec((tm, d), lambda i, j: (i, gs_blk)),
            pl.BlockSpec((tm, d), lambda i, j: (i, gr_blk)),
            const((1, width)), const((1, width)), const((1, width)),
            const((width, LANES)),
            const((width, d)), const((width, d)),
            pl.BlockSpec((d, tn), lambda i, j: (0, j)),
        ],
        out_specs=pl.BlockSpec((tm, tn), lambda i, j: (i, j)),
        scratch_shapes=[pltpu.VMEM((tm, d), BF16)],
        compiler_params=_params(("parallel", "arbitrary"), 56),
        name="merge_out",
    )(h, o_sb, y, r, k, v, g, p2, p2, lnw, lnb, rk, bd, wso, wro, wout)


def _layer(x, past_k, past_v, wkv0, shift0, w, *, tq, tc, scan_pairs):
    b, t, d = x.shape
    n = b * t
    width = w["w0"].shape[1]
    h1 = _ffn(x.reshape(n, d), w["ffn1_norm"], w["ffn1_wg"], w["ffn1_wu"], w["ffn1_wd"])
    p2 = _mix(h1, w["mix_norm"], w["w_in"], w["head_gain"], w["bd"][:512, :512], n_norm_cols=2 * width)
    p3 = p2.reshape(b, t, -1)
    n_pairs = width // LANES
    if past_k is None:
        o_sb = _attention(p3, None, None, w["u2"], n_pairs=n_pairs, q_blk0=0, k_blk0=n_pairs, v_blk0=2 * n_pairs,
                          tq=tq)
    else:
        o_sb = _attention_cached(p3, past_k, past_v, w["u2"], width=width)

    lora_w = w["mu_lora"].shape[1]
    lora_blk = (6 * width + 2 * d) // lora_w
    lora_cols = w["lora_cols"]
    s_rkv = shift0[:, :, :3 * width]
    s_lora = jnp.pad(shift0[:, :, 3 * width:], ((0, 0), (0, 0), (0, lora_w - lora_cols)))
    r, wl, k, v, av, bv, g = _rwkv_prep(
        p3, s_rkv, s_lora, w["mu_rkv"], w["mu_lora"], w["w0"], w["a0"], w["k_k"], w["k_a"],
        w["ww2"], w["wa2"], w["wg2"], w["member"], rkv_blk=1, lora_blk=lora_blk, tc=tc)

    s0 = wkv0.reshape(b, n_pairs, 2, HEAD_DIM, HEAD_DIM)
    z = jnp.zeros_like(s0[:, :, 0])
    s0_bd = jnp.concatenate([jnp.concatenate([s0[:, :, 0], z], axis=-1),
                             jnp.concatenate([z, s0[:, :, 1]], axis=-1)], axis=-2)
    y, s_bd = _rwkv_scan(r, wl, k, v, av, bv, s0_bd, w["tri"], tc=tc, pairs_per_step=scan_pairs)
    wkv = jnp.stack([s_bd[:, :, :HEAD_DIM, :HEAD_DIM], s_bd[:, :, HEAD_DIM:, HEAD_DIM:]], axis=2)
    wkv = wkv.reshape(b, 2 * n_pairs, HEAD_DIM, HEAD_DIM)

    flat = lambda a: a.reshape(n, width)
    gs_blk = (6 * width) // d
    h2 = _merge_out(h1, flat(o_sb), flat(y), flat(r), flat(k), flat(v), flat(g), p2,
                    w["ln_w"], w["ln_b"], w["r_k"], w["member"], w["sb_wo"], w["rw_wo"], w["w_out"],
                    gs_blk=gs_blk, gr_blk=gs_blk + 1)
    out = _ffn(h2, w["ffn2_norm"], w["ffn2_wg"], w["ffn2_wu"], w["ffn2_wd"])

    heads = width // HEAD_DIM
    k_new = p3[:, :, width:2 * width].reshape(b, t, heads, HEAD_DIM)
    v_new = p3[:, :, 2 * width:3 * width].reshape(b, t, heads, HEAD_DIM)
    shift = jnp.concatenate([p3[:, t - 1:, 3 * width:6 * width],
                             p3[:, t - 1:, 6 * width + 2 * d:6 * width + 2 * d + lora_cols]], axis=-1)
    return out.reshape(b, t, d), k_new, v_new, wkv, shift


def _layer_weights(l, ffn1_norm, ffn1_w_gate, ffn1_w_up, ffn1_w_down, mix_norm, w_in, sb_q_norm, sb_k_norm, sb_w_o,
                   rwkv_mu, rwkv_w0, rwkv_w_w2, rwkv_a0, rwkv_w_a2, rwkv_w_g2, rwkv_k_k, rwkv_k_a, rwkv_r_k,
                   rwkv_ln_w, rwkv_ln_b, rwkv_w_o, w_out, ffn2_norm, ffn2_w_gate, ffn2_w_up, ffn2_w_down):
    d = w_in.shape[1]
    width = rwkv_w0.shape[1]
    heads = width // HEAD_DIM
    n_decay, n_iclr, n_gate = rwkv_w_w2.shape[1], rwkv_w_a2.shape[1], rwkv_w_g2.shape[1]
    lora_cols = n_decay + n_iclr + n_gate
    lora_w = -(-lora_cols // 512) * 512
    row = lambda a: a.reshape(1, -1).astype(F32)
    wi = w_in[l]
    w_in_p = jnp.concatenate([
        wi[:, :6 * width], wi[:, 6 * width + lora_cols:], wi[:, 6 * width:6 * width + lora_cols],
        jnp.zeros((d, lora_w - lora_cols), wi.dtype)], axis=1).astype(BF16)
    total = w_in_p.shape[1]
    head_gain = jnp.concatenate([jnp.tile(sb_q_norm[l], heads), jnp.tile(sb_k_norm[l], heads),
                                 jnp.ones((total - 2 * width,), F32)]).reshape(1, total)
    mu = rwkv_mu[l]

    def lora_pad(wm, r0):
        return jnp.zeros((lora_w, width), F32).at[r0:r0 + wm.shape[0]].set(wm).astype(BF16)

    hid = jnp.arange(width) // HEAD_DIM
    bd = (hid[:, None] == hid[None, :]).astype(BF16)
    member = (hid[:, None] == jnp.arange(LANES)[None, :]).astype(BF16)
    tk = ATT_TK
    ki = jnp.arange(tk)
    u2 = jnp.concatenate([(ki[None, :] > ki[:, None]).astype(BF16), jnp.ones((16, tk), BF16)], axis=0)
    ci = jnp.arange(RW_CHUNK)
    tri = (ci[:, None] >= ci[None, :]).astype(BF16)
    return {
        "ffn1_norm": row(ffn1_norm[l]), "ffn1_wg": ffn1_w_gate[l].astype(BF16), "ffn1_wu": ffn1_w_up[l].astype(BF16),
        "ffn1_wd": ffn1_w_down[l].astype(BF16),
        "ffn2_norm": row(ffn2_norm[l]), "ffn2_wg": ffn2_w_gate[l].astype(BF16), "ffn2_wu": ffn2_w_up[l].astype(BF16),
        "ffn2_wd": ffn2_w_down[l].astype(BF16),
        "mix_norm": row(mix_norm[l]), "w_in": w_in_p, "head_gain": head_gain, "bd": bd, "member": member, "u2": u2, "tri": tri,
        "mu_rkv": row(mu[:3 * width]), "mu_lora": row(jnp.pad(mu[3 * width:], (0, lora_w - lora_cols))),
        "lora_cols": lora_cols,
        "w0": row(rwkv_w0[l]), "a0": row(rwkv_a0[l]), "k_k": row(rwkv_k_k[l]), "k_a": row(rwkv_k_a[l]),
        "ww2": lora_pad(rwkv_w_w2[l], 0), "wa2": lora_pad(rwkv_w_a2[l], n_decay),
        "wg2": lora_pad(rwkv_w_g2[l], n_decay + n_iclr),
        "ln_w": row(rwkv_ln_w[l]), "ln_b": row(rwkv_ln_b[l]), "r_k": row(rwkv_r_k[l]),
        "sb_wo": sb_w_o[l].astype(BF16), "rw_wo": rwkv_w_o[l].astype(BF16), "w_out": w_out[l].astype(BF16),
    }


def kernel(x_prompt, x_sample, cache_sb_k, cache_sb_v, state_rwkv_wkv, state_rwkv_shift, ffn1_norm, ffn1_w_gate, ffn1_w_up, ffn1_w_down, mix_norm, w_in, sb_q_norm, sb_k_norm, sb_w_o, rwkv_mu, rwkv_w0, rwkv_w_w2, rwkv_a0, rwkv_w_a2, rwkv_w_g2, rwkv_k_k, rwkv_k_a, rwkv_r_k, rwkv_ln_w, rwkv_ln_b, rwkv_w_o, w_out, ffn2_norm, ffn2_w_gate, ffn2_w_up, ffn2_w_down):
    depth = w_in.shape[0]
    yp, ys = x_prompt, x_sample
    bp = x_prompt.shape[0]
    width = rwkv_w0.shape[1]
    heads = width // HEAD_DIM
    rw_cols = state_rwkv_shift.shape[-1]
    outs = [[] for _ in range(8)]
    for l in range(depth):
        w = _layer_weights(l, ffn1_norm, ffn1_w_gate, ffn1_w_up, ffn1_w_down, mix_norm, w_in, sb_q_norm, sb_k_norm,
                           sb_w_o, rwkv_mu, rwkv_w0, rwkv_w_w2, rwkv_a0, rwkv_w_a2, rwkv_w_g2, rwkv_k_k, rwkv_k_a,
                           rwkv_r_k, rwkv_ln_w, rwkv_ln_b, rwkv_w_o, w_out, ffn2_norm, ffn2_w_gate, ffn2_w_up,
                           ffn2_w_down)
        wkv_zero = jnp.zeros((bp, heads, HEAD_DIM, HEAD_DIM), F32)
        shift_zero = jnp.zeros((bp, 1, rw_cols), F32)
        yp, kp, vp, wkvp, shp = _layer(yp, None, None, wkv_zero, shift_zero, w, tq=512, tc=256, scan_pairs=4)
        ys, kn, vn, wkvn, shn = _layer(ys, cache_sb_k[l], cache_sb_v[l], state_rwkv_wkv[l], state_rwkv_shift[l], w, tq=64, tc=64, scan_pairs=8)
        for lst, val in zip(outs, (kp, vp, wkvp, shp, kn, vn, wkvn, shn)):
            lst.append(val)
    return (yp, ys) + tuple(jnp.stack(o) for o in outs)
```

```python
import functools

import jax
import jax.numpy as jnp
from jax import lax
from jax.experimental import pallas as pl
from jax.experimental.pallas import tpu as pltpu

F32 = jnp.float32
BF16 = jnp.bfloat16

HEAD_DIM = 64
LANES = 128
NORM_EPS = 1e-6
GN_EPS = 64e-5
RW_CHUNK = 64
ATT_TK = 256

MIB = 1024 * 1024
LOG2_E = 1.4426950408889634
SIGN_BIT = 0x80000000
ATT_SKEW = 2


def _nt(x, y):
    return lax.dot_general(x, y, (((1,), (1,)), ((), ())), preferred_element_type=F32)


def _tn(x, y):
    return lax.dot_general(x, y, (((0,), (0,)), ((), ())), preferred_element_type=F32)


def _mm(x, y):
    return jnp.dot(x, y, preferred_element_type=F32)


def _neg_abs(x):
    return lax.bitcast_convert_type(lax.bitcast_convert_type(x, jnp.uint32) | jnp.uint32(SIGN_BIT), F32)


def _split2(x):
    hi = x.astype(BF16)
    lo = (x - hi.astype(F32)).astype(BF16)
    return hi, lo


def _head_sum(x, bd):
    hi, lo = _split2(x)
    return _mm(hi, bd) + _mm(lo, bd)


def _head_sum2(x, member):
    hi, lo = _split2(x)
    shi, slo = _split2(_mm(hi, member) + _mm(lo, member))
    return _nt(shi, member) + _nt(slo, member)


def _rms(x, g):
    ms = jnp.mean(x * x, axis=-1, keepdims=True)
    return x * lax.rsqrt(ms + NORM_EPS) * g


def _params(sem, vmem_mib):
    return pltpu.CompilerParams(dimension_semantics=sem, vmem_limit_bytes=vmem_mib * MIB)


def _ffn_body(x_ref, g_ref, wg_ref, wu_ref, wd_ref, o_ref, n_ref, acc_ref):
    f = pl.program_id(1)

    @pl.when(f == 0)
    def _():
        n_ref[...] = _rms(x_ref[...], g_ref[...]).astype(BF16)
        acc_ref[...] = jnp.zeros_like(acc_ref)

    n = n_ref[...]
    g = _mm(n, wg_ref[...])
    u = _mm(n, wu_ref[...])
    a = (g * jax.nn.sigmoid(g) * u).astype(BF16)
    acc_ref[...] += _mm(a, wd_ref[...])

    @pl.when(f == pl.num_programs(1) - 1)
    def _():
        o_ref[...] = x_ref[...] + 0.5 * acc_ref[...]


def _ffn(x, g, wg, wu, wd, *, tm=512, tf=512):
    n, d = x.shape
    ff = wg.shape[1]
    tm, tf = min(tm, n), min(tf, ff)
    return pl.pallas_call(
        _ffn_body,
        out_shape=jax.ShapeDtypeStruct((n, d), F32),
        grid=(n // tm, ff // tf),
        in_specs=[
            pl.BlockSpec((tm, d), lambda i, f: (i, 0)),
            pl.BlockSpec((1, d), lambda i, f: (0, 0)),
            pl.BlockSpec((d, tf), lambda i, f: (0, f)),
            pl.BlockSpec((d, tf), lambda i, f: (0, f)),
            pl.BlockSpec((tf, d), lambda i, f: (f, 0)),
        ],
        out_specs=pl.BlockSpec((tm, d), lambda i, f: (i, 0)),
        scratch_shapes=[pltpu.VMEM((tm, d), BF16), pltpu.VMEM((tm, d), F32)],
        compiler_params=_params(("parallel", "arbitrary"), 48),
        name="ffn",
    )(x, g, wg, wu, wd)


def _mix_body(h_ref, g_ref, w_ref, hg_ref, bd_ref, o_ref, n_ref, *, n_norm_tiles):
    j = pl.program_id(1)

    @pl.when(j == 0)
    def _():
        n_ref[...] = _rms(h_ref[...], g_ref[...]).astype(BF16)

    p = _mm(n_ref[...], w_ref[...])

    @pl.when(j < n_norm_tiles)
    def _():
        ms = _head_sum(p * p, bd_ref[...]) * (1.0 / HEAD_DIM)
        o_ref[...] = p * lax.rsqrt(ms + NORM_EPS) * hg_ref[...]

    @pl.when(j >= n_norm_tiles)
    def _():
        o_ref[...] = p


def _mix(h, g, w, hgain, bd, *, n_norm_cols, tm=1024, tn=512):
    n, d = h.shape
    cols = w.shape[1]
    tm = min(tm, n)
    return pl.pallas_call(
        functools.partial(_mix_body, n_norm_tiles=n_norm_cols // tn),
        out_shape=jax.ShapeDtypeStruct((n, cols), F32),
        grid=(n // tm, cols // tn),
        in_specs=[
            pl.BlockSpec((tm, d), lambda i, j: (i, 0)),
            pl.BlockSpec((1, d), lambda i, j: (0, 0)),
            pl.BlockSpec((d, tn), lambda i, j: (0, j)),
            pl.BlockSpec((1, tn), lambda i, j: (0, j)),
            pl.BlockSpec((tn, tn), lambda i, j: (0, 0)),
        ],
        out_specs=pl.BlockSpec((tm, tn), lambda i, j: (i, j)),
        scratch_shapes=[pltpu.VMEM((tm, d), BF16)],
        compiler_params=_params(("parallel", "arbitrary"), 48),
        name="mix",
    )(h, g, w, hgain, bd)


def _sb_items(items, ut, tk, run, pv):
    zs, parts = {}, {}

    def scores(n):
        _, q16, kblk, _, _ = items[n]
        zs[n] = _nt(kblk, q16)

    def keep_sums(n):
        mask = items[n][4]
        z = zs.pop(n)
        lp = jnp.log(1.0 + jnp.exp2(_neg_abs(z))) * LOG2_E
        log_beta = jnp.minimum(z, 0.0) - lp
        log_keep = log_beta - z
        if mask is not None:
            log_keep = jnp.where(mask, log_keep, 0.0)
        parts[n] = (log_beta, _mm(ut, log_keep.astype(BF16)))

    def weigh(n):
        ci, _, _, vt_rows, mask = items[n]
        log_beta, ext = parts.pop(n)
        w = jnp.exp2(log_beta + (ext[:tk] + run[ci][0:1]))
        if mask is not None:
            w = jnp.where(mask, w, 0.0)
        d = _mm(vt_rows, w.astype(BF16))
        pv[ci] = d if ci not in pv else pv[ci] + d
        run[ci] = run[ci] + ext[tk:tk + 8]

    for step in range(len(items) + 2 * ATT_SKEW):
        if step < len(items):
            scores(step)
        if 0 <= step - ATT_SKEW < len(items):
            keep_sums(step - ATT_SKEW)
        if 0 <= step - 2 * ATT_SKEW < len(items):
            weigh(step - 2 * ATT_SKEW)


def _sb_rows(items, ue, tk, run, pv):
    zs, parts = {}, {}

    def scores(n):
        _, qh, kth, _, _ = items[n]
        zs[n] = jnp.concatenate([_mm(q, kt) for q, kt in zip(qh, kth)], axis=0)

    def keep_sums(n):
        mask = items[n][4]
        z = zs.pop(n)
        lp = jnp.log(1.0 + jnp.exp2(_neg_abs(z))) * LOG2_E
        log_beta = jnp.minimum(z, 0.0) - lp
        log_keep = log_beta - z
        if mask is not None:
            log_keep = jnp.where(mask, log_keep, 0.0)
        parts[n] = (log_beta, _mm(log_keep.astype(BF16), ue))

    def weigh(n):
        ci, qh, _, vth, mask = items[n]
        log_beta, ext = parts.pop(n)
        tail = jnp.concatenate([ext[:, c0:c0 + LANES] + run[ci] for c0 in range(0, tk, LANES)], axis=1)
        w = jnp.exp2(log_beta + tail)
        if mask is not None:
            w = jnp.where(mask, w, 0.0)
        w = w.astype(BF16)
        tq = qh[0].shape[0]
        for h, vt in enumerate(vth):
            d = _nt(w[h * tq:(h + 1) * tq], vt)
            pv[ci, h] = d if (ci, h) not in pv else pv[ci, h] + d
        run[ci] = run[ci] + ext[:, tk:]

    for step in range(len(items) + 2 * ATT_SKEW):
        if step < len(items):
            scores(step)
        if 0 <= step - ATT_SKEW < len(items):
            keep_sums(step - ATT_SKEW)
        if 0 <= step - 2 * ATT_SKEW < len(items):
            weigh(step - 2 * ATT_SKEW)


def _attn_cached_body(q_ref, k_ref, v_ref, ckt_ref, cvt_ref, ue_ref, o_ref, qs, carry, acc, *,
                      tq, tk, n_heads, chunk_blocks, per_chain):
    j = pl.program_id(1)
    n_chains = n_heads // per_chain
    ue = ue_ref[...]
    heads_of = lambda c: range(c * per_chain, (c + 1) * per_chain)

    def run_items(items):
        run = {c: carry[c] for c in range(n_chains)}
        pv = {}
        _sb_rows(items, ue, tk, run, pv)
        for c in range(n_chains):
            carry[c] = run[c]
            for i, h in enumerate(heads_of(c)):
                acc[h] += pv[c, i]

    @pl.when(j == 0)
    def _():
        carry[...] = jnp.zeros_like(carry)
        acc[...] = jnp.zeros_like(acc)
        for h in range(n_heads):
            qs[h] = (q_ref[0, :, h * HEAD_DIM:(h + 1) * HEAD_DIM] * (HEAD_DIM ** -0.5 * LOG2_E)).astype(BF16)
        row = lax.broadcasted_iota(jnp.int32, (per_chain * tq, tk), 0) % tq
        col = lax.broadcasted_iota(jnp.int32, (per_chain * tq, tk), 1)
        pad = jnp.zeros((HEAD_DIM, tk - tq), F32)
        items = []
        for c in range(n_chains):
            def new_t(ref):
                return [jnp.concatenate([ref[0, :, h * HEAD_DIM:(h + 1) * HEAD_DIM].T, pad], axis=1).astype(BF16)
                        for h in heads_of(c)]
            items.append((c, [qs[h] for h in heads_of(c)], new_t(k_ref), new_t(v_ref), col < row))
        run_items(items)

    items = []
    for m in reversed(range(chunk_blocks)):
        keys = slice(m * tk, (m + 1) * tk)
        for c in range(n_chains):
            items.append((c, [qs[h] for h in heads_of(c)],
                          [ckt_ref[0, h, :, keys].astype(BF16) for h in heads_of(c)],
                          [cvt_ref[0, h, :, keys].astype(BF16) for h in heads_of(c)], None))
    run_items(items)

    @pl.when(j == pl.num_programs(1) - 1)
    def _():
        o_ref[0] = jnp.concatenate([acc[h] for h in range(n_heads)], axis=1)


def _attention_cached(p3, cache_k, cache_v, ue, *, width, chunk_keys=512, per_chain=4):
    b, t, _ = p3.shape
    tk = ATT_TK
    _, p_len, n_heads, _ = cache_k.shape
    chunk_keys = min(chunk_keys, p_len)
    n_chunks = p_len // chunk_keys
    ckt = jnp.transpose(cache_k, (0, 2, 3, 1))
    cvt = jnp.transpose(cache_v, (0, 2, 3, 1))
    cache_spec = pl.BlockSpec((1, n_heads, HEAD_DIM, chunk_keys), lambda bi, j: (bi, 0, 0, n_chunks - 1 - j))
    return pl.pallas_call(
        functools.partial(_attn_cached_body, tq=t, tk=tk, n_heads=n_heads, chunk_blocks=chunk_keys // tk,
                          per_chain=per_chain),
        out_shape=jax.ShapeDtypeStruct((b, t, width), F32),
        grid=(b, n_chunks),
        in_specs=[
            pl.BlockSpec((1, t, width), lambda bi, j: (bi, 0, 0)),
            pl.BlockSpec((1, t, width), lambda bi, j: (bi, 0, 1)),
            pl.BlockSpec((1, t, width), lambda bi, j: (bi, 0, 2)),
            cache_spec, cache_spec,
            pl.BlockSpec(ue.shape, lambda bi, j: (0, 0)),
        ],
        out_specs=pl.BlockSpec((1, t, width), lambda bi, j: (bi, 0, 0)),
        scratch_shapes=[
            pltpu.VMEM((n_heads, t, HEAD_DIM), BF16),
            pltpu.VMEM((n_heads // per_chain, per_chain * t, LANES), F32),
            pltpu.VMEM((n_heads, t, HEAD_DIM), F32),
        ],
        compiler_params=_params(("parallel", "arbitrary"), 48),
        name="sb_attention_cached",
    )(p3, p3, p3, ckt, cvt, ue)


def _attn_body(*refs, tq, tk, t_new, n_new_blocks, n_past_blocks, past_unroll, new_unroll, chains):
    if n_past_blocks:
        q_ref, k_ref, v_ref, pk_ref, pv_ref, ut_ref, o_ref, kb, vt, qs, carry, acc = refs
    else:
        q_ref, k_ref, v_ref, ut_ref, o_ref, kb, vt, qs, carry, acc = refs
        pk_ref = pv_ref = None
    i = pl.program_id(2)

    def pad_rows(x):
        if x.shape[0] == tk:
            return x
        return jnp.concatenate([x, jnp.zeros((tk - x.shape[0], x.shape[1]), x.dtype)], axis=0)

    @pl.when(i == 0)
    def _():
        if n_new_blocks == 1:
            kb[...] = pad_rows(k_ref[0]).astype(BF16)
            vt[0] = pad_rows(v_ref[0]).T.astype(BF16)
        else:
            def fill(blk, _):
                rows = pl.ds(pl.multiple_of(blk * tk, tk), tk)
                kb[rows, :] = k_ref[0, rows, :].astype(BF16)
                vt[blk] = v_ref[0, rows, :].T.astype(BF16)
                return 0

            lax.fori_loop(0, n_new_blocks, fill, 0)

    q = q_ref[0] * (HEAD_DIM ** -0.5 * LOG2_E)
    n_heads = q.shape[1] // HEAD_DIM
    head_of_lane = lax.broadcasted_iota(jnp.int32, q.shape, 1) // HEAD_DIM
    zero = jnp.zeros_like(q)
    for h in range(n_heads):
        qs[h * tq:(h + 1) * tq, :] = jnp.where(head_of_lane == h, q, zero).astype(BF16)
    carry[...] = jnp.zeros_like(carry)
    acc[...] = jnp.zeros_like(acc)

    def span(blocks):
        q16 = [qs[c0:c1, :] for c0, c1, _, _ in chains]
        items = [(ci, q16[ci], kblk, vtblk[r0:r1], mask)
                 for kblk, vtblk, mask in blocks for ci, (_, _, r0, r1) in enumerate(chains)]
        run = {ci: carry[:, c0:c1] for ci, (c0, c1, _, _) in enumerate(chains)}
        pv = {}
        _sb_items(items, ut_ref[...], tk, run, pv)
        for ci, (c0, c1, r0, r1) in enumerate(chains):
            carry[:, c0:c1] = run[ci]
            acc[r0:r1, c0:c1] += pv[ci]

    q0 = i * tq
    cw = chains[0][1] - chains[0][0]
    row = lax.broadcasted_iota(jnp.int32, (tk, cw), 0)
    col = lax.broadcasted_iota(jnp.int32, (tk, cw), 1) % tq
    n_diag = max(tq // tk, 1)
    diag = []
    for m in reversed(range(n_diag)):
        blk = q0 // tk + m
        diag.append((kb[pl.ds(pl.multiple_of(blk * tk, tk), tk), :], vt[blk], (row + m * tk) < col))
    span(diag)

    def new_blocks(last, count):
        blocks = []
        for m in range(count):
            blk = last - m
            blocks.append((kb[pl.ds(pl.multiple_of(blk * tk, tk), tk), :], vt[blk], None))
        span(blocks)

    n_left = q0 // tk
    n_main = n_left // new_unroll

    def main_step(it, _):
        new_blocks(n_left - 1 - it * new_unroll, new_unroll)
        return 0

    lax.fori_loop(0, n_main, main_step, 0)
    if new_unroll > n_diag:
        def rest_step(it, _):
            new_blocks(n_left - 1 - n_main * new_unroll - it * n_diag, n_diag)
            return 0

        lax.fori_loop(0, (n_left - n_main * new_unroll) // n_diag, rest_step, 0)

    if n_past_blocks:
        def past_step(it, _):
            blocks = []
            for m in range(past_unroll):
                k0 = pl.multiple_of((n_past_blocks - 1 - it * past_unroll - m) * tk, tk)
                blocks.append((pk_ref[0, pl.ds(k0, tk), :].astype(BF16),
                               pv_ref[0, pl.ds(k0, tk), :].T.astype(BF16), None))
            span(blocks)
            return 0

        lax.fori_loop(0, n_past_blocks // past_unroll, past_step, 0)

    out_t = acc[...].T
    out = out_t[:tq]
    for h in range(1, n_heads):
        out = jnp.where(head_of_lane == h, out_t[h * tq:(h + 1) * tq], out)
    o_ref[0] = out


def _attention(p3, past_k, past_v, u2, *, n_pairs, q_blk0, k_blk0, v_blk0, tq):
    b, t, _ = p3.shape
    tk = ATT_TK
    tq = min(tq, t)
    n_new_blocks = -(-t // tk)
    if tq % LANES == 0:
        npb = 1
        chains = ((0, tq, 0, HEAD_DIM), (tq, 2 * tq, HEAD_DIM, 2 * HEAD_DIM))
    else:
        npb = max(1, (2 * LANES) // (2 * tq))
        chains = ((0, 2 * npb * tq, 0, npb * LANES),)
    lw = npb * LANES
    has_past = past_k is not None
    n_past_blocks = past_k.shape[1] // tk if has_past else 0
    in_specs = [
        pl.BlockSpec((1, tq, lw), lambda bi, p, i: (bi, i, q_blk0 // npb + p)),
        pl.BlockSpec((1, t, lw), lambda bi, p, i: (bi, 0, k_blk0 // npb + p)),
        pl.BlockSpec((1, t, lw), lambda bi, p, i: (bi, 0, v_blk0 // npb + p)),
    ]
    args = [p3, p3, p3]
    if has_past:
        pp = past_k.shape[1]
        in_specs += [pl.BlockSpec((1, pp, lw), lambda bi, p, i: (bi, 0, p))] * 2
        args += [past_k, past_v]
    in_specs.append(pl.BlockSpec(u2.shape, lambda bi, p, i: (0, 0)))
    args.append(u2)
    return pl.pallas_call(
        functools.partial(_attn_body, tq=tq, tk=tk, t_new=t, n_new_blocks=n_new_blocks, n_past_blocks=n_past_blocks,
                          past_unroll=min(4, max(n_past_blocks, 1)), new_unroll=2 * max(tq // tk, 1), chains=chains),
        out_shape=jax.ShapeDtypeStruct((b, t, n_pairs * LANES), F32),
        grid=(b, n_pairs // npb, t // tq),
        in_specs=in_specs,
        out_specs=pl.BlockSpec((1, tq, lw), lambda bi, p, i: (bi, i, p)),
        scratch_shapes=[
            pltpu.VMEM((n_new_blocks * tk, lw), BF16),
            pltpu.VMEM((n_new_blocks, lw, tk), BF16),
            pltpu.VMEM((2 * npb * tq, lw), BF16),
            pltpu.VMEM((8, 2 * npb * tq), F32),
            pltpu.VMEM((lw, 2 * npb * tq), F32),
        ],
        compiler_params=_params(("parallel", "parallel", "arbitrary"), 48),
        name="sb_attention",
    )(*args)


def _prep_body(prkv_ref, plora_ref, s_rkv_ref, s_lora_ref, mu_rkv_ref, mu_lora_ref, w0_ref, a0_ref, kk_ref, ka_ref,
               ww2_ref, wa2_ref, wg2_ref, bd_ref,
               r_o, wl_o, k_o, v_o, av_o, bv_o, g_o, c_rkv, c_lora, *, tc, width):
    t = pl.program_id(1)

    @pl.when(t == 0)
    def _():
        c_rkv[0:1, :] = s_rkv_ref[0]
        c_lora[0:1, :] = s_lora_ref[0]

    def token_mix(p, prev, mu):
        row = lax.broadcasted_iota(jnp.int32, p.shape, 0)
        shifted = jnp.where(row == 0, prev, pltpu.roll(p, 1, 0))
        return p + (shifted - p) * mu

    def rkv_seg(s):
        cs = slice(s * width, (s + 1) * width)
        p = prkv_ref[0, :, cs]
        x = token_mix(p, c_rkv[0:1, cs], mu_rkv_ref[:, cs])
        c_rkv[0:1, cs] = p[tc - 1:tc, :]
        return x

    pl_ = plora_ref[0]
    xl = token_mix(pl_, c_lora[0:1, :], mu_lora_ref[...])
    c_lora[0:1, :] = pl_[tc - 1:tc, :]

    r_o[0] = rkv_seg(0)
    v_o[0] = rkv_seg(2)
    xk = rkv_seg(1)

    dec = w0_ref[...] + _mm(jnp.tanh(xl).astype(BF16), ww2_ref[...])
    nd = -dec
    softplus = jnp.maximum(nd, 0.0) + jnp.log(1.0 + jnp.exp(-jnp.abs(nd)))
    w_log = -softplus - 0.5
    wl_o[0] = -jnp.exp(w_log)
    a = jax.nn.sigmoid(a0_ref[...] + _mm(xl.astype(BF16), wa2_ref[...]))
    g_o[0] = _mm(jax.nn.sigmoid(xl).astype(BF16), wg2_ref[...])
    kk = xk * kk_ref[...]
    k_o[0] = xk * (1.0 + (a - 1.0) * ka_ref[...])
    norm = jnp.sqrt(_head_sum2(kk * kk, bd_ref[...]))
    kk = kk / jnp.maximum(norm, 1e-12)
    av_o[0] = -kk
    bv_o[0] = kk * a


def _rwkv_prep(p3, s_rkv, s_lora, mu_rkv, mu_lora, w0, a0, k_k, k_a, ww2, wa2, wg2, bd, *, rkv_blk, lora_blk, tc):
    b, t, _ = p3.shape
    width = w0.shape[1]
    lw = mu_lora.shape[1]
    tc = min(tc, t)
    const = lambda shape: pl.BlockSpec(shape, lambda bi, ti: (0,) * len(shape))
    out_spec = pl.BlockSpec((1, tc, width), lambda bi, ti: (bi, ti, 0))
    return pl.pallas_call(
        functools.partial(_prep_body, tc=tc, width=width),
        out_shape=[jax.ShapeDtypeStruct((b, t, width), F32)] * 7,
        grid=(b, t // tc),
        in_specs=[
            pl.BlockSpec((1, tc, 3 * width), lambda bi, ti: (bi, ti, rkv_blk)),
            pl.BlockSpec((1, tc, lw), lambda bi, ti: (bi, ti, lora_blk)),
            pl.BlockSpec((1, 1, 3 * width), lambda bi, ti: (bi, 0, 0)),
            pl.BlockSpec((1, 1, lw), lambda bi, ti: (bi, 0, 0)),
            const((1, 3 * width)), const((1, lw)),
            const((1, width)), const((1, width)), const((1, width)), const((1, width)),
            const((lw, width)), const((lw, width)), const((lw, width)),
            const((width, LANES)),
        ],
        out_specs=[out_spec] * 7,
        scratch_shapes=[pltpu.VMEM((8, 3 * width), F32), pltpu.VMEM((8, lw), F32)],
        compiler_params=_params(("parallel", "arbitrary"), 48),
        name="rwkv_prep",
    )(p3, p3, s_rkv, s_lora, mu_rkv, mu_lora, w0, a0, k_k, k_a, ww2, wa2, wg2, bd)


def _scan_body(r_ref, wl_ref, k_ref, v_ref, a_ref, b_ref, s0_ref, tri_ref, y_ref, sout_ref, s_scr, *,
               n_chunks, n_pairs):
    c_len = RW_CHUNK
    t = pl.program_id(2)

    @pl.when(t == 0)
    def _():
        s_scr[...] = s0_ref[0]

    row = lax.broadcasted_iota(jnp.int32, (c_len, c_len), 0)
    col = lax.broadcasted_iota(jnp.int32, (c_len, c_len), 1)
    strict = row > col
    incl = row >= col
    lane = lax.broadcasted_iota(jnp.int32, (c_len, LANES), 1)
    first = lane < HEAD_DIM
    brow = lax.broadcasted_iota(jnp.int32, (LANES, LANES), 0)
    bcol = lax.broadcasted_iota(jnp.int32, (LANES, LANES), 1)
    same_head = (brow // HEAD_DIM) == (bcol // HEAD_DIM)
    eye = brow == bcol
    tri = tri_ref[...]
    bf = lambda x: x.astype(BF16)

    cps = [(c, p) for c in range(n_chunks) for p in range(n_pairs)]
    sls = {(c, p): (slice(c * c_len, (c + 1) * c_len), slice(p * LANES, (p + 1) * LANES)) for c, p in cps}
    tiles = {}
    for c in range(n_chunks):
        sl = slice(c * c_len, (c + 1) * c_len)
        wl = wl_ref[0, sl, :]
        hi = wl.astype(BF16)
        rem = wl - hi.astype(F32)
        mid = rem.astype(BF16)
        lo = (rem - mid.astype(F32)).astype(BF16)
        cum = _mm(tri, hi) + _mm(tri, mid) + _mm(tri, lo)
        tot = cum[c_len - 1:c_len, :]
        e_neg = jnp.exp(-cum)
        e_end = jnp.exp(tot - cum)
        av, bv, kv, vv = a_ref[0, sl, :], b_ref[0, sl, :], k_ref[0, sl, :], v_ref[0, sl, :]
        tiles[c] = dict(at=av * jnp.exp(cum - wl), rt=r_ref[0, sl, :] * jnp.exp(cum), bt=bf(bv * e_neg),
                        kt=bf(kv * e_neg), bh=bv * e_end, kh=kv * e_end, vv=vv, etot=jnp.exp(tot))
    tile = lambda name, c, p: tiles[c][name][:, p * LANES:(p + 1) * LANES]

    chains = [(c, p, h) for c, p in cps for h in range(2)]
    mbk = {}
    for c, p, h in chains:
        sel = first if h == 0 else jnp.logical_not(first)
        at, rt = tile("at", c, p), tile("rt", c, p)
        zero = jnp.zeros_like(at)
        ar = jnp.concatenate([jnp.where(sel, at, zero), jnp.where(sel, rt, zero)], axis=0).astype(BF16)
        mbk[c, p, h] = (_nt(ar, tile("bt", c, p)), _nt(ar, tile("kt", c, p)))
    m_ab, p_rb, m_ak, p_rk, tm = {}, {}, {}, {}, {}
    for ch in chains:
        mb, mk = mbk[ch]
        m_ab[ch] = jnp.where(strict, mb[:c_len], 0.0)
        p_rb[ch] = bf(jnp.where(incl, mb[c_len:], 0.0))
        m_ak[ch] = bf(jnp.where(strict, mk[:c_len], 0.0))
        p_rk[ch] = bf(jnp.where(incl, mk[c_len:], 0.0))
        tm[ch] = jnp.where(row == col, 1.0, 0.0) + jnp.where((row // 2) == (col // 2), m_ab[ch], 0.0)
    s = 2
    while s < c_len:
        off = jnp.logical_and((row // (2 * s)) == (col // (2 * s)), (row // s) != (col // s))
        half = {ch: bf(_mm(bf(tm[ch]), bf(jnp.where(off, m_ab[ch], 0.0)))) for ch in chains}
        tm = {ch: tm[ch] + _mm(half[ch], bf(tm[ch])) for ch in chains}
        s *= 2
    t16 = {ch: bf(tm[ch]) for ch in chains}
    mv = {(c, p, h): _mm(m_ak[c, p, h], bf(tile("vv", c, p))) for c, p, h in chains}
    w1 = {(c, p, h): _mm(t16[c, p, h], bf(tile("at", c, p))) for c, p, h in chains}
    w2 = {ch: _mm(t16[ch], bf(mv[ch])) for ch in chains}
    qc = {(c, p, h): tile("rt", c, p) + _mm(p_rb[c, p, h], bf(w1[c, p, h])) for c, p, h in chains}
    y1 = {(c, p, h): _mm(p_rb[c, p, h], bf(w2[c, p, h])) + _mm(p_rk[c, p, h], bf(tile("vv", c, p)))
          for c, p, h in chains}
    both = lambda d, c, p: jnp.where(first, d[c, p, 0], d[c, p, 1])
    ac_t, dc_t, qcs, y1s = {}, {}, {}, {}
    for c, p in cps:
        bh, kh, vv = tile("bh", c, p), tile("kh", c, p), tile("vv", c, p)
        w1p, w2p = both(w1, c, p), both(w2, c, p)
        a_full = jnp.where(same_head, _tn(bf(w1p), bf(bh)), 0.0) + jnp.where(eye, tile("etot", c, p), 0.0)
        ac_t[c, p] = _split2(a_full)
        dc_t[c, p] = jnp.where(same_head, _tn(bf(jnp.concatenate([w2p, vv], axis=0)),
                                              bf(jnp.concatenate([bh, kh], axis=0))), 0.0)
        qcs[c, p], y1s[c, p] = bf(both(qc, c, p)), both(y1, c, p)
    for p in range(n_pairs):
        state = s_scr[p]
        for c in range(n_chunks):
            rs, ls = sls[c, p]
            s_hi, s_lo = _split2(state)
            a_hi, a_lo = ac_t[c, p]
            y_ref[0, rs, ls] = _nt(qcs[c, p], s_hi) + y1s[c, p]
            state = _mm(s_hi, a_hi) + _mm(s_hi, a_lo) + _mm(s_lo, a_hi) + dc_t[c, p]
        s_scr[p] = state

    @pl.when(t == pl.num_programs(2) - 1)
    def _():
        sout_ref[0] = s_scr[...]


def _rwkv_scan(r, wl, k, v, av, bv, s0_bd, tri, *, tc, pairs_per_step):
    b, t, width = r.shape
    n_pairs = width // LANES
    tc = min(tc, t)
    npb = pairs_per_step
    seq = pl.BlockSpec((1, tc, npb * LANES), lambda bi, p, ti: (bi, ti, p))
    state = pl.BlockSpec((1, npb, LANES, LANES), lambda bi, p, ti: (bi, p, 0, 0))
    return pl.pallas_call(
        functools.partial(_scan_body, n_chunks=tc // RW_CHUNK, n_pairs=npb),
        out_shape=[jax.ShapeDtypeStruct((b, t, width), F32), jax.ShapeDtypeStruct(s0_bd.shape, F32)],
        grid=(b, n_pairs // npb, t // tc),
        in_specs=[seq] * 6 + [state, pl.BlockSpec(tri.shape, lambda bi, p, ti: (0, 0))],
        out_specs=[seq, state],
        scratch_shapes=[pltpu.VMEM((npb, LANES, LANES), F32)],
        compiler_params=_params(("parallel", "parallel", "arbitrary"), 32),
        name="rwkv_scan",
    )(r, wl, k, v, av, bv, s0_bd, tri)


def _out_body(h_ref, osb_ref, y_ref, r_ref, k_ref, v_ref, g_ref, gs_ref, gr_ref, lnw_ref, lnb_ref, rk_ref, bd_ref,
              wso_ref, wro_ref, wout_ref, o_ref, m_ref):
    j = pl.program_id(1)

    @pl.when(j == 0)
    def _():
        bd = bd_ref[...]
        y = y_ref[...]
        mu = _head_sum2(y, bd) * (1.0 / HEAD_DIM)
        d = y - mu
        var = _head_sum2(d * d, bd) * (1.0 / HEAD_DIM)
        yn = d * lax.rsqrt(var + GN_EPS) * lnw_ref[...] + lnb_ref[...]
        bonus = _head_sum2(r_ref[...] * k_ref[...] * rk_ref[...], bd) * v_ref[...]
        yy = ((yn + bonus) * g_ref[...]).astype(BF16)
        o_sb = _mm(osb_ref[...].astype(BF16), wso_ref[...])
        o_rw = _mm(yy, wro_ref[...])
        merged = jax.nn.sigmoid(gs_ref[...]) * o_sb + jax.nn.sigmoid(gr_ref[...]) * o_rw
        m_ref[...] = merged.astype(BF16)

    o_ref[...] = h_ref[...] + _mm(m_ref[...], wout_ref[...])


def _merge_out(h, o_sb, y, r, k, v, g, p2, lnw, lnb, rk, bd, wso, wro, wout, *, gs_blk, gr_blk, tm=256, tn=512):
    n, d = h.shape
    width = o_sb.shape[1]
    tm, tn = min(tm, n), min(tn, d)
    tok = pl.BlockSpec((tm, width), lambda i, j: (i, 0))
    const = lambda shape: pl.BlockSpec(shape, lambda i, j: (0,) * len(shape))
    return pl.pallas_call(
        _out_body,
        out_shape=jax.ShapeDtypeStruct((n, d), F32),
        grid=(n // tm, d // tn),
        in_specs=[
            pl.BlockSpec((tm, tn), lambda i, j: (i, j)),
            tok, tok, tok, tok, tok, tok,
            pl.BlockSpec((tm, d), lambda i, j: (i, gs_blk)),
            pl.BlockSpec((tm, d), lambda i, j: (i, gr_blk)),
            const((1, width)), const((1, width)), const((1, width)),
            const((width, LANES)),
            const((width, d)), const((width, d)),
            pl.BlockSpec((d, tn), lambda i, j: (0, j)),
        ],
        out_specs=pl.BlockSpec((tm, tn), lambda i, j: (i, j)),
        scratch_shapes=[pltpu.VMEM((tm, d), BF16)],
        compiler_params=_params(("parallel", "arbitrary"), 56),
        name="merge_out",
    )(h, o_sb, y, r, k, v, g, p2, p2, lnw, lnb, rk, bd, wso, wro, wout)


def _layer(x, past_k, past_v, wkv0, shift0, w, *, tq, tc, scan_pairs):
    b, t, d = x.shape
    n = b * t
    width = w["w0"].shape[1]
    h1 = _ffn(x.reshape(n, d), w["ffn1_norm"], w["ffn1_wg"], w["ffn1_wu"], w["ffn1_wd"])
    p2 = _mix(h1, w["mix_norm"], w["w_in"], w["head_gain"], w["bd"][:512, :512], n_norm_cols=2 * width)
    p3 = p2.reshape(b, t, -1)
    n_pairs = width // LANES
    if past_k is None:
        o_sb = _attention(p3, None, None, w["u2"], n_pairs=n_pairs, q_blk0=0, k_blk0=n_pairs, v_blk0=2 * n_pairs,
                          tq=tq)
    else:
        o_sb = _attention_cached(p3, past_k, past_v, w["ue"], width=width)

    lora_w = w["mu_lora"].shape[1]
    lora_blk = (6 * width + 2 * d) // lora_w
    lora_cols = w["lora_cols"]
    s_rkv = shift0[:, :, :3 * width]
    s_lora = jnp.pad(shift0[:, :, 3 * width:], ((0, 0), (0, 0), (0, lora_w - lora_cols)))
    r, wl, k, v, av, bv, g = _rwkv_prep(
        p3, s_rkv, s_lora, w["mu_rkv"], w["mu_lora"], w["w0"], w["a0"], w["k_k"], w["k_a"],
        w["ww2"], w["wa2"], w["wg2"], w["member"], rkv_blk=1, lora_blk=lora_blk, tc=tc)

    s0 = wkv0.reshape(b, n_pairs, 2, HEAD_DIM, HEAD_DIM)
    z = jnp.zeros_like(s0[:, :, 0])
    s0_bd = jnp.concatenate([jnp.concatenate([s0[:, :, 0], z], axis=-1),
                             jnp.concatenate([z, s0[:, :, 1]], axis=-1)], axis=-2)
    y, s_bd = _rwkv_scan(r, wl, k, v, av, bv, s0_bd, w["tri"], tc=tc, pairs_per_step=scan_pairs)
    wkv = jnp.stack([s_bd[:, :, :HEAD_DIM, :HEAD_DIM], s_bd[:, :, HEAD_DIM:, HEAD_DIM:]], axis=2)
    wkv = wkv.reshape(b, 2 * n_pairs, HEAD_DIM, HEAD_DIM)

    flat = lambda a: a.reshape(n, width)
    gs_blk = (6 * width) // d
    h2 = _merge_out(h1, flat(o_sb), flat(y), flat(r), flat(k), flat(v), flat(g), p2,
                    w["ln_w"], w["ln_b"], w["r_k"], w["member"], w["sb_wo"], w["rw_wo"], w["w_out"],
                    gs_blk=gs_blk, gr_blk=gs_blk + 1)
    out = _ffn(h2, w["ffn2_norm"], w["ffn2_wg"], w["ffn2_wu"], w["ffn2_wd"])

    heads = width // HEAD_DIM
    k_new = p3[:, :, width:2 * width].reshape(b, t, heads, HEAD_DIM)
    v_new = p3[:, :, 2 * width:3 * width].reshape(b, t, heads, HEAD_DIM)
    shift = jnp.concatenate([p3[:, t - 1:, 3 * width:6 * width],
                             p3[:, t - 1:, 6 * width + 2 * d:6 * width + 2 * d + lora_cols]], axis=-1)
    return out.reshape(b, t, d), k_new, v_new, wkv, shift


def _layer_weights(l, ffn1_norm, ffn1_w_gate, ffn1_w_up, ffn1_w_down, mix_norm, w_in, sb_q_norm, sb_k_norm, sb_w_o,
                   rwkv_mu, rwkv_w0, rwkv_w_w2, rwkv_a0, rwkv_w_a2, rwkv_w_g2, rwkv_k_k, rwkv_k_a, rwkv_r_k,
                   rwkv_ln_w, rwkv_ln_b, rwkv_w_o, w_out, ffn2_norm, ffn2_w_gate, ffn2_w_up, ffn2_w_down):
    d = w_in.shape[1]
    width = rwkv_w0.shape[1]
    heads = width // HEAD_DIM
    n_decay, n_iclr, n_gate = rwkv_w_w2.shape[1], rwkv_w_a2.shape[1], rwkv_w_g2.shape[1]
    lora_cols = n_decay + n_iclr + n_gate
    lora_w = -(-lora_cols // 512) * 512
    row = lambda a: a.reshape(1, -1).astype(F32)
    wi = w_in[l]
    w_in_p = jnp.concatenate([
        wi[:, :6 * width], wi[:, 6 * width + lora_cols:], wi[:, 6 * width:6 * width + lora_cols],
        jnp.zeros((d, lora_w - lora_cols), wi.dtype)], axis=1).astype(BF16)
    total = w_in_p.shape[1]
    head_gain = jnp.concatenate([jnp.tile(sb_q_norm[l], heads), jnp.tile(sb_k_norm[l], heads),
                                 jnp.ones((total - 2 * width,), F32)]).reshape(1, total)
    mu = rwkv_mu[l]

    def lora_pad(wm, r0):
        return jnp.zeros((lora_w, width), F32).at[r0:r0 + wm.shape[0]].set(wm).astype(BF16)

    hid = jnp.arange(width) // HEAD_DIM
    bd = (hid[:, None] == hid[None, :]).astype(BF16)
    member = (hid[:, None] == jnp.arange(LANES)[None, :]).astype(BF16)
    tk = ATT_TK
    ki = jnp.arange(tk)
    u2 = jnp.concatenate([(ki[None, :] > ki[:, None]).astype(BF16), jnp.ones((16, tk), BF16)], axis=0)
    ue = jnp.concatenate([(ki[:, None] > ki[None, :]).astype(BF16), jnp.ones((tk, LANES), BF16)], axis=1)
    ci = jnp.arange(RW_CHUNK)
    tri = (ci[:, None] >= ci[None, :]).astype(BF16)
    return {
        "ffn1_norm": row(ffn1_norm[l]), "ffn1_wg": ffn1_w_gate[l].astype(BF16), "ffn1_wu": ffn1_w_up[l].astype(BF16),
        "ffn1_wd": ffn1_w_down[l].astype(BF16),
        "ffn2_norm": row(ffn2_norm[l]), "ffn2_wg": ffn2_w_gate[l].astype(BF16), "ffn2_wu": ffn2_w_up[l].astype(BF16),
        "ffn2_wd": ffn2_w_down[l].astype(BF16),
        "mix_norm": row(mix_norm[l]), "w_in": w_in_p, "head_gain": head_gain, "bd": bd, "member": member, "u2": u2, "ue": ue, "tri": tri,
        "mu_rkv": row(mu[:3 * width]), "mu_lora": row(jnp.pad(mu[3 * width:], (0, lora_w - lora_cols))),
        "lora_cols": lora_cols,
        "w0": row(rwkv_w0[l]), "a0": row(rwkv_a0[l]), "k_k": row(rwkv_k_k[l]), "k_a": row(rwkv_k_a[l]),
        "ww2": lora_pad(rwkv_w_w2[l], 0), "wa2": lora_pad(rwkv_w_a2[l], n_decay),
        "wg2": lora_pad(rwkv_w_g2[l], n_decay + n_iclr),
        "ln_w": row(rwkv_ln_w[l]), "ln_b": row(rwkv_ln_b[l]), "r_k": row(rwkv_r_k[l]),
        "sb_wo": sb_w_o[l].astype(BF16), "rw_wo": rwkv_w_o[l].astype(BF16), "w_out": w_out[l].astype(BF16),
    }


def kernel(x_prompt, x_sample, cache_sb_k, cache_sb_v, state_rwkv_wkv, state_rwkv_shift, ffn1_norm, ffn1_w_gate, ffn1_w_up, ffn1_w_down, mix_norm, w_in, sb_q_norm, sb_k_norm, sb_w_o, rwkv_mu, rwkv_w0, rwkv_w_w2, rwkv_a0, rwkv_w_a2, rwkv_w_g2, rwkv_k_k, rwkv_k_a, rwkv_r_k, rwkv_ln_w, rwkv_ln_b, rwkv_w_o, w_out, ffn2_norm, ffn2_w_gate, ffn2_w_up, ffn2_w_down):
    depth = w_in.shape[0]
    yp, ys = x_prompt, x_sample
    bp = x_prompt.shape[0]
    width = rwkv_w0.shape[1]
    heads = width // HEAD_DIM
    rw_cols = state_rwkv_shift.shape[-1]
    outs = [[] for _ in range(8)]
    for l in range(depth):
        w = _layer_weights(l, ffn1_norm, ffn1_w_gate, ffn1_w_up, ffn1_w_down, mix_norm, w_in, sb_q_norm, sb_k_norm,
                           sb_w_o, rwkv_mu, rwkv_w0, rwkv_w_w2, rwkv_a0, rwkv_w_a2, rwkv_w_g2, rwkv_k_k, rwkv_k_a,
                           rwkv_r_k, rwkv_ln_w, rwkv_ln_b, rwkv_w_o, w_out, ffn2_norm, ffn2_w_gate, ffn2_w_up,
                           ffn2_w_down)
        wkv_zero = jnp.zeros((bp, heads, HEAD_DIM, HEAD_DIM), F32)
        shift_zero = jnp.zeros((bp, 1, rw_cols), F32)
        yp, kp, vp, wkvp, shp = _layer(yp, None, None, wkv_zero, shift_zero, w, tq=512, tc=256, scan_pairs=4)
        ys, kn, vn, wkvn, shn = _layer(ys, cache_sb_k[l], cache_sb_v[l], state_rwkv_wkv[l], state_rwkv_shift[l], w, tq=64, tc=64, scan_pairs=8)
        for lst, val in zip(outs, (kp, vp, wkvp, shp, kn, vn, wkvn, shn)):
            lst.append(val)
    return (yp, ys) + tuple(jnp.stack(o) for o in outs)
```

```python
import functools

import jax
import jax.numpy as jnp
from jax import lax
from jax.experimental import pallas as pl
from jax.experimental.pallas import tpu as pltpu

F32 = jnp.float32
BF16 = jnp.bfloat16

HEAD_DIM = 64
LANES = 128
NORM_EPS = 1e-6
GN_EPS = 64e-5
RW_CHUNK = 64
ATT_TK = 256

MIB = 1024 * 1024
LOG2_E = 1.4426950408889634
SIGN_BIT = 0x80000000
ATT_CHAIN_LANES = 256
ATT_SKEW = 2


def _nt(x, y):
    return lax.dot_general(x, y, (((1,), (1,)), ((), ())), preferred_element_type=F32)


def _tn(x, y):
    return lax.dot_general(x, y, (((0,), (0,)), ((), ())), preferred_element_type=F32)


def _mm(x, y):
    return jnp.dot(x, y, preferred_element_type=F32)


def _neg_abs(x):
    return lax.bitcast_convert_type(lax.bitcast_convert_type(x, jnp.uint32) | jnp.uint32(SIGN_BIT), F32)


def _split2(x):
    hi = x.astype(BF16)
    lo = (x - hi.astype(F32)).astype(BF16)
    return hi, lo


def _head_sum(x, bd):
    hi, lo = _split2(x)
    return _mm(hi, bd) + _mm(lo, bd)


def _head_sum2(x, member):
    hi, lo = _split2(x)
    shi, slo = _split2(_mm(hi, member) + _mm(lo, member))
    return _nt(shi, member) + _nt(slo, member)


def _rms(x, g):
    ms = jnp.mean(x * x, axis=-1, keepdims=True)
    return x * lax.rsqrt(ms + NORM_EPS) * g


def _params(sem, vmem_mib):
    return pltpu.CompilerParams(dimension_semantics=sem, vmem_limit_bytes=vmem_mib * MIB)


def _ffn_body(x_ref, g_ref, wg_ref, wu_ref, wd_ref, o_ref, n_ref, acc_ref):
    f = pl.program_id(1)

    @pl.when(f == 0)
    def _():
        n_ref[...] = _rms(x_ref[...], g_ref[...]).astype(BF16)
        acc_ref[...] = jnp.zeros_like(acc_ref)

    n = n_ref[...]
    g = _mm(n, wg_ref[...])
    u = _mm(n, wu_ref[...])
    a = (g * jax.nn.sigmoid(g) * u).astype(BF16)
    acc_ref[...] += _mm(a, wd_ref[...])

    @pl.when(f == pl.num_programs(1) - 1)
    def _():
        o_ref[...] = x_ref[...] + 0.5 * acc_ref[...]


def _ffn(x, g, wg, wu, wd, *, tm=512, tf=512):
    n, d = x.shape
    ff = wg.shape[1]
    tm, tf = min(tm, n), min(tf, ff)
    return pl.pallas_call(
        _ffn_body,
        out_shape=jax.ShapeDtypeStruct((n, d), F32),
        grid=(n // tm, ff // tf),
        in_specs=[
            pl.BlockSpec((tm, d), lambda i, f: (i, 0)),
            pl.BlockSpec((1, d), lambda i, f: (0, 0)),
            pl.BlockSpec((d, tf), lambda i, f: (0, f)),
            pl.BlockSpec((d, tf), lambda i, f: (0, f)),
            pl.BlockSpec((tf, d), lambda i, f: (f, 0)),
        ],
        out_specs=pl.BlockSpec((tm, d), lambda i, f: (i, 0)),
        scratch_shapes=[pltpu.VMEM((tm, d), BF16), pltpu.VMEM((tm, d), F32)],
        compiler_params=_params(("parallel", "arbitrary"), 48),
        name="ffn",
    )(x, g, wg, wu, wd)


def _mix_body(h_ref, g_ref, w_ref, hg_ref, bd_ref, o_ref, n_ref, *, n_norm_tiles):
    j = pl.program_id(1)

    @pl.when(j == 0)
    def _():
        n_ref[...] = _rms(h_ref[...], g_ref[...]).astype(BF16)

    p = _mm(n_ref[...], w_ref[...])

    @pl.when(j < n_norm_tiles)
    def _():
        ms = _head_sum(p * p, bd_ref[...]) * (1.0 / HEAD_DIM)
        o_ref[...] = p * lax.rsqrt(ms + NORM_EPS) * hg_ref[...]

    @pl.when(j >= n_norm_tiles)
    def _():
        o_ref[...] = p


def _mix(h, g, w, hgain, bd, *, n_norm_cols, tm=1024, tn=512):
    n, d = h.shape
    cols = w.shape[1]
    tm = min(tm, n)
    return pl.pallas_call(
        functools.partial(_mix_body, n_norm_tiles=n_norm_cols // tn),
        out_shape=jax.ShapeDtypeStruct((n, cols), F32),
        grid=(n // tm, cols // tn),
        in_specs=[
            pl.BlockSpec((tm, d), lambda i, j: (i, 0)),
            pl.BlockSpec((1, d), lambda i, j: (0, 0)),
            pl.BlockSpec((d, tn), lambda i, j: (0, j)),
            pl.BlockSpec((1, tn), lambda i, j: (0, j)),
            pl.BlockSpec((tn, tn), lambda i, j: (0, 0)),
        ],
        out_specs=pl.BlockSpec((tm, tn), lambda i, j: (i, j)),
        scratch_shapes=[pltpu.VMEM((tm, d), BF16)],
        compiler_params=_params(("parallel", "arbitrary"), 48),
        name="mix",
    )(h, g, w, hgain, bd)


def _sb_items(items, ut, tk, run, pv):
    zs, parts = {}, {}

    def scores(n):
        _, q16, kblk, _, _ = items[n]
        zs[n] = _nt(kblk, q16)

    def keep_sums(n):
        mask = items[n][4]
        z = zs.pop(n)
        lp = jnp.log(1.0 + jnp.exp2(_neg_abs(z))) * LOG2_E
        log_beta = jnp.minimum(z, 0.0) - lp
        log_keep = log_beta - z
        if mask is not None:
            log_keep = jnp.where(mask, log_keep, 0.0)
        parts[n] = (log_beta, _mm(ut, log_keep.astype(BF16)))

    def weigh(n):
        ci, _, _, vt_rows, mask = items[n]
        log_beta, ext = parts.pop(n)
        w = jnp.exp2(log_beta + (ext[:tk] + run[ci][0:1]))
        if mask is not None:
            w = jnp.where(mask, w, 0.0)
        d = _mm(vt_rows, w.astype(BF16))
        pv[ci] = d if ci not in pv else pv[ci] + d
        run[ci] = run[ci] + ext[tk:tk + 8]

    for step in range(len(items) + 2 * ATT_SKEW):
        if step < len(items):
            scores(step)
        if 0 <= step - ATT_SKEW < len(items):
            keep_sums(step - ATT_SKEW)
        if 0 <= step - 2 * ATT_SKEW < len(items):
            weigh(step - 2 * ATT_SKEW)


def _sb_rows(items, ue, tk, run, pv):
    zs, parts = {}, {}

    def scores(n):
        _, qh, kth, _, _ = items[n]
        zs[n] = jnp.concatenate([_mm(q, kt) for q, kt in zip(qh, kth)], axis=0)

    def keep_sums(n):
        mask = items[n][4]
        z = zs.pop(n)
        lp = jnp.log(1.0 + jnp.exp2(_neg_abs(z))) * LOG2_E
        log_beta = jnp.minimum(z, 0.0) - lp
        log_keep = log_beta - z
        if mask is not None:
            log_keep = jnp.where(mask, log_keep, 0.0)
        parts[n] = (log_beta, _mm(log_keep.astype(BF16), ue))

    def weigh(n):
        ci, qh, _, vth, mask = items[n]
        log_beta, ext = parts.pop(n)
        tail = jnp.concatenate([ext[:, c0:c0 + LANES] + run[ci] for c0 in range(0, tk, LANES)], axis=1)
        w = jnp.exp2(log_beta + tail)
        if mask is not None:
            w = jnp.where(mask, w, 0.0)
        w = w.astype(BF16)
        tq = qh[0].shape[0]
        for h, vt in enumerate(vth):
            d = _nt(w[h * tq:(h + 1) * tq], vt)
            pv[ci, h] = d if (ci, h) not in pv else pv[ci, h] + d
        run[ci] = run[ci] + ext[:, tk:]

    for step in range(len(items) + 2 * ATT_SKEW):
        if step < len(items):
            scores(step)
        if 0 <= step - ATT_SKEW < len(items):
            keep_sums(step - ATT_SKEW)
        if 0 <= step - 2 * ATT_SKEW < len(items):
            weigh(step - 2 * ATT_SKEW)


def _attn_cached_body(q_ref, k_ref, v_ref, ckt_ref, cvt_ref, ue_ref, o_ref, qs, carry, acc, *,
                      tq, tk, n_heads, chunk_blocks, per_chain):
    j = pl.program_id(1)
    n_chains = n_heads // per_chain
    ue = ue_ref[...]
    heads_of = lambda c: range(c * per_chain, (c + 1) * per_chain)

    def run_items(items):
        run = {c: carry[c] for c in range(n_chains)}
        pv = {}
        _sb_rows(items, ue, tk, run, pv)
        for c in range(n_chains):
            carry[c] = run[c]
            for i, h in enumerate(heads_of(c)):
                acc[h] += pv[c, i]

    @pl.when(j == 0)
    def _():
        carry[...] = jnp.zeros_like(carry)
        acc[...] = jnp.zeros_like(acc)
        for h in range(n_heads):
            qs[h] = (q_ref[0, :, h * HEAD_DIM:(h + 1) * HEAD_DIM] * (HEAD_DIM ** -0.5 * LOG2_E)).astype(BF16)
        row = lax.broadcasted_iota(jnp.int32, (per_chain * tq, tk), 0) % tq
        col = lax.broadcasted_iota(jnp.int32, (per_chain * tq, tk), 1)
        pad = jnp.zeros((HEAD_DIM, tk - tq), F32)
        items = []
        for c in range(n_chains):
            def new_t(ref):
                return [jnp.concatenate([ref[0, :, h * HEAD_DIM:(h + 1) * HEAD_DIM].T, pad], axis=1).astype(BF16)
                        for h in heads_of(c)]
            items.append((c, [qs[h] for h in heads_of(c)], new_t(k_ref), new_t(v_ref), col < row))
        run_items(items)

    items = []
    for m in reversed(range(chunk_blocks)):
        keys = slice(m * tk, (m + 1) * tk)
        for c in range(n_chains):
            items.append((c, [qs[h] for h in heads_of(c)],
                          [ckt_ref[0, h, :, keys].astype(BF16) for h in heads_of(c)],
                          [cvt_ref[0, h, :, keys].astype(BF16) for h in heads_of(c)], None))
    run_items(items)

    @pl.when(j == pl.num_programs(1) - 1)
    def _():
        o_ref[0] = jnp.concatenate([acc[h] for h in range(n_heads)], axis=1)


def _attention_cached(p3, cache_k, cache_v, ue, *, width, chunk_keys=512, per_chain=4):
    b, t, _ = p3.shape
    tk = ATT_TK
    _, p_len, n_heads, _ = cache_k.shape
    chunk_keys = min(chunk_keys, p_len)
    n_chunks = p_len // chunk_keys
    ckt = jnp.transpose(cache_k, (0, 2, 3, 1))
    cvt = jnp.transpose(cache_v, (0, 2, 3, 1))
    cache_spec = pl.BlockSpec((1, n_heads, HEAD_DIM, chunk_keys), lambda bi, j: (bi, 0, 0, n_chunks - 1 - j))
    return pl.pallas_call(
        functools.partial(_attn_cached_body, tq=t, tk=tk, n_heads=n_heads, chunk_blocks=chunk_keys // tk,
                          per_chain=per_chain),
        out_shape=jax.ShapeDtypeStruct((b, t, width), F32),
        grid=(b, n_chunks),
        in_specs=[
            pl.BlockSpec((1, t, width), lambda bi, j: (bi, 0, 0)),
            pl.BlockSpec((1, t, width), lambda bi, j: (bi, 0, 1)),
            pl.BlockSpec((1, t, width), lambda bi, j: (bi, 0, 2)),
            cache_spec, cache_spec,
            pl.BlockSpec(ue.shape, lambda bi, j: (0, 0)),
        ],
        out_specs=pl.BlockSpec((1, t, width), lambda bi, j: (bi, 0, 0)),
        scratch_shapes=[
            pltpu.VMEM((n_heads, t, HEAD_DIM), BF16),
            pltpu.VMEM((n_heads // per_chain, per_chain * t, LANES), F32),
            pltpu.VMEM((n_heads, t, HEAD_DIM), F32),
        ],
        compiler_params=_params(("parallel", "arbitrary"), 48),
        name="sb_attention_cached",
    )(p3, p3, p3, ckt, cvt, ue)


def _attn_body(*refs, tq, tk, t_new, n_new_blocks, n_past_blocks, past_unroll, new_unroll, chains):
    if n_past_blocks:
        q_ref, k_ref, v_ref, pk_ref, pv_ref, ut_ref, o_ref, kb, vt, qs, carry, acc = refs
    else:
        q_ref, k_ref, v_ref, ut_ref, o_ref, kb, vt, qs, carry, acc = refs
        pk_ref = pv_ref = None
    i = pl.program_id(2)

    def pad_rows(x):
        if x.shape[0] == tk:
            return x
        return jnp.concatenate([x, jnp.zeros((tk - x.shape[0], x.shape[1]), x.dtype)], axis=0)

    @pl.when(i == 0)
    def _():
        if n_new_blocks == 1:
            kb[...] = pad_rows(k_ref[0]).astype(BF16)
            vt[0] = pad_rows(v_ref[0]).T.astype(BF16)
        else:
            def fill(blk, _):
                rows = pl.ds(pl.multiple_of(blk * tk, tk), tk)
                kb[rows, :] = k_ref[0, rows, :].astype(BF16)
                vt[blk] = v_ref[0, rows, :].T.astype(BF16)
                return 0

            lax.fori_loop(0, n_new_blocks, fill, 0)

    q = q_ref[0] * (HEAD_DIM ** -0.5 * LOG2_E)
    n_heads = q.shape[1] // HEAD_DIM
    head_of_lane = lax.broadcasted_iota(jnp.int32, q.shape, 1) // HEAD_DIM
    zero = jnp.zeros_like(q)
    for h in range(n_heads):
        qs[h * tq:(h + 1) * tq, :] = jnp.where(head_of_lane == h, q, zero).astype(BF16)
    carry[...] = jnp.zeros_like(carry)
    acc[...] = jnp.zeros_like(acc)

    def span(blocks):
        q16 = [qs[c0:c1, :] for c0, c1, _, _ in chains]
        cw = chains[0][1] - chains[0][0]
        row = lax.broadcasted_iota(jnp.int32, (tk, cw), 0)
        lane = lax.broadcasted_iota(jnp.int32, (tk, cw), 1)
        mask = lambda off, c0: None if off is None else (row + off) < (lane + c0) % tq
        items = [(ci, q16[ci], kblk, vtblk[r0:r1], mask(off, c0))
                 for kblk, vtblk, off in blocks for ci, (c0, _, r0, r1) in enumerate(chains)]
        run = {ci: carry[:, c0:c1] for ci, (c0, c1, _, _) in enumerate(chains)}
        pv = {}
        _sb_items(items, ut_ref[...], tk, run, pv)
        for ci, (c0, c1, r0, r1) in enumerate(chains):
            carry[:, c0:c1] = run[ci]
            acc[r0:r1, c0:c1] += pv[ci]

    q0 = i * tq
    n_diag = max(tq // tk, 1)
    diag = []
    for m in reversed(range(n_diag)):
        blk = q0 // tk + m
        diag.append((kb[pl.ds(pl.multiple_of(blk * tk, tk), tk), :], vt[blk], m * tk))
    span(diag)

    def new_blocks(last, count):
        blocks = []
        for m in range(count):
            blk = last - m
            blocks.append((kb[pl.ds(pl.multiple_of(blk * tk, tk), tk), :], vt[blk], None))
        span(blocks)

    n_left = q0 // tk
    n_main = n_left // new_unroll

    def main_step(it, _):
        new_blocks(n_left - 1 - it * new_unroll, new_unroll)
        return 0

    lax.fori_loop(0, n_main, main_step, 0)
    if new_unroll > n_diag:
        def rest_step(it, _):
            new_blocks(n_left - 1 - n_main * new_unroll - it * n_diag, n_diag)
            return 0

        lax.fori_loop(0, (n_left - n_main * new_unroll) // n_diag, rest_step, 0)

    if n_past_blocks:
        def past_step(it, _):
            blocks = []
            for m in range(past_unroll):
                k0 = pl.multiple_of((n_past_blocks - 1 - it * past_unroll - m) * tk, tk)
                blocks.append((pk_ref[0, pl.ds(k0, tk), :].astype(BF16),
                               pv_ref[0, pl.ds(k0, tk), :].T.astype(BF16), None))
            span(blocks)
            return 0

        lax.fori_loop(0, n_past_blocks // past_unroll, past_step, 0)

    out_t = acc[...].T
    out = out_t[:tq]
    for h in range(1, n_heads):
        out = jnp.where(head_of_lane == h, out_t[h * tq:(h + 1) * tq], out)
    o_ref[0] = out


def _attention(p3, past_k, past_v, u2, *, n_pairs, q_blk0, k_blk0, v_blk0, tq):
    b, t, _ = p3.shape
    tk = ATT_TK
    tq = min(tq, t)
    n_new_blocks = -(-t // tk)
    if tq % LANES == 0:
        npb = 1
        cw = min(tq, ATT_CHAIN_LANES)
        chains = tuple((h * tq + c0, h * tq + c0 + cw, h * HEAD_DIM, (h + 1) * HEAD_DIM)
                       for h in range(2) for c0 in range(0, tq, cw))
    else:
        npb = max(1, (2 * LANES) // (2 * tq))
        chains = ((0, 2 * npb * tq, 0, npb * LANES),)
    lw = npb * LANES
    has_past = past_k is not None
    n_past_blocks = past_k.shape[1] // tk if has_past else 0
    in_specs = [
        pl.BlockSpec((1, tq, lw), lambda bi, p, i: (bi, i, q_blk0 // npb + p)),
        pl.BlockSpec((1, t, lw), lambda bi, p, i: (bi, 0, k_blk0 // npb + p)),
        pl.BlockSpec((1, t, lw), lambda bi, p, i: (bi, 0, v_blk0 // npb + p)),
    ]
    args = [p3, p3, p3]
    if has_past:
        pp = past_k.shape[1]
        in_specs += [pl.BlockSpec((1, pp, lw), lambda bi, p, i: (bi, 0, p))] * 2
        args += [past_k, past_v]
    in_specs.append(pl.BlockSpec(u2.shape, lambda bi, p, i: (0, 0)))
    args.append(u2)
    return pl.pallas_call(
        functools.partial(_attn_body, tq=tq, tk=tk, t_new=t, n_new_blocks=n_new_blocks, n_past_blocks=n_past_blocks,
                          past_unroll=min(4, max(n_past_blocks, 1)), new_unroll=2 * max(tq // tk, 1), chains=chains),
        out_shape=jax.ShapeDtypeStruct((b, t, n_pairs * LANES), F32),
        grid=(b, n_pairs // npb, t // tq),
        in_specs=in_specs,
        out_specs=pl.BlockSpec((1, tq, lw), lambda bi, p, i: (bi, i, p)),
        scratch_shapes=[
            pltpu.VMEM((n_new_blocks * tk, lw), BF16),
            pltpu.VMEM((n_new_blocks, lw, tk), BF16),
            pltpu.VMEM((2 * npb * tq, lw), BF16),
            pltpu.VMEM((8, 2 * npb * tq), F32),
            pltpu.VMEM((lw, 2 * npb * tq), F32),
        ],
        compiler_params=_params(("parallel", "parallel", "arbitrary"), 48),
        name="sb_attention",
    )(*args)


def _prep_body(prkv_ref, plora_ref, s_rkv_ref, s_lora_ref, mu_rkv_ref, mu_lora_ref, w0_ref, a0_ref, kk_ref, ka_ref,
               ww2_ref, wa2_ref, wg2_ref, bd_ref,
               r_o, wl_o, k_o, v_o, av_o, bv_o, g_o, c_rkv, c_lora, *, tc, width):
    t = pl.program_id(1)

    @pl.when(t == 0)
    def _():
        c_rkv[0:1, :] = s_rkv_ref[0]
        c_lora[0:1, :] = s_lora_ref[0]

    def token_mix(p, prev, mu):
        row = lax.broadcasted_iota(jnp.int32, p.shape, 0)
        shifted = jnp.where(row == 0, prev, pltpu.roll(p, 1, 0))
        return p + (shifted - p) * mu

    def rkv_seg(s):
        cs = slice(s * width, (s + 1) * width)
        p = prkv_ref[0, :, cs]
        x = token_mix(p, c_rkv[0:1, cs], mu_rkv_ref[:, cs])
        c_rkv[0:1, cs] = p[tc - 1:tc, :]
        return x

    pl_ = plora_ref[0]
    xl = token_mix(pl_, c_lora[0:1, :], mu_lora_ref[...])
    c_lora[0:1, :] = pl_[tc - 1:tc, :]

    r_o[0] = rkv_seg(0)
    v_o[0] = rkv_seg(2)
    xk = rkv_seg(1)

    dec = w0_ref[...] + _mm(jnp.tanh(xl).astype(BF16), ww2_ref[...])
    nd = -dec
    softplus = jnp.maximum(nd, 0.0) + jnp.log(1.0 + jnp.exp(-jnp.abs(nd)))
    w_log = -softplus - 0.5
    wl_o[0] = -jnp.exp(w_log)
    a = jax.nn.sigmoid(a0_ref[...] + _mm(xl.astype(BF16), wa2_ref[...]))
    g_o[0] = _mm(jax.nn.sigmoid(xl).astype(BF16), wg2_ref[...])
    kk = xk * kk_ref[...]
    k_o[0] = xk * (1.0 + (a - 1.0) * ka_ref[...])
    norm = jnp.sqrt(_head_sum2(kk * kk, bd_ref[...]))
    kk = kk / jnp.maximum(norm, 1e-12)
    av_o[0] = -kk
    bv_o[0] = kk * a


def _rwkv_prep(p3, s_rkv, s_lora, mu_rkv, mu_lora, w0, a0, k_k, k_a, ww2, wa2, wg2, bd, *, rkv_blk, lora_blk, tc):
    b, t, _ = p3.shape
    width = w0.shape[1]
    lw = mu_lora.shape[1]
    tc = min(tc, t)
    const = lambda shape: pl.BlockSpec(shape, lambda bi, ti: (0,) * len(shape))
    out_spec = pl.BlockSpec((1, tc, width), lambda bi, ti: (bi, ti, 0))
    return pl.pallas_call(
        functools.partial(_prep_body, tc=tc, width=width),
        out_shape=[jax.ShapeDtypeStruct((b, t, width), F32)] * 7,
        grid=(b, t // tc),
        in_specs=[
            pl.BlockSpec((1, tc, 3 * width), lambda bi, ti: (bi, ti, rkv_blk)),
            pl.BlockSpec((1, tc, lw), lambda bi, ti: (bi, ti, lora_blk)),
            pl.BlockSpec((1, 1, 3 * width), lambda bi, ti: (bi, 0, 0)),
            pl.BlockSpec((1, 1, lw), lambda bi, ti: (bi, 0, 0)),
            const((1, 3 * width)), const((1, lw)),
            const((1, width)), const((1, width)), const((1, width)), const((1, width)),
            const((lw, width)), const((lw, width)), const((lw, width)),
            const((width, LANES)),
        ],
        out_specs=[out_spec] * 7,
        scratch_shapes=[pltpu.VMEM((8, 3 * width), F32), pltpu.VMEM((8, lw), F32)],
        compiler_params=_params(("parallel", "arbitrary"), 48),
        name="rwkv_prep",
    )(p3, p3, s_rkv, s_lora, mu_rkv, mu_lora, w0, a0, k_k, k_a, ww2, wa2, wg2, bd)


def _scan_body(r_ref, wl_ref, k_ref, v_ref, a_ref, b_ref, s0_ref, tri_ref, y_ref, sout_ref, s_scr, *,
               n_chunks, n_pairs):
    c_len = RW_CHUNK
    t = pl.program_id(2)

    @pl.when(t == 0)
    def _():
        s_scr[...] = s0_ref[0]

    row = lax.broadcasted_iota(jnp.int32, (c_len, c_len), 0)
    col = lax.broadcasted_iota(jnp.int32, (c_len, c_len), 1)
    strict = row > col
    incl = row >= col
    lane = lax.broadcasted_iota(jnp.int32, (c_len, LANES), 1)
    first = lane < HEAD_DIM
    brow = lax.broadcasted_iota(jnp.int32, (LANES, LANES), 0)
    bcol = lax.broadcasted_iota(jnp.int32, (LANES, LANES), 1)
    same_head = (brow // HEAD_DIM) == (bcol // HEAD_DIM)
    eye = brow == bcol
    tri = tri_ref[...]
    bf = lambda x: x.astype(BF16)

    cps = [(c, p) for c in range(n_chunks) for p in range(n_pairs)]
    sls = {(c, p): (slice(c * c_len, (c + 1) * c_len), slice(p * LANES, (p + 1) * LANES)) for c, p in cps}
    tiles = {}
    for c in range(n_chunks):
        sl = slice(c * c_len, (c + 1) * c_len)
        wl = wl_ref[0, sl, :]
        hi = wl.astype(BF16)
        rem = wl - hi.astype(F32)
        mid = rem.astype(BF16)
        lo = (rem - mid.astype(F32)).astype(BF16)
        cum = _mm(tri, hi) + _mm(tri, mid) + _mm(tri, lo)
        tot = cum[c_len - 1:c_len, :]
        e_neg = jnp.exp(-cum)
        e_end = jnp.exp(tot - cum)
        av, bv, kv, vv = a_ref[0, sl, :], b_ref[0, sl, :], k_ref[0, sl, :], v_ref[0, sl, :]
        tiles[c] = dict(at=av * jnp.exp(cum - wl), rt=r_ref[0, sl, :] * jnp.exp(cum), bt=bf(bv * e_neg),
                        kt=bf(kv * e_neg), bh=bv * e_end, kh=kv * e_end, vv=vv, etot=jnp.exp(tot))
    tile = lambda name, c, p: tiles[c][name][:, p * LANES:(p + 1) * LANES]

    chains = [(c, p, h) for c, p in cps for h in range(2)]
    mbk = {}
    for c, p, h in chains:
        sel = first if h == 0 else jnp.logical_not(first)
        at, rt = tile("at", c, p), tile("rt", c, p)
        zero = jnp.zeros_like(at)
        ar = jnp.concatenate([jnp.where(sel, at, zero), jnp.where(sel, rt, zero)], axis=0).astype(BF16)
        mbk[c, p, h] = (_nt(ar, tile("bt", c, p)), _nt(ar, tile("kt", c, p)))
    m_ab, p_rb, m_ak, p_rk, tm = {}, {}, {}, {}, {}
    for ch in chains:
        mb, mk = mbk[ch]
        m_ab[ch] = jnp.where(strict, mb[:c_len], 0.0)
        p_rb[ch] = bf(jnp.where(incl, mb[c_len:], 0.0))
        m_ak[ch] = bf(jnp.where(strict, mk[:c_len], 0.0))
        p_rk[ch] = bf(jnp.where(incl, mk[c_len:], 0.0))
        tm[ch] = jnp.where(row == col, 1.0, 0.0) + jnp.where((row // 2) == (col // 2), m_ab[ch], 0.0)
    s = 2
    while s < c_len:
        off = jnp.logical_and((row // (2 * s)) == (col // (2 * s)), (row // s) != (col // s))
        half = {ch: bf(_mm(bf(tm[ch]), bf(jnp.where(off, m_ab[ch], 0.0)))) for ch in chains}
        tm = {ch: tm[ch] + _mm(half[ch], bf(tm[ch])) for ch in chains}
        s *= 2
    t16 = {ch: bf(tm[ch]) for ch in chains}
    mv = {(c, p, h): _mm(m_ak[c, p, h], bf(tile("vv", c, p))) for c, p, h in chains}
    w1 = {(c, p, h): _mm(t16[c, p, h], bf(tile("at", c, p))) for c, p, h in chains}
    w2 = {ch: _mm(t16[ch], bf(mv[ch])) for ch in chains}
    qc = {(c, p, h): tile("rt", c, p) + _mm(p_rb[c, p, h], bf(w1[c, p, h])) for c, p, h in chains}
    y1 = {(c, p, h): _mm(p_rb[c, p, h], bf(w2[c, p, h])) + _mm(p_rk[c, p, h], bf(tile("vv", c, p)))
          for c, p, h in chains}
    both = lambda d, c, p: jnp.where(first, d[c, p, 0], d[c, p, 1])
    ac_t, dc_t, qcs, y1s = {}, {}, {}, {}
    for c, p in cps:
        bh, kh, vv = tile("bh", c, p), tile("kh", c, p), tile("vv", c, p)
        w1p, w2p = both(w1, c, p), both(w2, c, p)
        a_full = jnp.where(same_head, _tn(bf(w1p), bf(bh)), 0.0) + jnp.where(eye, tile("etot", c, p), 0.0)
        ac_t[c, p] = _split2(a_full)
        dc_t[c, p] = jnp.where(same_head, _tn(bf(jnp.concatenate([w2p, vv], axis=0)),
                                              bf(jnp.concatenate([bh, kh], axis=0))), 0.0)
        qcs[c, p], y1s[c, p] = bf(both(qc, c, p)), both(y1, c, p)
    for p in range(n_pairs):
        state = s_scr[p]
        for c in range(n_chunks):
            rs, ls = sls[c, p]
            s_hi, s_lo = _split2(state)
            a_hi, a_lo = ac_t[c, p]
            y_ref[0, rs, ls] = _nt(qcs[c, p], s_hi) + y1s[c, p]
            state = _mm(s_hi, a_hi) + _mm(s_hi, a_lo) + _mm(s_lo, a_hi) + dc_t[c, p]
        s_scr[p] = state

    @pl.when(t == pl.num_programs(2) - 1)
    def _():
        sout_ref[0] = s_scr[...]


def _rwkv_scan(r, wl, k, v, av, bv, s0_bd, tri, *, tc, pairs_per_step):
    b, t, width = r.shape
    n_pairs = width // LANES
    tc = min(tc, t)
    npb = pairs_per_step
    seq = pl.BlockSpec((1, tc, npb * LANES), lambda bi, p, ti: (bi, ti, p))
    state = pl.BlockSpec((1, npb, LANES, LANES), lambda bi, p, ti: (bi, p, 0, 0))
    return pl.pallas_call(
        functools.partial(_scan_body, n_chunks=tc // RW_CHUNK, n_pairs=npb),
        out_shape=[jax.ShapeDtypeStruct((b, t, width), F32), jax.ShapeDtypeStruct(s0_bd.shape, F32)],
        grid=(b, n_pairs // npb, t // tc),
        in_specs=[seq] * 6 + [state, pl.BlockSpec(tri.shape, lambda bi, p, ti: (0, 0))],
        out_specs=[seq, state],
        scratch_shapes=[pltpu.VMEM((npb, LANES, LANES), F32)],
        compiler_params=_params(("parallel", "parallel", "arbitrary"), 32),
        name="rwkv_scan",
    )(r, wl, k, v, av, bv, s0_bd, tri)


def _out_body(h_ref, osb_ref, y_ref, r_ref, k_ref, v_ref, g_ref, gs_ref, gr_ref, lnw_ref, lnb_ref, rk_ref, bd_ref,
              wso_ref, wro_ref, wout_ref, o_ref, m_ref):
    j = pl.program_id(1)

    @pl.when(j == 0)
    def _():
        bd = bd_ref[...]
        y = y_ref[...]
        mu = _head_sum2(y, bd) * (1.0 / HEAD_DIM)
        d = y - mu
        var = _head_sum2(d * d, bd) * (1.0 / HEAD_DIM)
        yn = d * lax.rsqrt(var + GN_EPS) * lnw_ref[...] + lnb_ref[...]
        bonus = _head_sum2(r_ref[...] * k_ref[...] * rk_ref[...], bd) * v_ref[...]
        yy = ((yn + bonus) * g_ref[...]).astype(BF16)
        o_sb = _mm(osb_ref[...].astype(BF16), wso_ref[...])
        o_rw = _mm(yy, wro_ref[...])
        merged = jax.nn.sigmoid(gs_ref[...]) * o_sb + jax.nn.sigmoid(gr_ref[...]) * o_rw
        m_ref[...] = merged.astype(BF16)

    o_ref[...] = h_ref[...] + _mm(m_ref[...], wout_ref[...])


def _merge_out(h, o_sb, y, r, k, v, g, p2, lnw, lnb, rk, bd, wso, wro, wout, *, gs_blk, gr_blk, tm=128, tn=2048):
    n, d = h.shape
    width = o_sb.shape[1]
    tm, tn = min(tm, n), min(tn, d)
    tok = pl.BlockSpec((tm, width), lambda i, j: (i, 0))
    const = lambda shape: pl.BlockSpec(shape, lambda i, j: (0,) * len(shape))
    return pl.pallas_call(
        _out_body,
        out_shape=jax.ShapeDtypeStruct((n, d), F32),
        grid=(n // tm, d // tn),
        in_specs=[
            pl.BlockSpec((tm, tn), lambda i, j: (i, j)),
            tok, tok, tok, tok, tok, tok,
            pl.BlockSpec((tm, d), lambda i, j: (i, gs_blk)),
            pl.BlockSpec((tm, d), lambda i, j: (i, gr_blk)),
            const((1, width)), const((1, width)), const((1, width)),
            const((width, LANES)),
            const((width, d)), const((width, d)),
            pl.BlockSpec((d, tn), lambda i, j: (0, j)),
        ],
        out_specs=pl.BlockSpec((tm, tn), lambda i, j: (i, j)),
        scratch_shapes=[pltpu.VMEM((tm, d), BF16)],
        compiler_params=_params(("parallel", "arbitrary"), 56),
        name="merge_out",
    )(h, o_sb, y, r, k, v, g, p2, p2, lnw, lnb, rk, bd, wso, wro, wout)


def _layer(x, past_k, past_v, wkv0, shift0, w, *, tq, tc, scan_pairs):
    b, t, d = x.shape
    n = b * t
    width = w["w0"].shape[1]
    h1 = _ffn(x.reshape(n, d), w["ffn1_norm"], w["ffn1_wg"], w["ffn1_wu"], w["ffn1_wd"])
    p2 = _mix(h1, w["mix_norm"], w["w_in"], w["head_gain"], w["bd"][:512, :512], n_norm_cols=2 * width)
    p3 = p2.reshape(b, t, -1)
    n_pairs = width // LANES
    if past_k is None:
        o_sb = _attention(p3, None, None, w["u2"], n_pairs=n_pairs, q_blk0=0, k_blk0=n_pairs, v_blk0=2 * n_pairs,
                          tq=tq)
    else:
        o_sb = _attention_cached(p3, past_k, past_v, w["ue"], width=width)

    lora_w = w["mu_lora"].shape[1]
    lora_blk = (6 * width + 2 * d) // lora_w
    lora_cols = w["lora_cols"]
    s_rkv = shift0[:, :, :3 * width]
    s_lora = jnp.pad(shift0[:, :, 3 * width:], ((0, 0), (0, 0), (0, lora_w - lora_cols)))
    r, wl, k, v, av, bv, g = _rwkv_prep(
        p3, s_rkv, s_lora, w["mu_rkv"], w["mu_lora"], w["w0"], w["a0"], w["k_k"], w["k_a"],
        w["ww2"], w["wa2"], w["wg2"], w["member"], rkv_blk=1, lora_blk=lora_blk, tc=tc)

    s0 = wkv0.reshape(b, n_pairs, 2, HEAD_DIM, HEAD_DIM)
    z = jnp.zeros_like(s0[:, :, 0])
    s0_bd = jnp.concatenate([jnp.concatenate([s0[:, :, 0], z], axis=-1),
                             jnp.concatenate([z, s0[:, :, 1]], axis=-1)], axis=-2)
    y, s_bd = _rwkv_scan(r, wl, k, v, av, bv, s0_bd, w["tri"], tc=tc, pairs_per_step=scan_pairs)
    wkv = jnp.stack([s_bd[:, :, :HEAD_DIM, :HEAD_DIM], s_bd[:, :, HEAD_DIM:, HEAD_DIM:]], axis=2)
    wkv = wkv.reshape(b, 2 * n_pairs, HEAD_DIM, HEAD_DIM)

    flat = lambda a: a.reshape(n, width)
    gs_blk = (6 * width) // d
    h2 = _merge_out(h1, flat(o_sb), flat(y), flat(r), flat(k), flat(v), flat(g), p2,
                    w["ln_w"], w["ln_b"], w["r_k"], w["member"], w["sb_wo"], w["rw_wo"], w["w_out"],
                    gs_blk=gs_blk, gr_blk=gs_blk + 1)
    out = _ffn(h2, w["ffn2_norm"], w["ffn2_wg"], w["ffn2_wu"], w["ffn2_wd"])

    heads = width // HEAD_DIM
    k_new = p3[:, :, width:2 * width].reshape(b, t, heads, HEAD_DIM)
    v_new = p3[:, :, 2 * width:3 * width].reshape(b, t, heads, HEAD_DIM)
    shift = jnp.concatenate([p3[:, t - 1:, 3 * width:6 * width],
                             p3[:, t - 1:, 6 * width + 2 * d:6 * width + 2 * d + lora_cols]], axis=-1)
    return out.reshape(b, t, d), k_new, v_new, wkv, shift


def _layer_weights(l, ffn1_norm, ffn1_w_gate, ffn1_w_up, ffn1_w_down, mix_norm, w_in, sb_q_norm, sb_k_norm, sb_w_o,
                   rwkv_mu, rwkv_w0, rwkv_w_w2, rwkv_a0, rwkv_w_a2, rwkv_w_g2, rwkv_k_k, rwkv_k_a, rwkv_r_k,
                   rwkv_ln_w, rwkv_ln_b, rwkv_w_o, w_out, ffn2_norm, ffn2_w_gate, ffn2_w_up, ffn2_w_down):
    d = w_in.shape[1]
    width = rwkv_w0.shape[1]
    heads = width // HEAD_DIM
    n_decay, n_iclr, n_gate = rwkv_w_w2.shape[1], rwkv_w_a2.shape[1], rwkv_w_g2.shape[1]
    lora_cols = n_decay + n_iclr + n_gate
    lora_w = -(-lora_cols // 512) * 512
    row = lambda a: a.reshape(1, -1).astype(F32)
    wi = w_in[l]
    w_in_p = jnp.concatenate([
        wi[:, :6 * width], wi[:, 6 * width + lora_cols:], wi[:, 6 * width:6 * width + lora_cols],
        jnp.zeros((d, lora_w - lora_cols), wi.dtype)], axis=1).astype(BF16)
    total = w_in_p.shape[1]
    head_gain = jnp.concatenate([jnp.tile(sb_q_norm[l], heads), jnp.tile(sb_k_norm[l], heads),
                                 jnp.ones((total - 2 * width,), F32)]).reshape(1, total)
    mu = rwkv_mu[l]

    def lora_pad(wm, r0):
        return jnp.zeros((lora_w, width), F32).at[r0:r0 + wm.shape[0]].set(wm).astype(BF16)

    hid = jnp.arange(width) // HEAD_DIM
    bd = (hid[:, None] == hid[None, :]).astype(BF16)
    member = (hid[:, None] == jnp.arange(LANES)[None, :]).astype(BF16)
    tk = ATT_TK
    ki = jnp.arange(tk)
    u2 = jnp.concatenate([(ki[None, :] > ki[:, None]).astype(BF16), jnp.ones((16, tk), BF16)], axis=0)
    ue = jnp.concatenate([(ki[:, None] > ki[None, :]).astype(BF16), jnp.ones((tk, LANES), BF16)], axis=1)
    ci = jnp.arange(RW_CHUNK)
    tri = (ci[:, None] >= ci[None, :]).astype(BF16)
    return {
        "ffn1_norm": row(ffn1_norm[l]), "ffn1_wg": ffn1_w_gate[l].astype(BF16), "ffn1_wu": ffn1_w_up[l].astype(BF16),
        "ffn1_wd": ffn1_w_down[l].astype(BF16),
        "ffn2_norm": row(ffn2_norm[l]), "ffn2_wg": ffn2_w_gate[l].astype(BF16), "ffn2_wu": ffn2_w_up[l].astype(BF16),
        "ffn2_wd": ffn2_w_down[l].astype(BF16),
        "mix_norm": row(mix_norm[l]), "w_in": w_in_p, "head_gain": head_gain, "bd": bd, "member": member, "u2": u2, "ue": ue, "tri": tri,
        "mu_rkv": row(mu[:3 * width]), "mu_lora": row(jnp.pad(mu[3 * width:], (0, lora_w - lora_cols))),
        "lora_cols": lora_cols,
        "w0": row(rwkv_w0[l]), "a0": row(rwkv_a0[l]), "k_k": row(rwkv_k_k[l]), "k_a": row(rwkv_k_a[l]),
        "ww2": lora_pad(rwkv_w_w2[l], 0), "wa2": lora_pad(rwkv_w_a2[l], n_decay),
        "wg2": lora_pad(rwkv_w_g2[l], n_decay + n_iclr),
        "ln_w": row(rwkv_ln_w[l]), "ln_b": row(rwkv_ln_b[l]), "r_k": row(rwkv_r_k[l]),
        "sb_wo": sb_w_o[l].astype(BF16), "rw_wo": rwkv_w_o[l].astype(BF16), "w_out": w_out[l].astype(BF16),
    }


def kernel(x_prompt, x_sample, cache_sb_k, cache_sb_v, state_rwkv_wkv, state_rwkv_shift, ffn1_norm, ffn1_w_gate, ffn1_w_up, ffn1_w_down, mix_norm, w_in, sb_q_norm, sb_k_norm, sb_w_o, rwkv_mu, rwkv_w0, rwkv_w_w2, rwkv_a0, rwkv_w_a2, rwkv_w_g2, rwkv_k_k, rwkv_k_a, rwkv_r_k, rwkv_ln_w, rwkv_ln_b, rwkv_w_o, w_out, ffn2_norm, ffn2_w_gate, ffn2_w_up, ffn2_w_down):
    depth = w_in.shape[0]
    yp, ys = x_prompt, x_sample
    bp = x_prompt.shape[0]
    width = rwkv_w0.shape[1]
    heads = width // HEAD_DIM
    rw_cols = state_rwkv_shift.shape[-1]
    outs = [[] for _ in range(8)]
    for l in range(depth):
        w = _layer_weights(l, ffn1_norm, ffn1_w_gate, ffn1_w_up, ffn1_w_down, mix_norm, w_in, sb_q_norm, sb_k_norm,
                           sb_w_o, rwkv_mu, rwkv_w0, rwkv_w_w2, rwkv_a0, rwkv_w_a2, rwkv_w_g2, rwkv_k_k, rwkv_k_a,
                           rwkv_r_k, rwkv_ln_w, rwkv_ln_b, rwkv_w_o, w_out, ffn2_norm, ffn2_w_gate, ffn2_w_up,
                           ffn2_w_down)
        wkv_zero = jnp.zeros((bp, heads, HEAD_DIM, HEAD_DIM), F32)
        shift_zero = jnp.zeros((bp, 1, rw_cols), F32)
        yp, kp, vp, wkvp, shp = _layer(yp, None, None, wkv_zero, shift_zero, w, tq=512, tc=256, scan_pairs=4)
        ys, kn, vn, wkvn, shn = _layer(ys, cache_sb_k[l], cache_sb_v[l], state_rwkv_wkv[l], state_rwkv_shift[l], w, tq=64, tc=64, scan_pairs=8)
        for lst, val in zip(outs, (kp, vp, wkvp, shp, kn, vn, wkvn, shn)):
            lst.append(val)
    return (yp, ys) + tuple(jnp.stack(o) for o in outs)
```

```python
import functools

import jax
import jax.numpy as jnp
from jax import lax
from jax.experimental import pallas as pl
from jax.experimental.pallas import tpu as pltpu

F32 = jnp.float32
BF16 = jnp.bfloat16

HEAD_DIM = 64
LANES = 128
NORM_EPS = 1e-6
GN_EPS = 64e-5
RW_CHUNK = 64
ATT_TK = 256

MIB = 1024 * 1024
LOG2_E = 1.4426950408889634
SIGN_BIT = 0x80000000
ATT_CHAIN_LANES = 256
ATT_SKEW = 2


def _nt(x, y):
    return lax.dot_general(x, y, (((1,), (1,)), ((), ())), preferred_element_type=F32)


def _tn(x, y):
    return lax.dot_general(x, y, (((0,), (0,)), ((), ())), preferred_element_type=F32)


def _mm(x, y):
    return jnp.dot(x, y, preferred_element_type=F32)


def _neg_abs(x):
    return lax.bitcast_convert_type(lax.bitcast_convert_type(x, jnp.uint32) | jnp.uint32(SIGN_BIT), F32)


def _split2(x):
    hi = x.astype(BF16)
    lo = (x - hi.astype(F32)).astype(BF16)
    return hi, lo


def _head_sum2(x, member):
    hi, lo = _split2(x)
    shi, slo = _split2(_mm(hi, member) + _mm(lo, member))
    return _nt(shi, member) + _nt(slo, member)


def _rms(x, g):
    ms = jnp.mean(x * x, axis=-1, keepdims=True)
    return x * lax.rsqrt(ms + NORM_EPS) * g


def _params(sem, vmem_mib):
    return pltpu.CompilerParams(dimension_semantics=sem, vmem_limit_bytes=vmem_mib * MIB)


def _ffn_body(x_ref, g_ref, wg_ref, wu_ref, wd_ref, o_ref, n_ref, acc_ref):
    f = pl.program_id(1)

    @pl.when(f == 0)
    def _():
        n_ref[...] = _rms(x_ref[...], g_ref[...]).astype(BF16)
        acc_ref[...] = jnp.zeros_like(acc_ref)

    n = n_ref[...]
    g = _mm(n, wg_ref[...])
    u = _mm(n, wu_ref[...])
    a = (g * jax.nn.sigmoid(g) * u).astype(BF16)
    acc_ref[...] += _mm(a, wd_ref[...])

    @pl.when(f == pl.num_programs(1) - 1)
    def _():
        o_ref[...] = x_ref[...] + 0.5 * acc_ref[...]


def _ffn(x, g, wg, wu, wd, *, tm=512, tf=512):
    n, d = x.shape
    ff = wg.shape[1]
    tm, tf = min(tm, n), min(tf, ff)
    return pl.pallas_call(
        _ffn_body,
        out_shape=jax.ShapeDtypeStruct((n, d), F32),
        grid=(n // tm, ff // tf),
        in_specs=[
            pl.BlockSpec((tm, d), lambda i, f: (i, 0)),
            pl.BlockSpec((1, d), lambda i, f: (0, 0)),
            pl.BlockSpec((d, tf), lambda i, f: (0, f)),
            pl.BlockSpec((d, tf), lambda i, f: (0, f)),
            pl.BlockSpec((tf, d), lambda i, f: (f, 0)),
        ],
        out_specs=pl.BlockSpec((tm, d), lambda i, f: (i, 0)),
        scratch_shapes=[pltpu.VMEM((tm, d), BF16), pltpu.VMEM((tm, d), F32)],
        compiler_params=_params(("parallel", "arbitrary"), 48),
        name="ffn",
    )(x, g, wg, wu, wd)


def _mix_body(h_ref, g_ref, w_ref, hg_ref, member_ref, o_ref, n_ref, *, n_norm_cols):
    j = pl.program_id(1)

    @pl.when(j == 0)
    def _():
        n_ref[...] = _rms(h_ref[...], g_ref[...]).astype(BF16)

    p = _mm(n_ref[...], w_ref[...])
    tn = p.shape[1]

    n_norm_tiles = -(-n_norm_cols // tn)
    for jj in range(n_norm_tiles):
        wn = min(tn, n_norm_cols - jj * tn)

        @pl.when(j == jj)
        def _():
            pn = p[:, :wn]
            ms = _head_sum2(pn * pn, member_ref[0:wn, :]) * (1.0 / HEAD_DIM)
            o_ref[:, :wn] = pn * lax.rsqrt(ms + NORM_EPS) * hg_ref[:, :wn]
            if wn < tn:
                o_ref[:, wn:] = p[:, wn:]

    @pl.when(j >= n_norm_tiles)
    def _():
        o_ref[...] = p


def _mix(h, g, w, hgain, member, *, n_norm_cols, tm=512):
    n, d = h.shape
    cols = w.shape[1]
    tm = min(tm, n)
    tn = next(c for c in (1536, 1024, 512) if cols % c == 0)
    member = member[:tn]
    return pl.pallas_call(
        functools.partial(_mix_body, n_norm_cols=n_norm_cols),
        out_shape=jax.ShapeDtypeStruct((n, cols), F32),
        grid=(n // tm, cols // tn),
        in_specs=[
            pl.BlockSpec((tm, d), lambda i, j: (i, 0)),
            pl.BlockSpec((1, d), lambda i, j: (0, 0)),
            pl.BlockSpec((d, tn), lambda i, j: (0, j)),
            pl.BlockSpec((1, tn), lambda i, j: (0, j)),
            pl.BlockSpec((tn, LANES), lambda i, j: (0, 0)),
        ],
        out_specs=pl.BlockSpec((tm, tn), lambda i, j: (i, j)),
        scratch_shapes=[pltpu.VMEM((tm, d), BF16)],
        compiler_params=_params(("parallel", "arbitrary"), 56),
        name="mix",
    )(h, g, w, hgain, member)


def _sb_items(items, ut, tk, run, pv):
    zs, parts = {}, {}

    def scores(n):
        _, q16, kblk, _, _ = items[n]
        zs[n] = _nt(kblk, q16)

    def keep_sums(n):
        mask = items[n][4]
        z = zs.pop(n)
        lp = jnp.log(1.0 + jnp.exp2(_neg_abs(z))) * LOG2_E
        log_beta = jnp.minimum(z, 0.0) - lp
        log_keep = log_beta - z
        if mask is not None:
            log_keep = jnp.where(mask, log_keep, 0.0)
        parts[n] = (log_beta, _mm(ut, log_keep.astype(BF16)))

    def weigh(n):
        ci, _, _, vt_rows, mask = items[n]
        log_beta, ext = parts.pop(n)
        w = jnp.exp2(log_beta + (ext[:tk] + run[ci][0:1]))
        if mask is not None:
            w = jnp.where(mask, w, 0.0)
        d = _mm(vt_rows, w.astype(BF16))
        pv[ci] = d if ci not in pv else pv[ci] + d
        run[ci] = run[ci] + ext[tk:tk + 8]

    for step in range(len(items) + 2 * ATT_SKEW):
        if step < len(items):
            scores(step)
        if 0 <= step - ATT_SKEW < len(items):
            keep_sums(step - ATT_SKEW)
        if 0 <= step - 2 * ATT_SKEW < len(items):
            weigh(step - 2 * ATT_SKEW)


def _sb_rows(items, ue, tk, run, pv):
    zs, parts = {}, {}

    def scores(n):
        _, qh, kth, _, _ = items[n]
        zs[n] = jnp.concatenate([_mm(q, kt) for q, kt in zip(qh, kth)], axis=0)

    def keep_sums(n):
        mask = items[n][4]
        z = zs.pop(n)
        lp = jnp.log(1.0 + jnp.exp2(_neg_abs(z))) * LOG2_E
        log_beta = jnp.minimum(z, 0.0) - lp
        log_keep = log_beta - z
        if mask is not None:
            log_keep = jnp.where(mask, log_keep, 0.0)
        parts[n] = (log_beta, _mm(log_keep.astype(BF16), ue))

    def weigh(n):
        ci, qh, _, vth, mask = items[n]
        log_beta, ext = parts.pop(n)
        tail = jnp.concatenate([ext[:, c0:c0 + LANES] + run[ci] for c0 in range(0, tk, LANES)], axis=1)
        w = jnp.exp2(log_beta + tail)
        if mask is not None:
            w = jnp.where(mask, w, 0.0)
        w = w.astype(BF16)
        tq = qh[0].shape[0]
        for h, vt in enumerate(vth):
            d = _nt(w[h * tq:(h + 1) * tq], vt)
            pv[ci, h] = d if (ci, h) not in pv else pv[ci, h] + d
        run[ci] = run[ci] + ext[:, tk:]

    for step in range(len(items) + 2 * ATT_SKEW):
        if step < len(items):
            scores(step)
        if 0 <= step - ATT_SKEW < len(items):
            keep_sums(step - ATT_SKEW)
        if 0 <= step - 2 * ATT_SKEW < len(items):
            weigh(step - 2 * ATT_SKEW)


def _attn_cached_body(q_ref, k_ref, v_ref, ckt_ref, cvt_ref, ue_ref, o_ref, qs, carry, acc, *,
                      tq, tk, n_heads, chunk_blocks, per_chain):
    j = pl.program_id(1)
    n_chains = n_heads // per_chain
    ue = ue_ref[...]
    heads_of = lambda c: range(c * per_chain, (c + 1) * per_chain)

    def run_items(items):
        run = {c: carry[c] for c in range(n_chains)}
        pv = {}
        _sb_rows(items, ue, tk, run, pv)
        for c in range(n_chains):
            carry[c] = run[c]
            for i, h in enumerate(heads_of(c)):
                acc[h] += pv[c, i]

    @pl.when(j == 0)
    def _():
        carry[...] = jnp.zeros_like(carry)
        acc[...] = jnp.zeros_like(acc)
        for h in range(n_heads):
            qs[h] = (q_ref[0, :, h * HEAD_DIM:(h + 1) * HEAD_DIM] * (HEAD_DIM ** -0.5 * LOG2_E)).astype(BF16)
        row = lax.broadcasted_iota(jnp.int32, (per_chain * tq, tk), 0) % tq
        col = lax.broadcasted_iota(jnp.int32, (per_chain * tq, tk), 1)
        pad = jnp.zeros((HEAD_DIM, tk - tq), F32)
        items = []
        for c in range(n_chains):
            def new_t(ref):
                return [jnp.concatenate([ref[0, :, h * HEAD_DIM:(h + 1) * HEAD_DIM].T, pad], axis=1).astype(BF16)
                        for h in heads_of(c)]
            items.append((c, [qs[h] for h in heads_of(c)], new_t(k_ref), new_t(v_ref), col < row))
        run_items(items)

    items = []
    for m in reversed(range(chunk_blocks)):
        keys = slice(m * tk, (m + 1) * tk)
        for c in range(n_chains):
            items.append((c, [qs[h] for h in heads_of(c)],
                          [ckt_ref[0, h, :, keys].astype(BF16) for h in heads_of(c)],
                          [cvt_ref[0, h, :, keys].astype(BF16) for h in heads_of(c)], None))
    run_items(items)

    @pl.when(j == pl.num_programs(1) - 1)
    def _():
        o_ref[0] = jnp.concatenate([acc[h] for h in range(n_heads)], axis=1)


def _attention_cached(p3, cache_k, cache_v, ue, *, width, chunk_keys=1024, per_chain=4):
    b, t, _ = p3.shape
    tk = ATT_TK
    _, p_len, n_heads, _ = cache_k.shape
    chunk_keys = min(chunk_keys, p_len)
    n_chunks = p_len // chunk_keys
    ckt = jnp.transpose(cache_k, (0, 2, 3, 1))
    cvt = jnp.transpose(cache_v, (0, 2, 3, 1))
    cache_spec = pl.BlockSpec((1, n_heads, HEAD_DIM, chunk_keys), lambda bi, j: (bi, 0, 0, n_chunks - 1 - j))
    return pl.pallas_call(
        functools.partial(_attn_cached_body, tq=t, tk=tk, n_heads=n_heads, chunk_blocks=chunk_keys // tk,
                          per_chain=per_chain),
        out_shape=jax.ShapeDtypeStruct((b, t, width), F32),
        grid=(b, n_chunks),
        in_specs=[
            pl.BlockSpec((1, t, width), lambda bi, j: (bi, 0, 0)),
            pl.BlockSpec((1, t, width), lambda bi, j: (bi, 0, 1)),
            pl.BlockSpec((1, t, width), lambda bi, j: (bi, 0, 2)),
            cache_spec, cache_spec,
            pl.BlockSpec(ue.shape, lambda bi, j: (0, 0)),
        ],
        out_specs=pl.BlockSpec((1, t, width), lambda bi, j: (bi, 0, 0)),
        scratch_shapes=[
            pltpu.VMEM((n_heads, t, HEAD_DIM), BF16),
            pltpu.VMEM((n_heads // per_chain, per_chain * t, LANES), F32),
            pltpu.VMEM((n_heads, t, HEAD_DIM), F32),
        ],
        compiler_params=_params(("parallel", "arbitrary"), 48),
        name="sb_attention_cached",
    )(p3, p3, p3, ckt, cvt, ue)


def _attn_body(*refs, tq, tk, t_new, n_new_blocks, n_past_blocks, past_unroll, new_unroll, chains):
    if n_past_blocks:
        q_ref, k_ref, v_ref, pk_ref, pv_ref, ut_ref, o_ref, kb, vt, qs, carry, acc = refs
    else:
        q_ref, k_ref, v_ref, ut_ref, o_ref, kb, vt, qs, carry, acc = refs
        pk_ref = pv_ref = None
    i = pl.program_id(2)

    def pad_rows(x):
        if x.shape[0] == tk:
            return x
        return jnp.concatenate([x, jnp.zeros((tk - x.shape[0], x.shape[1]), x.dtype)], axis=0)

    @pl.when(i == 0)
    def _():
        if n_new_blocks == 1:
            kb[...] = pad_rows(k_ref[0]).astype(BF16)
            vt[0] = pad_rows(v_ref[0]).T.astype(BF16)
        else:
            def fill(blk, _):
                rows = pl.ds(pl.multiple_of(blk * tk, tk), tk)
                kb[rows, :] = k_ref[0, rows, :].astype(BF16)
                vt[blk] = v_ref[0, rows, :].T.astype(BF16)
                return 0

            lax.fori_loop(0, n_new_blocks, fill, 0)

    q = q_ref[0] * (HEAD_DIM ** -0.5 * LOG2_E)
    n_heads = q.shape[1] // HEAD_DIM
    head_of_lane = lax.broadcasted_iota(jnp.int32, q.shape, 1) // HEAD_DIM
    zero = jnp.zeros_like(q)
    for h in range(n_heads):
        qs[h * tq:(h + 1) * tq, :] = jnp.where(head_of_lane == h, q, zero).astype(BF16)
    carry[...] = jnp.zeros_like(carry)
    acc[...] = jnp.zeros_like(acc)

    def span(blocks):
        q16 = [qs[c0:c1, :] for c0, c1, _, _ in chains]
        cw = chains[0][1] - chains[0][0]
        row = lax.broadcasted_iota(jnp.int32, (tk, cw), 0)
        lane = lax.broadcasted_iota(jnp.int32, (tk, cw), 1)
        mask = lambda off, c0: None if off is None else (row + off) < (lane + c0) % tq
        items = [(ci, q16[ci], kblk, vtblk[r0:r1], mask(off, c0))
                 for kblk, vtblk, off in blocks for ci, (c0, _, r0, r1) in enumerate(chains)]
        run = {ci: carry[:, c0:c1] for ci, (c0, c1, _, _) in enumerate(chains)}
        pv = {}
        _sb_items(items, ut_ref[...], tk, run, pv)
        for ci, (c0, c1, r0, r1) in enumerate(chains):
            carry[:, c0:c1] = run[ci]
            acc[r0:r1, c0:c1] += pv[ci]

    q0 = i * tq
    n_diag = max(tq // tk, 1)
    diag = []
    for m in reversed(range(n_diag)):
        blk = q0 // tk + m
        diag.append((kb[pl.ds(pl.multiple_of(blk * tk, tk), tk), :], vt[blk], m * tk))
    span(diag)

    def new_blocks(last, count):
        blocks = []
        for m in range(count):
            blk = last - m
            blocks.append((kb[pl.ds(pl.multiple_of(blk * tk, tk), tk), :], vt[blk], None))
        span(blocks)

    last = q0 // tk - 1
    per_trip = new_unroll
    while per_trip >= n_diag:
        def step(it, _, last=last, per_trip=per_trip):
            new_blocks(last - it * per_trip, per_trip)
            return 0

        trips = (last + 1) // per_trip
        lax.fori_loop(0, trips, step, 0)
        last = last - trips * per_trip
        per_trip //= 2

    if n_past_blocks:
        def past_step(it, _):
            blocks = []
            for m in range(past_unroll):
                k0 = pl.multiple_of((n_past_blocks - 1 - it * past_unroll - m) * tk, tk)
                blocks.append((pk_ref[0, pl.ds(k0, tk), :].astype(BF16),
                               pv_ref[0, pl.ds(k0, tk), :].T.astype(BF16), None))
            span(blocks)
            return 0

        lax.fori_loop(0, n_past_blocks // past_unroll, past_step, 0)

    out_t = acc[...].T
    out = out_t[:tq]
    for h in range(1, n_heads):
        out = jnp.where(head_of_lane == h, out_t[h * tq:(h + 1) * tq], out)
    o_ref[0] = out


def _attention(p3, past_k, past_v, u2, *, n_pairs, q_blk0, k_blk0, v_blk0, tq):
    b, t, _ = p3.shape
    tk = ATT_TK
    tq = min(tq, t)
    n_new_blocks = -(-t // tk)
    if tq % LANES == 0:
        npb = 1
        cw = min(tq, ATT_CHAIN_LANES)
        chains = tuple((h * tq + c0, h * tq + c0 + cw, h * HEAD_DIM, (h + 1) * HEAD_DIM)
                       for h in range(2) for c0 in range(0, tq, cw))
    else:
        npb = max(1, (2 * LANES) // (2 * tq))
        chains = ((0, 2 * npb * tq, 0, npb * LANES),)
    lw = npb * LANES
    has_past = past_k is not None
    n_past_blocks = past_k.shape[1] // tk if has_past else 0
    in_specs = [
        pl.BlockSpec((1, tq, lw), lambda bi, p, i: (bi, i, q_blk0 // npb + p)),
        pl.BlockSpec((1, t, lw), lambda bi, p, i: (bi, 0, k_blk0 // npb + p)),
        pl.BlockSpec((1, t, lw), lambda bi, p, i: (bi, 0, v_blk0 // npb + p)),
    ]
    args = [p3, p3, p3]
    if has_past:
        pp = past_k.shape[1]
        in_specs += [pl.BlockSpec((1, pp, lw), lambda bi, p, i: (bi, 0, p))] * 2
        args += [past_k, past_v]
    in_specs.append(pl.BlockSpec(u2.shape, lambda bi, p, i: (0, 0)))
    args.append(u2)
    return pl.pallas_call(
        functools.partial(_attn_body, tq=tq, tk=tk, t_new=t, n_new_blocks=n_new_blocks, n_past_blocks=n_past_blocks,
                          past_unroll=min(4, max(n_past_blocks, 1)), new_unroll=4 * max(tq // tk, 1), chains=chains),
        out_shape=jax.ShapeDtypeStruct((b, t, n_pairs * LANES), F32),
        grid=(b, n_pairs // npb, t // tq),
        in_specs=in_specs,
        out_specs=pl.BlockSpec((1, tq, lw), lambda bi, p, i: (bi, i, p)),
        scratch_shapes=[
            pltpu.VMEM((n_new_blocks * tk, lw), BF16),
            pltpu.VMEM((n_new_blocks, lw, tk), BF16),
            pltpu.VMEM((2 * npb * tq, lw), BF16),
            pltpu.VMEM((8, 2 * npb * tq), F32),
            pltpu.VMEM((lw, 2 * npb * tq), F32),
        ],
        compiler_params=_params(("parallel", "parallel", "arbitrary"), 48),
        name="sb_attention",
    )(*args)


def _prep_body(prkv_ref, plora_ref, s_rkv_ref, s_lora_ref, mu_rkv_ref, mu_lora_ref, w0_ref, a0_ref, kk_ref, ka_ref,
               ww2_ref, wa2_ref, wg2_ref, bd_ref,
               r_o, wl_o, k_o, v_o, av_o, bv_o, g_o, c_rkv, c_lora, *, tc, width):
    t = pl.program_id(1)

    @pl.when(t == 0)
    def _():
        c_rkv[0:1, :] = s_rkv_ref[0]
        c_lora[0:1, :] = s_lora_ref[0]

    def token_mix(p, prev, mu):
        row = lax.broadcasted_iota(jnp.int32, p.shape, 0)
        shifted = jnp.where(row == 0, prev, pltpu.roll(p, 1, 0))
        return p + (shifted - p) * mu

    def rkv_seg(s):
        cs = slice(s * width, (s + 1) * width)
        p = prkv_ref[0, :, cs]
        x = token_mix(p, c_rkv[0:1, cs], mu_rkv_ref[:, cs])
        c_rkv[0:1, cs] = p[tc - 1:tc, :]
        return x

    pl_ = plora_ref[0]
    xl = token_mix(pl_, c_lora[0:1, :], mu_lora_ref[...])
    c_lora[0:1, :] = pl_[tc - 1:tc, :]

    r_o[0] = rkv_seg(0)
    v_o[0] = rkv_seg(2)
    xk = rkv_seg(1)

    dec = w0_ref[...] + _mm(jnp.tanh(xl).astype(BF16), ww2_ref[...])
    nd = -dec
    softplus = jnp.maximum(nd, 0.0) + jnp.log(1.0 + jnp.exp(-jnp.abs(nd)))
    w_log = -softplus - 0.5
    wl_o[0] = -jnp.exp(w_log)
    a = jax.nn.sigmoid(a0_ref[...] + _mm(xl.astype(BF16), wa2_ref[...]))
    g_o[0] = _mm(jax.nn.sigmoid(xl).astype(BF16), wg2_ref[...])
    kk = xk * kk_ref[...]
    k_o[0] = xk * (1.0 + (a - 1.0) * ka_ref[...])
    norm = jnp.sqrt(_head_sum2(kk * kk, bd_ref[...]))
    kk = kk / jnp.maximum(norm, 1e-12)
    av_o[0] = -kk
    bv_o[0] = kk * a


def _rwkv_prep(p3, s_rkv, s_lora, mu_rkv, mu_lora, w0, a0, k_k, k_a, ww2, wa2, wg2, bd, *, rkv_blk, lora_blk, tc):
    b, t, _ = p3.shape
    width = w0.shape[1]
    lw = mu_lora.shape[1]
    tc = min(tc, t)
    const = lambda shape: pl.BlockSpec(shape, lambda bi, ti: (0,) * len(shape))
    out_spec = pl.BlockSpec((1, tc, width), lambda bi, ti: (bi, ti, 0))
    return pl.pallas_call(
        functools.partial(_prep_body, tc=tc, width=width),
        out_shape=[jax.ShapeDtypeStruct((b, t, width), F32)] * 7,
        grid=(b, t // tc),
        in_specs=[
            pl.BlockSpec((1, tc, 3 * width), lambda bi, ti: (bi, ti, rkv_blk)),
            pl.BlockSpec((1, tc, lw), lambda bi, ti: (bi, ti, lora_blk)),
            pl.BlockSpec((1, 1, 3 * width), lambda bi, ti: (bi, 0, 0)),
            pl.BlockSpec((1, 1, lw), lambda bi, ti: (bi, 0, 0)),
            const((1, 3 * width)), const((1, lw)),
            const((1, width)), const((1, width)), const((1, width)), const((1, width)),
            const((lw, width)), const((lw, width)), const((lw, width)),
            const((width, LANES)),
        ],
        out_specs=[out_spec] * 7,
        scratch_shapes=[pltpu.VMEM((8, 3 * width), F32), pltpu.VMEM((8, lw), F32)],
        compiler_params=_params(("parallel", "arbitrary"), 48),
        name="rwkv_prep",
    )(p3, p3, s_rkv, s_lora, mu_rkv, mu_lora, w0, a0, k_k, k_a, ww2, wa2, wg2, bd)


def _scan_body(r_ref, wl_ref, k_ref, v_ref, a_ref, b_ref, s0_ref, tri_ref, y_ref, sout_ref, s_scr, *,
               n_chunks, n_pairs):
    c_len = RW_CHUNK
    t = pl.program_id(2)

    @pl.when(t == 0)
    def _():
        s_scr[...] = s0_ref[0]

    row = lax.broadcasted_iota(jnp.int32, (c_len, c_len), 0)
    col = lax.broadcasted_iota(jnp.int32, (c_len, c_len), 1)
    strict = row > col
    incl = row >= col
    lane = lax.broadcasted_iota(jnp.int32, (c_len, LANES), 1)
    first = lane < HEAD_DIM
    brow = lax.broadcasted_iota(jnp.int32, (LANES, LANES), 0)
    bcol = lax.broadcasted_iota(jnp.int32, (LANES, LANES), 1)
    same_head = (brow // HEAD_DIM) == (bcol // HEAD_DIM)
    eye = brow == bcol
    tri = tri_ref[...]
    bf = lambda x: x.astype(BF16)

    cps = [(c, p) for c in range(n_chunks) for p in range(n_pairs)]
    sls = {(c, p): (slice(c * c_len, (c + 1) * c_len), slice(p * LANES, (p + 1) * LANES)) for c, p in cps}
    tiles = {}
    for c in range(n_chunks):
        sl = slice(c * c_len, (c + 1) * c_len)
        wl = wl_ref[0, sl, :]
        hi = wl.astype(BF16)
        rem = wl - hi.astype(F32)
        mid = rem.astype(BF16)
        lo = (rem - mid.astype(F32)).astype(BF16)
        cum = _mm(tri, hi) + _mm(tri, mid) + _mm(tri, lo)
        tot = cum[c_len - 1:c_len, :]
        e_neg = jnp.exp(-cum)
        e_end = jnp.exp(tot - cum)
        av, bv, kv, vv = a_ref[0, sl, :], b_ref[0, sl, :], k_ref[0, sl, :], v_ref[0, sl, :]
        tiles[c] = dict(at=av * jnp.exp(cum - wl), rt=r_ref[0, sl, :] * jnp.exp(cum), bt=bf(bv * e_neg),
                        kt=bf(kv * e_neg), bh=bv * e_end, kh=kv * e_end, vv=vv, etot=jnp.exp(tot))
    tile = lambda name, c, p: tiles[c][name][:, p * LANES:(p + 1) * LANES]

    chains = [(c, p, h) for c, p in cps for h in range(2)]
    mbk = {}
    for c, p, h in chains:
        sel = first if h == 0 else jnp.logical_not(first)
        at, rt = tile("at", c, p), tile("rt", c, p)
        zero = jnp.zeros_like(at)
        ar = jnp.concatenate([jnp.where(sel, at, zero), jnp.where(sel, rt, zero)], axis=0).astype(BF16)
        mbk[c, p, h] = (_nt(ar, tile("bt", c, p)), _nt(ar, tile("kt", c, p)))
    m_ab, p_rb, m_ak, p_rk, tm = {}, {}, {}, {}, {}
    for ch in chains:
        mb, mk = mbk[ch]
        m_ab[ch] = jnp.where(strict, mb[:c_len], 0.0)
        p_rb[ch] = bf(jnp.where(incl, mb[c_len:], 0.0))
        m_ak[ch] = bf(jnp.where(strict, mk[:c_len], 0.0))
        p_rk[ch] = bf(jnp.where(incl, mk[c_len:], 0.0))
        tm[ch] = jnp.where(row == col, 1.0, 0.0) + jnp.where((row // 2) == (col // 2), m_ab[ch], 0.0)
    s = 2
    while s < c_len:
        off = jnp.logical_and((row // (2 * s)) == (col // (2 * s)), (row // s) != (col // s))
        half = {ch: bf(_mm(bf(tm[ch]), bf(jnp.where(off, m_ab[ch], 0.0)))) for ch in chains}
        tm = {ch: tm[ch] + _mm(half[ch], bf(tm[ch])) for ch in chains}
        s *= 2
    t16 = {ch: bf(tm[ch]) for ch in chains}
    mv = {(c, p, h): _mm(m_ak[c, p, h], bf(tile("vv", c, p))) for c, p, h in chains}
    w1 = {(c, p, h): _mm(t16[c, p, h], bf(tile("at", c, p))) for c, p, h in chains}
    w2 = {ch: _mm(t16[ch], bf(mv[ch])) for ch in chains}
    qc = {(c, p, h): tile("rt", c, p) + _mm(p_rb[c, p, h], bf(w1[c, p, h])) for c, p, h in chains}
    y1 = {(c, p, h): _mm(p_rb[c, p, h], bf(w2[c, p, h])) + _mm(p_rk[c, p, h], bf(tile("vv", c, p)))
          for c, p, h in chains}
    both = lambda d, c, p: jnp.where(first, d[c, p, 0], d[c, p, 1])
    ac_t, dc_t, qcs, y1s = {}, {}, {}, {}
    for c, p in cps:
        bh, kh, vv = tile("bh", c, p), tile("kh", c, p), tile("vv", c, p)
        w1p, w2p = both(w1, c, p), both(w2, c, p)
        a_full = jnp.where(same_head, _tn(bf(w1p), bf(bh)), 0.0) + jnp.where(eye, tile("etot", c, p), 0.0)
        ac_t[c, p] = _split2(a_full)
        dc_t[c, p] = jnp.where(same_head, _tn(bf(jnp.concatenate([w2p, vv], axis=0)),
                                              bf(jnp.concatenate([bh, kh], axis=0))), 0.0)
        qcs[c, p], y1s[c, p] = bf(both(qc, c, p)), both(y1, c, p)
    for p in range(n_pairs):
        state = s_scr[p]
        for c in range(n_chunks):
            rs, ls = sls[c, p]
            s_hi, s_lo = _split2(state)
            a_hi, a_lo = ac_t[c, p]
            y_ref[0, rs, ls] = _nt(qcs[c, p], s_hi) + y1s[c, p]
            state = _mm(s_hi, a_hi) + _mm(s_hi, a_lo) + _mm(s_lo, a_hi) + dc_t[c, p]
        s_scr[p] = state

    @pl.when(t == pl.num_programs(2) - 1)
    def _():
        sout_ref[0] = s_scr[...]


def _rwkv_scan(r, wl, k, v, av, bv, s0_bd, tri, *, tc, pairs_per_step):
    b, t, width = r.shape
    n_pairs = width // LANES
    tc = min(tc, t)
    npb = pairs_per_step
    seq = pl.BlockSpec((1, tc, npb * LANES), lambda bi, p, ti: (bi, ti, p))
    state = pl.BlockSpec((1, npb, LANES, LANES), lambda bi, p, ti: (bi, p, 0, 0))
    return pl.pallas_call(
        functools.partial(_scan_body, n_chunks=tc // RW_CHUNK, n_pairs=npb),
        out_shape=[jax.ShapeDtypeStruct((b, t, width), F32), jax.ShapeDtypeStruct(s0_bd.shape, F32)],
        grid=(b, n_pairs // npb, t // tc),
        in_specs=[seq] * 6 + [state, pl.BlockSpec(tri.shape, lambda bi, p, ti: (0, 0))],
        out_specs=[seq, state],
        scratch_shapes=[pltpu.VMEM((npb, LANES, LANES), F32)],
        compiler_params=_params(("parallel", "parallel", "arbitrary"), 32),
        name="rwkv_scan",
    )(r, wl, k, v, av, bv, s0_bd, tri)


def _out_body(h_ref, osb_ref, y_ref, r_ref, k_ref, v_ref, g_ref, gs_ref, gr_ref, lnw_ref, lnb_ref, rk_ref, bd_ref,
              wso_ref, wro_ref, wout_ref, o_ref, m_ref):
    j = pl.program_id(1)

    @pl.when(j == 0)
    def _():
        bd = bd_ref[...]
        y = y_ref[...]
        mu = _head_sum2(y, bd) * (1.0 / HEAD_DIM)
        d = y - mu
        var = _head_sum2(d * d, bd) * (1.0 / HEAD_DIM)
        yn = d * lax.rsqrt(var + GN_EPS) * lnw_ref[...] + lnb_ref[...]
        bonus = _head_sum2(r_ref[...] * k_ref[...] * rk_ref[...], bd) * v_ref[...]
        yy = ((yn + bonus) * g_ref[...]).astype(BF16)
        o_sb = _mm(osb_ref[...].astype(BF16), wso_ref[...])
        o_rw = _mm(yy, wro_ref[...])
        merged = jax.nn.sigmoid(gs_ref[...]) * o_sb + jax.nn.sigmoid(gr_ref[...]) * o_rw
        m_ref[...] = merged.astype(BF16)

    o_ref[...] = h_ref[...] + _mm(m_ref[...], wout_ref[...])


def _merge_out(h, o_sb, y, r, k, v, g, p2, lnw, lnb, rk, bd, wso, wro, wout, *, gs_blk, gr_blk, tm=128, tn=2048):
    n, d = h.shape
    width = o_sb.shape[1]
    tm, tn = min(tm, n), min(tn, d)
    tok = pl.BlockSpec((tm, width), lambda i, j: (i, 0))
    const = lambda shape: pl.BlockSpec(shape, lambda i, j: (0,) * len(shape))
    return pl.pallas_call(
        _out_body,
        out_shape=jax.ShapeDtypeStruct((n, d), F32),
        grid=(n // tm, d // tn),
        in_specs=[
            pl.BlockSpec((tm, tn), lambda i, j: (i, j)),
            tok, tok, tok, tok, tok, tok,
            pl.BlockSpec((tm, d), lambda i, j: (i, gs_blk)),
            pl.BlockSpec((tm, d), lambda i, j: (i, gr_blk)),
            const((1, width)), const((1, width)), const((1, width)),
            const((width, LANES)),
            const((width, d)), const((width, d)),
            pl.BlockSpec((d, tn), lambda i, j: (0, j)),
        ],
        out_specs=pl.BlockSpec((tm, tn), lambda i, j: (i, j)),
        scratch_shapes=[pltpu.VMEM((tm, d), BF16)],
        compiler_params=_params(("parallel", "arbitrary"), 56),
        name="merge_out",
    )(h, o_sb, y, r, k, v, g, p2, p2, lnw, lnb, rk, bd, wso, wro, wout)


def _layer(x, past_k, past_v, wkv0, shift0, w, *, tq, tc, scan_pairs):
    b, t, d = x.shape
    n = b * t
    width = w["w0"].shape[1]
    h1 = _ffn(x.reshape(n, d), w["ffn1_norm"], w["ffn1_wg"], w["ffn1_wu"], w["ffn1_wd"])
    p2 = _mix(h1, w["mix_norm"], w["w_in"], w["head_gain"], w["member_mix"], n_norm_cols=2 * width)
    p3 = p2.reshape(b, t, -1)
    n_pairs = width // LANES
    if past_k is None:
        o_sb = _attention(p3, None, None, w["u2"], n_pairs=n_pairs, q_blk0=0, k_blk0=n_pairs, v_blk0=2 * n_pairs,
                          tq=tq)
    else:
        o_sb = _attention_cached(p3, past_k, past_v, w["ue"], width=width)

    lora_w = w["mu_lora"].shape[1]
    lora_blk = (6 * width + 2 * d) // lora_w
    lora_cols = w["lora_cols"]
    s_rkv = shift0[:, :, :3 * width]
    s_lora = jnp.pad(shift0[:, :, 3 * width:], ((0, 0), (0, 0), (0, lora_w - lora_cols)))
    r, wl, k, v, av, bv, g = _rwkv_prep(
        p3, s_rkv, s_lora, w["mu_rkv"], w["mu_lora"], w["w0"], w["a0"], w["k_k"], w["k_a"],
        w["ww2"], w["wa2"], w["wg2"], w["member"], rkv_blk=1, lora_blk=lora_blk, tc=tc)

    s0 = wkv0.reshape(b, n_pairs, 2, HEAD_DIM, HEAD_DIM)
    z = jnp.zeros_like(s0[:, :, 0])
    s0_bd = jnp.concatenate([jnp.concatenate([s0[:, :, 0], z], axis=-1),
                             jnp.concatenate([z, s0[:, :, 1]], axis=-1)], axis=-2)
    y, s_bd = _rwkv_scan(r, wl, k, v, av, bv, s0_bd, w["tri"], tc=tc, pairs_per_step=scan_pairs)
    wkv = jnp.stack([s_bd[:, :, :HEAD_DIM, :HEAD_DIM], s_bd[:, :, HEAD_DIM:, HEAD_DIM:]], axis=2)
    wkv = wkv.reshape(b, 2 * n_pairs, HEAD_DIM, HEAD_DIM)

    flat = lambda a: a.reshape(n, width)
    gs_blk = (6 * width) // d
    h2 = _merge_out(h1, flat(o_sb), flat(y), flat(r), flat(k), flat(v), flat(g), p2,
                    w["ln_w"], w["ln_b"], w["r_k"], w["member"], w["sb_wo"], w["rw_wo"], w["w_out"],
                    gs_blk=gs_blk, gr_blk=gs_blk + 1)
    out = _ffn(h2, w["ffn2_norm"], w["ffn2_wg"], w["ffn2_wu"], w["ffn2_wd"])

    heads = width // HEAD_DIM
    k_new = p3[:, :, width:2 * width].reshape(b, t, heads, HEAD_DIM)
    v_new = p3[:, :, 2 * width:3 * width].reshape(b, t, heads, HEAD_DIM)
    shift = jnp.concatenate([p3[:, t - 1:, 3 * width:6 * width],
                             p3[:, t - 1:, 6 * width + 2 * d:6 * width + 2 * d + lora_cols]], axis=-1)
    return out.reshape(b, t, d), k_new, v_new, wkv, shift


def _layer_weights(l, ffn1_norm, ffn1_w_gate, ffn1_w_up, ffn1_w_down, mix_norm, w_in, sb_q_norm, sb_k_norm, sb_w_o,
                   rwkv_mu, rwkv_w0, rwkv_w_w2, rwkv_a0, rwkv_w_a2, rwkv_w_g2, rwkv_k_k, rwkv_k_a, rwkv_r_k,
                   rwkv_ln_w, rwkv_ln_b, rwkv_w_o, w_out, ffn2_norm, ffn2_w_gate, ffn2_w_up, ffn2_w_down):
    d = w_in.shape[1]
    width = rwkv_w0.shape[1]
    heads = width // HEAD_DIM
    n_decay, n_iclr, n_gate = rwkv_w_w2.shape[1], rwkv_w_a2.shape[1], rwkv_w_g2.shape[1]
    lora_cols = n_decay + n_iclr + n_gate
    lora_w = -(-lora_cols // 512) * 512
    row = lambda a: a.reshape(1, -1).astype(F32)
    wi = w_in[l]
    w_in_p = jnp.concatenate([
        wi[:, :6 * width], wi[:, 6 * width + lora_cols:], wi[:, 6 * width:6 * width + lora_cols],
        jnp.zeros((d, lora_w - lora_cols), wi.dtype)], axis=1).astype(BF16)
    total = w_in_p.shape[1]
    head_gain = jnp.concatenate([jnp.tile(sb_q_norm[l], heads), jnp.tile(sb_k_norm[l], heads),
                                 jnp.ones((total - 2 * width,), F32)]).reshape(1, total)
    mu = rwkv_mu[l]

    def lora_pad(wm, r0):
        return jnp.zeros((lora_w, width), F32).at[r0:r0 + wm.shape[0]].set(wm).astype(BF16)

    hid = jnp.arange(width) // HEAD_DIM
    member_mix = ((jnp.arange(1536) // HEAD_DIM)[:, None] == jnp.arange(LANES)[None, :]).astype(BF16)
    member = (hid[:, None] == jnp.arange(LANES)[None, :]).astype(BF16)
    tk = ATT_TK
    ki = jnp.arange(tk)
    u2 = jnp.concatenate([(ki[None, :] > ki[:, None]).astype(BF16), jnp.ones((16, tk), BF16)], axis=0)
    ue = jnp.concatenate([(ki[:, None] > ki[None, :]).astype(BF16), jnp.ones((tk, LANES), BF16)], axis=1)
    ci = jnp.arange(RW_CHUNK)
    tri = (ci[:, None] >= ci[None, :]).astype(BF16)
    return {
        "ffn1_norm": row(ffn1_norm[l]), "ffn1_wg": ffn1_w_gate[l].astype(BF16), "ffn1_wu": ffn1_w_up[l].astype(BF16),
        "ffn1_wd": ffn1_w_down[l].astype(BF16),
        "ffn2_norm": row(ffn2_norm[l]), "ffn2_wg": ffn2_w_gate[l].astype(BF16), "ffn2_wu": ffn2_w_up[l].astype(BF16),
        "ffn2_wd": ffn2_w_down[l].astype(BF16),
        "mix_norm": row(mix_norm[l]), "w_in": w_in_p, "head_gain": head_gain, "member_mix": member_mix, "member": member, "u2": u2, "ue": ue, "tri": tri,
        "mu_rkv": row(mu[:3 * width]), "mu_lora": row(jnp.pad(mu[3 * width:], (0, lora_w - lora_cols))),
        "lora_cols": lora_cols,
        "w0": row(rwkv_w0[l]), "a0": row(rwkv_a0[l]), "k_k": row(rwkv_k_k[l]), "k_a": row(rwkv_k_a[l]),
        "ww2": lora_pad(rwkv_w_w2[l], 0), "wa2": lora_pad(rwkv_w_a2[l], n_decay),
        "wg2": lora_pad(rwkv_w_g2[l], n_decay + n_iclr),
        "ln_w": row(rwkv_ln_w[l]), "ln_b": row(rwkv_ln_b[l]), "r_k": row(rwkv_r_k[l]),
        "sb_wo": sb_w_o[l].astype(BF16), "rw_wo": rwkv_w_o[l].astype(BF16), "w_out": w_out[l].astype(BF16),
    }


def kernel(x_prompt, x_sample, cache_sb_k, cache_sb_v, state_rwkv_wkv, state_rwkv_shift, ffn1_norm, ffn1_w_gate, ffn1_w_up, ffn1_w_down, mix_norm, w_in, sb_q_norm, sb_k_norm, sb_w_o, rwkv_mu, rwkv_w0, rwkv_w_w2, rwkv_a0, rwkv_w_a2, rwkv_w_g2, rwkv_k_k, rwkv_k_a, rwkv_r_k, rwkv_ln_w, rwkv_ln_b, rwkv_w_o, w_out, ffn2_norm, ffn2_w_gate, ffn2_w_up, ffn2_w_down):
    depth = w_in.shape[0]
    yp, ys = x_prompt, x_sample
    bp = x_prompt.shape[0]
    width = rwkv_w0.shape[1]
    heads = width // HEAD_DIM
    rw_cols = state_rwkv_shift.shape[-1]
    outs = [[] for _ in range(8)]
    for l in range(depth):
        w = _layer_weights(l, ffn1_norm, ffn1_w_gate, ffn1_w_up, ffn1_w_down, mix_norm, w_in, sb_q_norm, sb_k_norm,
                           sb_w_o, rwkv_mu, rwkv_w0, rwkv_w_w2, rwkv_a0, rwkv_w_a2, rwkv_w_g2, rwkv_k_k, rwkv_k_a,
                           rwkv_r_k, rwkv_ln_w, rwkv_ln_b, rwkv_w_o, w_out, ffn2_norm, ffn2_w_gate, ffn2_w_up,
                           ffn2_w_down)
        wkv_zero = jnp.zeros((bp, heads, HEAD_DIM, HEAD_DIM), F32)
        shift_zero = jnp.zeros((bp, 1, rw_cols), F32)
        yp, kp, vp, wkvp, shp = _layer(yp, None, None, wkv_zero, shift_zero, w, tq=512, tc=256, scan_pairs=4)
        ys, kn, vn, wkvn, shn = _layer(ys, cache_sb_k[l], cache_sb_v[l], state_rwkv_wkv[l], state_rwkv_shift[l], w, tq=64, tc=64, scan_pairs=8)
        for lst, val in zip(outs, (kp, vp, wkvp, shp, kn, vn, wkvn, shn)):
            lst.append(val)
    return (yp, ys) + tuple(jnp.stack(o) for o in outs)
```

```python
import functools

import jax
import jax.numpy as jnp
from jax import lax
from jax.experimental import pallas as pl
from jax.experimental.pallas import tpu as pltpu

F32 = jnp.float32
BF16 = jnp.bfloat16

HEAD_DIM = 64
LANES = 128
NORM_EPS = 1e-6
GN_EPS = 64e-5
RW_CHUNK = 64
ATT_TK = 256

MIB = 1024 * 1024
LOG2_E = 1.4426950408889634
SIGN_BIT = 0x80000000
ATT_CHAIN_LANES = 256
ATT_SKEW = 2


def _nt(x, y):
    return lax.dot_general(x, y, (((1,), (1,)), ((), ())), preferred_element_type=F32)


def _tn(x, y):
    return lax.dot_general(x, y, (((0,), (0,)), ((), ())), preferred_element_type=F32)


def _mm(x, y):
    return jnp.dot(x, y, preferred_element_type=F32)


def _neg_abs(x):
    return lax.bitcast_convert_type(lax.bitcast_convert_type(x, jnp.uint32) | jnp.uint32(SIGN_BIT), F32)


def _split2(x):
    hi = x.astype(BF16)
    lo = (x - hi.astype(F32)).astype(BF16)
    return hi, lo


def _head_sum2(x, member):
    hi, lo = _split2(x)
    shi, slo = _split2(_mm(hi, member) + _mm(lo, member))
    return _nt(shi, member) + _nt(slo, member)


def _rms(x, g):
    ms = jnp.mean(x * x, axis=-1, keepdims=True)
    return x * lax.rsqrt(ms + NORM_EPS) * g


VMEM_LIMIT_MIB = {"ffn": 48, "mix": 48, "sb_attention": 48, "sb_attention_cached": 48, "rwkv_prep": 48,
                  "rwkv_scan": 32, "merge_out": 56}


def _call_opts(name, *sem):
    return dict(name=name, compiler_params=pltpu.CompilerParams(
        dimension_semantics=sem, vmem_limit_bytes=VMEM_LIMIT_MIB[name] * MIB))


def _ffn_body(x_ref, g_ref, wg_ref, wu_ref, wd_ref, o_ref, n_ref, acc_ref):
    f = pl.program_id(1)

    @pl.when(f == 0)
    def _():
        n_ref[...] = _rms(x_ref[...], g_ref[...]).astype(BF16)
        acc_ref[...] = jnp.zeros_like(acc_ref)

    n = n_ref[...]
    g = _mm(n, wg_ref[...])
    u = _mm(n, wu_ref[...])
    a = (g * jax.nn.sigmoid(g) * u).astype(BF16)
    acc_ref[...] += _mm(a, wd_ref[...])

    @pl.when(f == pl.num_programs(1) - 1)
    def _():
        o_ref[...] = x_ref[...] + 0.5 * acc_ref[...]


def _ffn(x, g, wg, wu, wd, *, tm=512, tf=512):
    n, d = x.shape
    ff = wg.shape[1]
    tm, tf = min(tm, n), min(tf, ff)
    return pl.pallas_call(
        _ffn_body,
        out_shape=jax.ShapeDtypeStruct((n, d), F32),
        grid=(n // tm, ff // tf),
        in_specs=[
            pl.BlockSpec((tm, d), lambda i, f: (i, 0)),
            pl.BlockSpec((1, d), lambda i, f: (0, 0)),
            pl.BlockSpec((d, tf), lambda i, f: (0, f)),
            pl.BlockSpec((d, tf), lambda i, f: (0, f)),
            pl.BlockSpec((tf, d), lambda i, f: (f, 0)),
        ],
        out_specs=pl.BlockSpec((tm, d), lambda i, f: (i, 0)),
        scratch_shapes=[pltpu.VMEM((tm, d), BF16), pltpu.VMEM((tm, d), F32)],
        **_call_opts("ffn", "parallel", "arbitrary"),
    )(x, g, wg, wu, wd)


def _mix_body(h_ref, g_ref, w_ref, hg_ref, member_ref, o_ref, n_ref, *, n_norm_cols):
    j = pl.program_id(1)

    @pl.when(j == 0)
    def _():
        n_ref[...] = _rms(h_ref[...], g_ref[...]).astype(BF16)

    p = _mm(n_ref[...], w_ref[...])
    tn = p.shape[1]

    n_norm_tiles = -(-n_norm_cols // tn)
    for jj in range(n_norm_tiles):
        wn = min(tn, n_norm_cols - jj * tn)

        @pl.when(j == jj)
        def _():
            pn = p[:, :wn]
            ms = _head_sum2(pn * pn, member_ref[0:wn, :]) * (1.0 / HEAD_DIM)
            o_ref[:, :wn] = pn * lax.rsqrt(ms + NORM_EPS) * hg_ref[:, :wn]
            if wn < tn:
                o_ref[:, wn:] = p[:, wn:]

    @pl.when(j >= n_norm_tiles)
    def _():
        o_ref[...] = p


def _mix(h, g, w, hgain, member, *, n_norm_cols, tm=1024, tn=512):
    n, d = h.shape
    cols = w.shape[1]
    tm = min(tm, n)
    member = member[:tn]
    return pl.pallas_call(
        functools.partial(_mix_body, n_norm_cols=n_norm_cols),
        out_shape=jax.ShapeDtypeStruct((n, cols), F32),
        grid=(n // tm, cols // tn),
        in_specs=[
            pl.BlockSpec((tm, d), lambda i, j: (i, 0)),
            pl.BlockSpec((1, d), lambda i, j: (0, 0)),
            pl.BlockSpec((d, tn), lambda i, j: (0, j)),
            pl.BlockSpec((1, tn), lambda i, j: (0, j)),
            pl.BlockSpec((tn, LANES), lambda i, j: (0, 0)),
        ],
        out_specs=pl.BlockSpec((tm, tn), lambda i, j: (i, j)),
        scratch_shapes=[pltpu.VMEM((tm, d), BF16)],
        **_call_opts("mix", "parallel", "arbitrary"),
    )(h, g, w, hgain, member)


def _sb_items(items, ut, tk, run, pv):
    zs, parts = {}, {}

    def scores(n):
        _, q16, kblk, _, _ = items[n]
        zs[n] = _nt(kblk, q16)

    def keep_sums(n):
        mask = items[n][4]
        z = zs.pop(n)
        lp = jnp.log(1.0 + jnp.exp2(_neg_abs(z))) * LOG2_E
        log_beta = jnp.minimum(z, 0.0) - lp
        log_keep = log_beta - z
        if mask is not None:
            log_keep = jnp.where(mask, log_keep, 0.0)
        parts[n] = (log_beta, _mm(ut, log_keep.astype(BF16)))

    def weigh(n):
        ci, _, _, vt_rows, mask = items[n]
        log_beta, ext = parts.pop(n)
        w = jnp.exp2(log_beta + (ext[:tk] + run[ci][0:1]))
        if mask is not None:
            w = jnp.where(mask, w, 0.0)
        d = _mm(vt_rows, w.astype(BF16))
        pv[ci] = d if ci not in pv else pv[ci] + d
        run[ci] = run[ci] + ext[tk:tk + 8]

    for step in range(len(items) + 2 * ATT_SKEW):
        if step < len(items):
            scores(step)
        if 0 <= step - ATT_SKEW < len(items):
            keep_sums(step - ATT_SKEW)
        if 0 <= step - 2 * ATT_SKEW < len(items):
            weigh(step - 2 * ATT_SKEW)


def _sb_rows(items, ue, tk, run, pv):
    zs, parts = {}, {}

    def scores(n):
        _, qh, kth, _, _ = items[n]
        zs[n] = jnp.concatenate([_mm(q, kt) for q, kt in zip(qh, kth)], axis=0)

    def keep_sums(n):
        mask = items[n][4]
        z = zs.pop(n)
        lp = jnp.log(1.0 + jnp.exp2(_neg_abs(z))) * LOG2_E
        log_beta = jnp.minimum(z, 0.0) - lp
        log_keep = log_beta - z
        if mask is not None:
            log_keep = jnp.where(mask, log_keep, 0.0)
        parts[n] = (log_beta, _mm(log_keep.astype(BF16), ue))

    def weigh(n):
        ci, qh, _, vth, mask = items[n]
        log_beta, ext = parts.pop(n)
        tail = jnp.concatenate([ext[:, c0:c0 + LANES] + run[ci] for c0 in range(0, tk, LANES)], axis=1)
        w = jnp.exp2(log_beta + tail)
        if mask is not None:
            w = jnp.where(mask, w, 0.0)
        w = w.astype(BF16)
        tq = qh[0].shape[0]
        for h, vt in enumerate(vth):
            d = _nt(w[h * tq:(h + 1) * tq], vt)
            pv[ci, h] = d if (ci, h) not in pv else pv[ci, h] + d
        run[ci] = run[ci] + ext[:, tk:]

    for step in range(len(items) + 2 * ATT_SKEW):
        if step < len(items):
            scores(step)
        if 0 <= step - ATT_SKEW < len(items):
            keep_sums(step - ATT_SKEW)
        if 0 <= step - 2 * ATT_SKEW < len(items):
            weigh(step - 2 * ATT_SKEW)


def _attn_cached_body(q_ref, k_ref, v_ref, ckt_ref, cvt_ref, ue_ref, o_ref, qs, carry, acc, *,
                      tq, tk, n_heads, chunk_blocks, per_chain):
    j = pl.program_id(1)
    n_chains = n_heads // per_chain
    ue = ue_ref[...]
    heads_of = lambda c: range(c * per_chain, (c + 1) * per_chain)

    def run_items(items):
        run = {c: carry[c] for c in range(n_chains)}
        pv = {}
        _sb_rows(items, ue, tk, run, pv)
        for c in range(n_chains):
            carry[c] = run[c]
            for i, h in enumerate(heads_of(c)):
                acc[h] += pv[c, i]

    @pl.when(j == 0)
    def _():
        carry[...] = jnp.zeros_like(carry)
        acc[...] = jnp.zeros_like(acc)
        for h in range(n_heads):
            qs[h] = (q_ref[0, :, h * HEAD_DIM:(h + 1) * HEAD_DIM] * (HEAD_DIM ** -0.5 * LOG2_E)).astype(BF16)
        row = lax.broadcasted_iota(jnp.int32, (per_chain * tq, tk), 0) % tq
        col = lax.broadcasted_iota(jnp.int32, (per_chain * tq, tk), 1)
        pad = jnp.zeros((HEAD_DIM, tk - tq), F32)
        items = []
        for c in range(n_chains):
            def new_t(ref):
                return [jnp.concatenate([ref[0, :, h * HEAD_DIM:(h + 1) * HEAD_DIM].T, pad], axis=1).astype(BF16)
                        for h in heads_of(c)]
            items.append((c, [qs[h] for h in heads_of(c)], new_t(k_ref), new_t(v_ref), col < row))
        run_items(items)

    items = []
    for m in reversed(range(chunk_blocks)):
        keys = slice(m * tk, (m + 1) * tk)
        for c in range(n_chains):
            items.append((c, [qs[h] for h in heads_of(c)],
                          [ckt_ref[0, h, :, keys].astype(BF16) for h in heads_of(c)],
                          [cvt_ref[0, h, :, keys].astype(BF16) for h in heads_of(c)], None))
    run_items(items)

    @pl.when(j == pl.num_programs(1) - 1)
    def _():
        o_ref[0] = jnp.concatenate([acc[h] for h in range(n_heads)], axis=1)


def _attention_cached(p3, cache_k, cache_v, ue, *, width, chunk_keys=1024, per_chain=4):
    b, t, _ = p3.shape
    tk = ATT_TK
    _, p_len, n_heads, _ = cache_k.shape
    chunk_keys = min(chunk_keys, p_len)
    n_chunks = p_len // chunk_keys
    ckt = jnp.transpose(cache_k, (0, 2, 3, 1))
    cvt = jnp.transpose(cache_v, (0, 2, 3, 1))
    cache_spec = pl.BlockSpec((1, n_heads, HEAD_DIM, chunk_keys), lambda bi, j: (bi, 0, 0, n_chunks - 1 - j))
    return pl.pallas_call(
        functools.partial(_attn_cached_body, tq=t, tk=tk, n_heads=n_heads, chunk_blocks=chunk_keys // tk,
                          per_chain=per_chain),
        out_shape=jax.ShapeDtypeStruct((b, t, width), F32),
        grid=(b, n_chunks),
        in_specs=[
            pl.BlockSpec((1, t, width), lambda bi, j: (bi, 0, 0)),
            pl.BlockSpec((1, t, width), lambda bi, j: (bi, 0, 1)),
            pl.BlockSpec((1, t, width), lambda bi, j: (bi, 0, 2)),
            cache_spec, cache_spec,
            pl.BlockSpec(ue.shape, lambda bi, j: (0, 0)),
        ],
        out_specs=pl.BlockSpec((1, t, width), lambda bi, j: (bi, 0, 0)),
        scratch_shapes=[
            pltpu.VMEM((n_heads, t, HEAD_DIM), BF16),
            pltpu.VMEM((n_heads // per_chain, per_chain * t, LANES), F32),
            pltpu.VMEM((n_heads, t, HEAD_DIM), F32),
        ],
        **_call_opts("sb_attention_cached", "parallel", "arbitrary"),
    )(p3, p3, p3, ckt, cvt, ue)


def _attn_body(q_ref, k_ref, v_ref, ut_ref, o_ref, kb, vt, qs, carry, acc, *, tq, tk, n_blocks, new_unroll, chains):
    i = pl.program_id(2)

    @pl.when(i == 0)
    def _():
        def fill(blk, _):
            rows = pl.ds(pl.multiple_of(blk * tk, tk), tk)
            kb[rows, :] = k_ref[0, rows, :].astype(BF16)
            vt[blk] = v_ref[0, rows, :].T.astype(BF16)
            return 0

        lax.fori_loop(0, n_blocks, fill, 0)

    q = q_ref[0] * (HEAD_DIM ** -0.5 * LOG2_E)
    n_heads = q.shape[1] // HEAD_DIM
    head_of_lane = lax.broadcasted_iota(jnp.int32, q.shape, 1) // HEAD_DIM
    zero = jnp.zeros_like(q)
    for h in range(n_heads):
        qs[h * tq:(h + 1) * tq, :] = jnp.where(head_of_lane == h, q, zero).astype(BF16)
    carry[...] = jnp.zeros_like(carry)
    acc[...] = jnp.zeros_like(acc)

    def span(blocks):
        q16 = [qs[c0:c1, :] for c0, c1, _, _ in chains]
        cw = chains[0][1] - chains[0][0]
        row = lax.broadcasted_iota(jnp.int32, (tk, cw), 0)
        lane = lax.broadcasted_iota(jnp.int32, (tk, cw), 1)
        mask = lambda off, c0: None if off is None else (row + off) < (lane + c0) % tq
        items = [(ci, q16[ci], kblk, vtblk[r0:r1], mask(off, c0))
                 for kblk, vtblk, off in blocks for ci, (c0, _, r0, r1) in enumerate(chains)]
        run = {ci: carry[:, c0:c1] for ci, (c0, c1, _, _) in enumerate(chains)}
        pv = {}
        _sb_items(items, ut_ref[...], tk, run, pv)
        for ci, (c0, c1, r0, r1) in enumerate(chains):
            carry[:, c0:c1] = run[ci]
            acc[r0:r1, c0:c1] += pv[ci]

    q0 = i * tq
    n_diag = tq // tk
    diag = []
    for m in reversed(range(n_diag)):
        blk = q0 // tk + m
        diag.append((kb[pl.ds(pl.multiple_of(blk * tk, tk), tk), :], vt[blk], m * tk))
    span(diag)

    def new_blocks(last, count):
        blocks = []
        for m in range(count):
            blk = last - m
            blocks.append((kb[pl.ds(pl.multiple_of(blk * tk, tk), tk), :], vt[blk], None))
        span(blocks)

    last = q0 // tk - 1
    per_trip = new_unroll
    while per_trip >= n_diag:
        def step(it, _, last=last, per_trip=per_trip):
            new_blocks(last - it * per_trip, per_trip)
            return 0

        trips = (last + 1) // per_trip
        lax.fori_loop(0, trips, step, 0)
        last = last - trips * per_trip
        per_trip //= 2

    out_t = acc[...].T
    out = out_t[:tq]
    for h in range(1, n_heads):
        out = jnp.where(head_of_lane == h, out_t[h * tq:(h + 1) * tq], out)
    o_ref[0] = out


def _attention(p3, ut, *, n_pairs, q_blk0, k_blk0, v_blk0, tq):
    b, t, _ = p3.shape
    tk = ATT_TK
    tq = min(tq, t)
    assert tq % tk == 0 and t % tq == 0, (t, tq, tk)
    cw = min(tq, ATT_CHAIN_LANES)
    chains = tuple((h * tq + c0, h * tq + c0 + cw, h * HEAD_DIM, (h + 1) * HEAD_DIM)
                   for h in range(2) for c0 in range(0, tq, cw))
    seq = lambda blk0: pl.BlockSpec((1, t, LANES), lambda bi, p, i: (bi, 0, blk0 + p))
    return pl.pallas_call(
        functools.partial(_attn_body, tq=tq, tk=tk, n_blocks=t // tk, new_unroll=4 * (tq // tk), chains=chains),
        out_shape=jax.ShapeDtypeStruct((b, t, n_pairs * LANES), F32),
        grid=(b, n_pairs, t // tq),
        in_specs=[
            pl.BlockSpec((1, tq, LANES), lambda bi, p, i: (bi, i, q_blk0 + p)),
            seq(k_blk0), seq(v_blk0),
            pl.BlockSpec(ut.shape, lambda bi, p, i: (0, 0)),
        ],
        out_specs=pl.BlockSpec((1, tq, LANES), lambda bi, p, i: (bi, i, p)),
        scratch_shapes=[
            pltpu.VMEM((t, LANES), BF16),
            pltpu.VMEM((t // tk, LANES, tk), BF16),
            pltpu.VMEM((2 * tq, LANES), BF16),
            pltpu.VMEM((8, 2 * tq), F32),
            pltpu.VMEM((LANES, 2 * tq), F32),
        ],
        **_call_opts("sb_attention", "parallel", "parallel", "arbitrary"),
    )(p3, p3, p3, ut)


def _prep_body(prkv_ref, plora_ref, s_rkv_ref, s_lora_ref, mu_rkv_ref, mu_lora_ref, w0_ref, a0_ref, kk_ref, ka_ref,
               ww2_ref, wa2_ref, wg2_ref, member_ref,
               r_o, wl_o, k_o, v_o, av_o, bv_o, g_o, c_rkv, c_lora, *, tc, width):
    t = pl.program_id(1)

    @pl.when(t == 0)
    def _():
        c_rkv[0:1, :] = s_rkv_ref[0]
        c_lora[0:1, :] = s_lora_ref[0]

    def token_mix(p, prev, mu):
        row = lax.broadcasted_iota(jnp.int32, p.shape, 0)
        shifted = jnp.where(row == 0, prev, pltpu.roll(p, 1, 0))
        return p + (shifted - p) * mu

    def rkv_seg(s):
        cs = slice(s * width, (s + 1) * width)
        p = prkv_ref[0, :, cs]
        x = token_mix(p, c_rkv[0:1, cs], mu_rkv_ref[:, cs])
        c_rkv[0:1, cs] = p[tc - 1:tc, :]
        return x

    pl_ = plora_ref[0]
    xl = token_mix(pl_, c_lora[0:1, :], mu_lora_ref[...])
    c_lora[0:1, :] = pl_[tc - 1:tc, :]

    r_o[0] = rkv_seg(0)
    v_o[0] = rkv_seg(2)
    xk = rkv_seg(1)

    dec = w0_ref[...] + _mm(jnp.tanh(xl).astype(BF16), ww2_ref[...])
    nd = -dec
    softplus = jnp.maximum(nd, 0.0) + jnp.log(1.0 + jnp.exp(-jnp.abs(nd)))
    w_log = -softplus - 0.5
    wl_o[0] = -jnp.exp(w_log)
    a = jax.nn.sigmoid(a0_ref[...] + _mm(xl.astype(BF16), wa2_ref[...]))
    g_o[0] = _mm(jax.nn.sigmoid(xl).astype(BF16), wg2_ref[...])
    kk = xk * kk_ref[...]
    k_o[0] = xk * (1.0 + (a - 1.0) * ka_ref[...])
    norm = jnp.sqrt(_head_sum2(kk * kk, member_ref[...]))
    kk = kk / jnp.maximum(norm, 1e-12)
    av_o[0] = -kk
    bv_o[0] = kk * a


def _rwkv_prep(p3, s_rkv, s_lora, mu_rkv, mu_lora, w0, a0, k_k, k_a, ww2, wa2, wg2, member, *, rkv_blk, lora_blk, tc):
    b, t, _ = p3.shape
    width = w0.shape[1]
    lw = mu_lora.shape[1]
    tc = min(tc, t)
    const = lambda shape: pl.BlockSpec(shape, lambda bi, ti: (0,) * len(shape))
    out_spec = pl.BlockSpec((1, tc, width), lambda bi, ti: (bi, ti, 0))
    return pl.pallas_call(
        functools.partial(_prep_body, tc=tc, width=width),
        out_shape=[jax.ShapeDtypeStruct((b, t, width), F32)] * 7,
        grid=(b, t // tc),
        in_specs=[
            pl.BlockSpec((1, tc, 3 * width), lambda bi, ti: (bi, ti, rkv_blk)),
            pl.BlockSpec((1, tc, lw), lambda bi, ti: (bi, ti, lora_blk)),
            pl.BlockSpec((1, 1, 3 * width), lambda bi, ti: (bi, 0, 0)),
            pl.BlockSpec((1, 1, lw), lambda bi, ti: (bi, 0, 0)),
            const((1, 3 * width)), const((1, lw)),
            const((1, width)), const((1, width)), const((1, width)), const((1, width)),
            const((lw, width)), const((lw, width)), const((lw, width)),
            const((width, LANES)),
        ],
        out_specs=[out_spec] * 7,
        scratch_shapes=[pltpu.VMEM((8, 3 * width), F32), pltpu.VMEM((8, lw), F32)],
        **_call_opts("rwkv_prep", "parallel", "arbitrary"),
    )(p3, p3, s_rkv, s_lora, mu_rkv, mu_lora, w0, a0, k_k, k_a, ww2, wa2, wg2, member)


def _scan_body(r_ref, wl_ref, k_ref, v_ref, a_ref, b_ref, s0_ref, tri_ref, y_ref, sout_ref, s_scr, *,
               n_chunks, n_pairs):
    c_len = RW_CHUNK
    t = pl.program_id(2)

    @pl.when(t == 0)
    def _():
        s_scr[...] = s0_ref[0]

    row = lax.broadcasted_iota(jnp.int32, (c_len, c_len), 0)
    col = lax.broadcasted_iota(jnp.int32, (c_len, c_len), 1)
    strict = row > col
    incl = row >= col
    lane = lax.broadcasted_iota(jnp.int32, (c_len, LANES), 1)
    first = lane < HEAD_DIM
    brow = lax.broadcasted_iota(jnp.int32, (LANES, LANES), 0)
    bcol = lax.broadcasted_iota(jnp.int32, (LANES, LANES), 1)
    same_head = (brow // HEAD_DIM) == (bcol // HEAD_DIM)
    eye = brow == bcol
    tri = tri_ref[...]
    bf = lambda x: x.astype(BF16)

    cps = [(c, p) for c in range(n_chunks) for p in range(n_pairs)]
    sls = {(c, p): (slice(c * c_len, (c + 1) * c_len), slice(p * LANES, (p + 1) * LANES)) for c, p in cps}
    tiles = {}
    for c in range(n_chunks):
        sl = slice(c * c_len, (c + 1) * c_len)
        wl = wl_ref[0, sl, :]
        hi = wl.astype(BF16)
        rem = wl - hi.astype(F32)
        mid = rem.astype(BF16)
        lo = (rem - mid.astype(F32)).astype(BF16)
        cum = _mm(tri, hi) + _mm(tri, mid) + _mm(tri, lo)
        tot = cum[c_len - 1:c_len, :]
        e_neg = jnp.exp(-cum)
        e_end = jnp.exp(tot - cum)
        av, bv, kv, vv = a_ref[0, sl, :], b_ref[0, sl, :], k_ref[0, sl, :], v_ref[0, sl, :]
        tiles[c] = dict(at=av * jnp.exp(cum - wl), rt=r_ref[0, sl, :] * jnp.exp(cum), bt=bf(bv * e_neg),
                        kt=bf(kv * e_neg), bh=bv * e_end, kh=kv * e_end, vv=vv, etot=jnp.exp(tot))
    tile = lambda name, c, p: tiles[c][name][:, p * LANES:(p + 1) * LANES]

    chains = [(c, p, h) for c, p in cps for h in range(2)]
    mbk = {}
    for c, p, h in chains:
        sel = first if h == 0 else jnp.logical_not(first)
        at, rt = tile("at", c, p), tile("rt", c, p)
        zero = jnp.zeros_like(at)
        ar = jnp.concatenate([jnp.where(sel, at, zero), jnp.where(sel, rt, zero)], axis=0).astype(BF16)
        mbk[c, p, h] = (_nt(ar, tile("bt", c, p)), _nt(ar, tile("kt", c, p)))
    m_ab, p_rb, m_ak, p_rk, tm = {}, {}, {}, {}, {}
    for ch in chains:
        mb, mk = mbk[ch]
        m_ab[ch] = jnp.where(strict, mb[:c_len], 0.0)
        p_rb[ch] = bf(jnp.where(incl, mb[c_len:], 0.0))
        m_ak[ch] = bf(jnp.where(strict, mk[:c_len], 0.0))
        p_rk[ch] = bf(jnp.where(incl, mk[c_len:], 0.0))
        tm[ch] = jnp.where(row == col, 1.0, 0.0) + jnp.where((row // 2) == (col // 2), m_ab[ch], 0.0)
    s = 2
    while s < c_len:
        off = jnp.logical_and((row // (2 * s)) == (col // (2 * s)), (row // s) != (col // s))
        half = {ch: bf(_mm(bf(tm[ch]), bf(jnp.where(off, m_ab[ch], 0.0)))) for ch in chains}
        tm = {ch: tm[ch] + _mm(half[ch], bf(tm[ch])) for ch in chains}
        s *= 2
    t16 = {ch: bf(tm[ch]) for ch in chains}
    mv = {(c, p, h): _mm(m_ak[c, p, h], bf(tile("vv", c, p))) for c, p, h in chains}
    w1 = {(c, p, h): _mm(t16[c, p, h], bf(tile("at", c, p))) for c, p, h in chains}
    w2 = {ch: _mm(t16[ch], bf(mv[ch])) for ch in chains}
    qc = {(c, p, h): tile("rt", c, p) + _mm(p_rb[c, p, h], bf(w1[c, p, h])) for c, p, h in chains}
    y1 = {(c, p, h): _mm(p_rb[c, p, h], bf(w2[c, p, h])) + _mm(p_rk[c, p, h], bf(tile("vv", c, p)))
          for c, p, h in chains}
    both = lambda d, c, p: jnp.where(first, d[c, p, 0], d[c, p, 1])
    ac_t, dc_t, qcs, y1s = {}, {}, {}, {}
    for c, p in cps:
        bh, kh, vv = tile("bh", c, p), tile("kh", c, p), tile("vv", c, p)
        w1p, w2p = both(w1, c, p), both(w2, c, p)
        a_full = jnp.where(same_head, _tn(bf(w1p), bf(bh)), 0.0) + jnp.where(eye, tile("etot", c, p), 0.0)
        ac_t[c, p] = _split2(a_full)
        dc_t[c, p] = jnp.where(same_head, _tn(bf(jnp.concatenate([w2p, vv], axis=0)),
                                              bf(jnp.concatenate([bh, kh], axis=0))), 0.0)
        qcs[c, p], y1s[c, p] = bf(both(qc, c, p)), both(y1, c, p)
    for p in range(n_pairs):
        state = s_scr[p]
        for c in range(n_chunks):
            rs, ls = sls[c, p]
            s_hi, s_lo = _split2(state)
            a_hi, a_lo = ac_t[c, p]
            y_ref[0, rs, ls] = _nt(qcs[c, p], s_hi) + y1s[c, p]
            state = _mm(s_hi, a_hi) + _mm(s_hi, a_lo) + _mm(s_lo, a_hi) + dc_t[c, p]
        s_scr[p] = state

    @pl.when(t == pl.num_programs(2) - 1)
    def _():
        sout_ref[0] = s_scr[...]


def _rwkv_scan(r, wl, k, v, av, bv, s0_bd, tri, *, tc, pairs_per_step):
    b, t, width = r.shape
    n_pairs = width // LANES
    tc = min(tc, t)
    npb = pairs_per_step
    seq = pl.BlockSpec((1, tc, npb * LANES), lambda bi, p, ti: (bi, ti, p))
    state = pl.BlockSpec((1, npb, LANES, LANES), lambda bi, p, ti: (bi, p, 0, 0))
    return pl.pallas_call(
        functools.partial(_scan_body, n_chunks=tc // RW_CHUNK, n_pairs=npb),
        out_shape=[jax.ShapeDtypeStruct((b, t, width), F32), jax.ShapeDtypeStruct(s0_bd.shape, F32)],
        grid=(b, n_pairs // npb, t // tc),
        in_specs=[seq] * 6 + [state, pl.BlockSpec(tri.shape, lambda bi, p, ti: (0, 0))],
        out_specs=[seq, state],
        scratch_shapes=[pltpu.VMEM((npb, LANES, LANES), F32)],
        **_call_opts("rwkv_scan", "parallel", "parallel", "arbitrary"),
    )(r, wl, k, v, av, bv, s0_bd, tri)


def _out_body(h_ref, osb_ref, y_ref, r_ref, k_ref, v_ref, g_ref, gs_ref, gr_ref, lnw_ref, lnb_ref, rk_ref, member_ref,
              wso_ref, wro_ref, wout_ref, o_ref, m_ref):
    j = pl.program_id(1)

    @pl.when(j == 0)
    def _():
        member = member_ref[...]
        y = y_ref[...]
        mu = _head_sum2(y, member) * (1.0 / HEAD_DIM)
        d = y - mu
        var = _head_sum2(d * d, member) * (1.0 / HEAD_DIM)
        yn = d * lax.rsqrt(var + GN_EPS) * lnw_ref[...] + lnb_ref[...]
        bonus = _head_sum2(r_ref[...] * k_ref[...] * rk_ref[...], member) * v_ref[...]
        yy = ((yn + bonus) * g_ref[...]).astype(BF16)
        o_sb = _mm(osb_ref[...].astype(BF16), wso_ref[...])
        o_rw = _mm(yy, wro_ref[...])
        merged = jax.nn.sigmoid(gs_ref[...]) * o_sb + jax.nn.sigmoid(gr_ref[...]) * o_rw
        m_ref[...] = merged.astype(BF16)

    o_ref[...] = h_ref[...] + _mm(m_ref[...], wout_ref[...])


def _merge_out(h, o_sb, y, r, k, v, g, p2, lnw, lnb, rk, member, wso, wro, wout, *, gs_blk, gr_blk, tm=128, tn=2048):
    n, d = h.shape
    width = o_sb.shape[1]
    tm, tn = min(tm, n), min(tn, d)
    tok = pl.BlockSpec((tm, width), lambda i, j: (i, 0))
    const = lambda shape: pl.BlockSpec(shape, lambda i, j: (0,) * len(shape))
    return pl.pallas_call(
        _out_body,
        out_shape=jax.ShapeDtypeStruct((n, d), F32),
        grid=(n // tm, d // tn),
        in_specs=[
            pl.BlockSpec((tm, tn), lambda i, j: (i, j)),
            tok, tok, tok, tok, tok, tok,
            pl.BlockSpec((tm, d), lambda i, j: (i, gs_blk)),
            pl.BlockSpec((tm, d), lambda i, j: (i, gr_blk)),
            const((1, width)), const((1, width)), const((1, width)),
            const((width, LANES)),
            const((width, d)), const((width, d)),
            pl.BlockSpec((d, tn), lambda i, j: (0, j)),
        ],
        out_specs=pl.BlockSpec((tm, tn), lambda i, j: (i, j)),
        scratch_shapes=[pltpu.VMEM((tm, d), BF16)],
        **_call_opts("merge_out", "parallel", "arbitrary"),
    )(h, o_sb, y, r, k, v, g, p2, p2, lnw, lnb, rk, member, wso, wro, wout)


def _layer(x, past_k, past_v, wkv0, shift0, w, *, tq, tc, scan_pairs):
    b, t, d = x.shape
    n = b * t
    width = w["w0"].shape[1]
    h1 = _ffn(x.reshape(n, d), w["ffn1_norm"], w["ffn1_wg"], w["ffn1_wu"], w["ffn1_wd"])
    p2 = _mix(h1, w["mix_norm"], w["w_in"], w["head_gain"], w["member_mix"], n_norm_cols=2 * width)
    p3 = p2.reshape(b, t, -1)
    n_pairs = width // LANES
    if past_k is None:
        o_sb = _attention(p3, w["u2"], n_pairs=n_pairs, q_blk0=0, k_blk0=n_pairs, v_blk0=2 * n_pairs, tq=tq)
    else:
        o_sb = _attention_cached(p3, past_k, past_v, w["ue"], width=width)

    lora_w = w["mu_lora"].shape[1]
    lora_blk = (6 * width + 2 * d) // lora_w
    lora_cols = w["lora_cols"]
    s_rkv = shift0[:, :, :3 * width]
    s_lora = jnp.pad(shift0[:, :, 3 * width:], ((0, 0), (0, 0), (0, lora_w - lora_cols)))
    r, wl, k, v, av, bv, g = _rwkv_prep(
        p3, s_rkv, s_lora, w["mu_rkv"], w["mu_lora"], w["w0"], w["a0"], w["k_k"], w["k_a"],
        w["ww2"], w["wa2"], w["wg2"], w["member"], rkv_blk=1, lora_blk=lora_blk, tc=tc)

    s0 = wkv0.reshape(b, n_pairs, 2, HEAD_DIM, HEAD_DIM)
    z = jnp.zeros_like(s0[:, :, 0])
    s0_bd = jnp.concatenate([jnp.concatenate([s0[:, :, 0], z], axis=-1),
                             jnp.concatenate([z, s0[:, :, 1]], axis=-1)], axis=-2)
    y, s_bd = _rwkv_scan(r, wl, k, v, av, bv, s0_bd, w["tri"], tc=tc, pairs_per_step=scan_pairs)
    wkv = jnp.stack([s_bd[:, :, :HEAD_DIM, :HEAD_DIM], s_bd[:, :, HEAD_DIM:, HEAD_DIM:]], axis=2)
    wkv = wkv.reshape(b, 2 * n_pairs, HEAD_DIM, HEAD_DIM)

    flat = lambda a: a.reshape(n, width)
    gs_blk = (6 * width) // d
    h2 = _merge_out(h1, flat(o_sb), flat(y), flat(r), flat(k), flat(v), flat(g), p2,
                    w["ln_w"], w["ln_b"], w["r_k"], w["member"], w["sb_wo"], w["rw_wo"], w["w_out"],
                    gs_blk=gs_blk, gr_blk=gs_blk + 1)
    out = _ffn(h2, w["ffn2_norm"], w["ffn2_wg"], w["ffn2_wu"], w["ffn2_wd"])

    heads = width // HEAD_DIM
    k_new = p3[:, :, width:2 * width].reshape(b, t, heads, HEAD_DIM)
    v_new = p3[:, :, 2 * width:3 * width].reshape(b, t, heads, HEAD_DIM)
    shift = jnp.concatenate([p3[:, t - 1:, 3 * width:6 * width],
                             p3[:, t - 1:, 6 * width + 2 * d:6 * width + 2 * d + lora_cols]], axis=-1)
    return out.reshape(b, t, d), k_new, v_new, wkv, shift


def _layer_weights(l, ffn1_norm, ffn1_w_gate, ffn1_w_up, ffn1_w_down, mix_norm, w_in, sb_q_norm, sb_k_norm, sb_w_o,
                   rwkv_mu, rwkv_w0, rwkv_w_w2, rwkv_a0, rwkv_w_a2, rwkv_w_g2, rwkv_k_k, rwkv_k_a, rwkv_r_k,
                   rwkv_ln_w, rwkv_ln_b, rwkv_w_o, w_out, ffn2_norm, ffn2_w_gate, ffn2_w_up, ffn2_w_down):
    d = w_in.shape[1]
    width = rwkv_w0.shape[1]
    heads = width // HEAD_DIM
    n_decay, n_iclr, n_gate = rwkv_w_w2.shape[1], rwkv_w_a2.shape[1], rwkv_w_g2.shape[1]
    lora_cols = n_decay + n_iclr + n_gate
    lora_w = -(-lora_cols // 512) * 512
    row = lambda a: a.reshape(1, -1).astype(F32)
    wi = w_in[l]
    w_in_p = jnp.concatenate([
        wi[:, :6 * width], wi[:, 6 * width + lora_cols:], wi[:, 6 * width:6 * width + lora_cols],
        jnp.zeros((d, lora_w - lora_cols), wi.dtype)], axis=1).astype(BF16)
    total = w_in_p.shape[1]
    head_gain = jnp.concatenate([jnp.tile(sb_q_norm[l], heads), jnp.tile(sb_k_norm[l], heads),
                                 jnp.ones((total - 2 * width,), F32)]).reshape(1, total)
    mu = rwkv_mu[l]

    def lora_pad(wm, r0):
        return jnp.zeros((lora_w, width), F32).at[r0:r0 + wm.shape[0]].set(wm).astype(BF16)

    hid = jnp.arange(width) // HEAD_DIM
    member_mix = ((jnp.arange(1536) // HEAD_DIM)[:, None] == jnp.arange(LANES)[None, :]).astype(BF16)
    member = (hid[:, None] == jnp.arange(LANES)[None, :]).astype(BF16)
    tk = ATT_TK
    ki = jnp.arange(tk)
    u2 = jnp.concatenate([(ki[None, :] > ki[:, None]).astype(BF16), jnp.ones((16, tk), BF16)], axis=0)
    ue = jnp.concatenate([(ki[:, None] > ki[None, :]).astype(BF16), jnp.ones((tk, LANES), BF16)], axis=1)
    ci = jnp.arange(RW_CHUNK)
    tri = (ci[:, None] >= ci[None, :]).astype(BF16)
    return {
        "ffn1_norm": row(ffn1_norm[l]), "ffn1_wg": ffn1_w_gate[l].astype(BF16), "ffn1_wu": ffn1_w_up[l].astype(BF16),
        "ffn1_wd": ffn1_w_down[l].astype(BF16),
        "ffn2_norm": row(ffn2_norm[l]), "ffn2_wg": ffn2_w_gate[l].astype(BF16), "ffn2_wu": ffn2_w_up[l].astype(BF16),
        "ffn2_wd": ffn2_w_down[l].astype(BF16),
        "mix_norm": row(mix_norm[l]), "w_in": w_in_p, "head_gain": head_gain, "member_mix": member_mix, "member": member, "u2": u2, "ue": ue, "tri": tri,
        "mu_rkv": row(mu[:3 * width]), "mu_lora": row(jnp.pad(mu[3 * width:], (0, lora_w - lora_cols))),
        "lora_cols": lora_cols,
        "w0": row(rwkv_w0[l]), "a0": row(rwkv_a0[l]), "k_k": row(rwkv_k_k[l]), "k_a": row(rwkv_k_a[l]),
        "ww2": lora_pad(rwkv_w_w2[l], 0), "wa2": lora_pad(rwkv_w_a2[l], n_decay),
        "wg2": lora_pad(rwkv_w_g2[l], n_decay + n_iclr),
        "ln_w": row(rwkv_ln_w[l]), "ln_b": row(rwkv_ln_b[l]), "r_k": row(rwkv_r_k[l]),
        "sb_wo": sb_w_o[l].astype(BF16), "rw_wo": rwkv_w_o[l].astype(BF16), "w_out": w_out[l].astype(BF16),
    }


def kernel(x_prompt, x_sample, cache_sb_k, cache_sb_v, state_rwkv_wkv, state_rwkv_shift, ffn1_norm, ffn1_w_gate, ffn1_w_up, ffn1_w_down, mix_norm, w_in, sb_q_norm, sb_k_norm, sb_w_o, rwkv_mu, rwkv_w0, rwkv_w_w2, rwkv_a0, rwkv_w_a2, rwkv_w_g2, rwkv_k_k, rwkv_k_a, rwkv_r_k, rwkv_ln_w, rwkv_ln_b, rwkv_w_o, w_out, ffn2_norm, ffn2_w_gate, ffn2_w_up, ffn2_w_down):
    depth = w_in.shape[0]
    yp, ys = x_prompt, x_sample
    bp = x_prompt.shape[0]
    width = rwkv_w0.shape[1]
    heads = width // HEAD_DIM
    rw_cols = state_rwkv_shift.shape[-1]
    outs = [[] for _ in range(8)]
    for l in range(depth):
        w = _layer_weights(l, ffn1_norm, ffn1_w_gate, ffn1_w_up, ffn1_w_down, mix_norm, w_in, sb_q_norm, sb_k_norm,
                           sb_w_o, rwkv_mu, rwkv_w0, rwkv_w_w2, rwkv_a0, rwkv_w_a2, rwkv_w_g2, rwkv_k_k, rwkv_k_a,
                           rwkv_r_k, rwkv_ln_w, rwkv_ln_b, rwkv_w_o, w_out, ffn2_norm, ffn2_w_gate, ffn2_w_up,
                           ffn2_w_down)
        wkv_zero = jnp.zeros((bp, heads, HEAD_DIM, HEAD_DIM), F32)
        shift_zero = jnp.zeros((bp, 1, rw_cols), F32)
        yp, kp, vp, wkvp, shp = _layer(yp, None, None, wkv_zero, shift_zero, w, tq=512, tc=256, scan_pairs=4)
        ys, kn, vn, wkvn, shn = _layer(ys, cache_sb_k[l], cache_sb_v[l], state_rwkv_wkv[l], state_rwkv_shift[l], w, tq=64, tc=64, scan_pairs=8)
        for lst, val in zip(outs, (kp, vp, wkvp, shp, kn, vn, wkvn, shn)):
            lst.append(val)
    return (yp, ys) + tuple(jnp.stack(o) for o in outs)
```

```python
import functools

import jax
import jax.numpy as jnp
from jax import lax
from jax.experimental import pallas as pl
from jax.experimental.pallas import tpu as pltpu

F32 = jnp.float32
BF16 = jnp.bfloat16

HEAD_DIM = 64
LANES = 128
NORM_EPS = 1e-6
GN_EPS = 64e-5
RW_CHUNK = 64
ATT_TK = 256

MIB = 1024 * 1024
LOG2_E = 1.4426950408889634
SIGN_BIT = 0x80000000
ATT_CHAIN_LANES = 256
ATT_SKEW = 2


def _nt(x, y):
    return lax.dot_general(x, y, (((1,), (1,)), ((), ())), preferred_element_type=F32)


def _tn(x, y):
    return lax.dot_general(x, y, (((0,), (0,)), ((), ())), preferred_element_type=F32)


def _mm(x, y):
    return jnp.dot(x, y, preferred_element_type=F32)


def _neg_abs(x):
    return lax.bitcast_convert_type(lax.bitcast_convert_type(x, jnp.uint32) | jnp.uint32(SIGN_BIT), F32)


def _split2(x):
    hi = x.astype(BF16)
    lo = (x - hi.astype(F32)).astype(BF16)
    return hi, lo


def _head_sum2(x, member):
    hi, lo = _split2(x)
    shi, slo = _split2(_mm(hi, member) + _mm(lo, member))
    return _nt(shi, member) + _nt(slo, member)


def _rms(x, g):
    ms = jnp.mean(x * x, axis=-1, keepdims=True)
    return x * lax.rsqrt(ms + NORM_EPS) * g


VMEM_LIMIT_MIB = {"ffn": 48, "mix": 48, "sb_attention": 48, "sb_attention_cached": 48, "rwkv_prep": 48,
                  "rwkv_scan": 32, "merge_out": 56}


def _call_opts(name, *sem):
    return dict(name=name, compiler_params=pltpu.CompilerParams(
        dimension_semantics=sem, vmem_limit_bytes=VMEM_LIMIT_MIB[name] * MIB))


def _ffn_body(x_ref, g_ref, wg_ref, wu_ref, wd_ref, o_ref, n_ref, acc_ref):
    f = pl.program_id(1)

    @pl.when(f == 0)
    def _():
        n_ref[...] = _rms(x_ref[...], g_ref[...]).astype(BF16)
        acc_ref[...] = jnp.zeros_like(acc_ref)

    n = n_ref[...]
    g = _mm(n, wg_ref[...])
    u = _mm(n, wu_ref[...])
    a = (g * jax.nn.sigmoid(g) * u).astype(BF16)
    acc_ref[...] += _mm(a, wd_ref[...])

    @pl.when(f == pl.num_programs(1) - 1)
    def _():
        o_ref[...] = x_ref[...] + 0.5 * acc_ref[...]


def _ffn(x, g, wg, wu, wd, *, tm=512, tf=512):
    n, d = x.shape
    ff = wg.shape[1]
    tm, tf = min(tm, n), min(tf, ff)
    return pl.pallas_call(
        _ffn_body,
        out_shape=jax.ShapeDtypeStruct((n, d), F32),
        grid=(n // tm, ff // tf),
        in_specs=[
            pl.BlockSpec((tm, d), lambda i, f: (i, 0)),
            pl.BlockSpec((1, d), lambda i, f: (0, 0)),
            pl.BlockSpec((d, tf), lambda i, f: (0, f)),
            pl.BlockSpec((d, tf), lambda i, f: (0, f)),
            pl.BlockSpec((tf, d), lambda i, f: (f, 0)),
        ],
        out_specs=pl.BlockSpec((tm, d), lambda i, f: (i, 0)),
        scratch_shapes=[pltpu.VMEM((tm, d), BF16), pltpu.VMEM((tm, d), F32)],
        **_call_opts("ffn", "parallel", "arbitrary"),
    )(x, g, wg, wu, wd)


def _mix_body(h_ref, g_ref, w_ref, hg_ref, bd_ref, o_ref, k_ref, v_ref, n_ref, *, head_tiles):
    j = pl.program_id(1)

    @pl.when(j == 0)
    def _():
        n_ref[...] = _rms(h_ref[...], g_ref[...]).astype(BF16)

    p = _mm(n_ref[...], w_ref[...])

    @pl.when(j < 2 * head_tiles)
    def _():
        sq = p * p
        hi, lo = _split2(sq)
        bd = bd_ref[...]
        ms = (_mm(hi, bd) + _mm(lo, bd)) * (1.0 / HEAD_DIM)
        normed = p * lax.rsqrt(ms + NORM_EPS) * hg_ref[...]
        o_ref[...] = normed

        @pl.when(j >= head_tiles)
        def _():
            k_ref[...] = normed

    @pl.when(j >= 2 * head_tiles)
    def _():
        o_ref[...] = p

        @pl.when(j < 3 * head_tiles)
        def _():
            v_ref[...] = p


def _mix(h, g, w, hgain, bd, *, width, tm=1024, tn=512):
    n, d = h.shape
    cols = w.shape[1]
    tm = min(tm, n)
    head_tiles = width // tn
    pick = lambda first: (lambda i, j: (i, jnp.clip(j - first, 0, head_tiles - 1)))
    return pl.pallas_call(
        functools.partial(_mix_body, head_tiles=head_tiles),
        out_shape=[jax.ShapeDtypeStruct((n, cols), F32), jax.ShapeDtypeStruct((n, width), F32),
                   jax.ShapeDtypeStruct((n, width), F32)],
        grid=(n // tm, cols // tn),
        in_specs=[
            pl.BlockSpec((tm, d), lambda i, j: (i, 0)),
            pl.BlockSpec((1, d), lambda i, j: (0, 0)),
            pl.BlockSpec((d, tn), lambda i, j: (0, j)),
            pl.BlockSpec((1, tn), lambda i, j: (0, j)),
            pl.BlockSpec((tn, tn), lambda i, j: (0, 0)),
        ],
        out_specs=[pl.BlockSpec((tm, tn), lambda i, j: (i, j)),
                   pl.BlockSpec((tm, tn), pick(head_tiles)), pl.BlockSpec((tm, tn), pick(2 * head_tiles))],
        scratch_shapes=[pltpu.VMEM((tm, d), BF16)],
        **_call_opts("mix", "parallel", "arbitrary"),
    )(h, g, w, hgain, bd)


def _sb_items(items, ut, tk, run, pv):
    zs, parts = {}, {}

    def scores(n):
        _, q16, kblk, _, _ = items[n]
        zs[n] = _nt(kblk, q16)

    def keep_sums(n):
        mask = items[n][4]
        z = zs.pop(n)
        lp = jnp.log(1.0 + jnp.exp2(_neg_abs(z))) * LOG2_E
        log_beta = jnp.minimum(z, 0.0) - lp
        log_keep = log_beta - z
        if mask is not None:
            log_keep = jnp.where(mask, log_keep, 0.0)
        parts[n] = (log_beta, _mm(ut, log_keep.astype(BF16)))

    def weigh(n):
        ci, _, _, vt_rows, mask = items[n]
        log_beta, ext = parts.pop(n)
        w = jnp.exp2(log_beta + (ext[:tk] + run[ci][0:1]))
        if mask is not None:
            w = jnp.where(mask, w, 0.0)
        d = _mm(vt_rows, w.astype(BF16))
        pv[ci] = d if ci not in pv else pv[ci] + d
        run[ci] = run[ci] + ext[tk:tk + 8]

    for step in range(len(items) + 2 * ATT_SKEW):
        if step < len(items):
            scores(step)
        if 0 <= step - ATT_SKEW < len(items):
            keep_sums(step - ATT_SKEW)
        if 0 <= step - 2 * ATT_SKEW < len(items):
            weigh(step - 2 * ATT_SKEW)


def _sb_rows(items, ue, tk, run, pv):
    zs, parts = {}, {}

    def scores(n):
        _, qh, kth, _, _ = items[n]
        zs[n] = jnp.concatenate([_mm(q, kt) for q, kt in zip(qh, kth)], axis=0)

    def keep_sums(n):
        mask = items[n][4]
        z = zs.pop(n)
        lp = jnp.log(1.0 + jnp.exp2(_neg_abs(z))) * LOG2_E
        log_beta = jnp.minimum(z, 0.0) - lp
        log_keep = log_beta - z
        if mask is not None:
            log_keep = jnp.where(mask, log_keep, 0.0)
        parts[n] = (log_beta, _mm(log_keep.astype(BF16), ue))

    def weigh(n):
        ci, qh, _, vth, mask = items[n]
        log_beta, ext = parts.pop(n)
        tail = jnp.concatenate([ext[:, c0:c0 + LANES] + run[ci] for c0 in range(0, tk, LANES)], axis=1)
        w = jnp.exp2(log_beta + tail)
        if mask is not None:
            w = jnp.where(mask, w, 0.0)
        w = w.astype(BF16)
        tq = qh[0].shape[0]
        for h, vt in enumerate(vth):
            d = _nt(w[h * tq:(h + 1) * tq], vt)
            pv[ci, h] = d if (ci, h) not in pv else pv[ci, h] + d
        run[ci] = run[ci] + ext[:, tk:]

    for step in range(len(items) + 2 * ATT_SKEW):
        if step < len(items):
            scores(step)
        if 0 <= step - ATT_SKEW < len(items):
            keep_sums(step - ATT_SKEW)
        if 0 <= step - 2 * ATT_SKEW < len(items):
            weigh(step - 2 * ATT_SKEW)


def _attn_cached_body(q_ref, k_ref, v_ref, ckt_ref, cvt_ref, ue_ref, o_ref, qs, carry, acc, *,
                      tq, tk, n_heads, chunk_blocks, per_chain):
    j = pl.program_id(1)
    n_chains = n_heads // per_chain
    ue = ue_ref[...]
    heads_of = lambda c: range(c * per_chain, (c + 1) * per_chain)

    def run_items(items):
        run = {c: carry[c] for c in range(n_chains)}
        pv = {}
        _sb_rows(items, ue, tk, run, pv)
        for c in range(n_chains):
            carry[c] = run[c]
            for i, h in enumerate(heads_of(c)):
                acc[h] += pv[c, i]

    @pl.when(j == 0)
    def _():
        carry[...] = jnp.zeros_like(carry)
        acc[...] = jnp.zeros_like(acc)
        for h in range(n_heads):
            qs[h] = (q_ref[0, :, h * HEAD_DIM:(h + 1) * HEAD_DIM] * (HEAD_DIM ** -0.5 * LOG2_E)).astype(BF16)
        row = lax.broadcasted_iota(jnp.int32, (per_chain * tq, tk), 0) % tq
        col = lax.broadcasted_iota(jnp.int32, (per_chain * tq, tk), 1)
        pad = jnp.zeros((HEAD_DIM, tk - tq), F32)
        items = []
        for c in range(n_chains):
            def new_t(ref):
                return [jnp.concatenate([ref[0, :, h * HEAD_DIM:(h + 1) * HEAD_DIM].T, pad], axis=1).astype(BF16)
                        for h in heads_of(c)]
            items.append((c, [qs[h] for h in heads_of(c)], new_t(k_ref), new_t(v_ref), col < row))
        run_items(items)

    items = []
    for m in reversed(range(chunk_blocks)):
        keys = slice(m * tk, (m + 1) * tk)
        for c in range(n_chains):
            items.append((c, [qs[h] for h in heads_of(c)],
                          [ckt_ref[0, h, :, keys].astype(BF16) for h in heads_of(c)],
                          [cvt_ref[0, h, :, keys].astype(BF16) for h in heads_of(c)], None))
    run_items(items)

    @pl.when(j == pl.num_programs(1) - 1)
    def _():
        o_ref[0] = jnp.concatenate([acc[h] for h in range(n_heads)], axis=1)


def _attention_cached(p3, cache_k, cache_v, ue, *, width, chunk_keys=1024, per_chain=4):
    b, t, _ = p3.shape
    tk = ATT_TK
    _, p_len, n_heads, _ = cache_k.shape
    chunk_keys = min(chunk_keys, p_len)
    n_chunks = p_len // chunk_keys
    ckt = jnp.transpose(cache_k, (0, 2, 3, 1))
    cvt = jnp.transpose(cache_v, (0, 2, 3, 1))
    cache_spec = pl.BlockSpec((1, n_heads, HEAD_DIM, chunk_keys), lambda bi, j: (bi, 0, 0, n_chunks - 1 - j))
    return pl.pallas_call(
        functools.partial(_attn_cached_body, tq=t, tk=tk, n_heads=n_heads, chunk_blocks=chunk_keys // tk,
                          per_chain=per_chain),
        out_shape=jax.ShapeDtypeStruct((b, t, width), F32),
        grid=(b, n_chunks),
        in_specs=[
            pl.BlockSpec((1, t, width), lambda bi, j: (bi, 0, 0)),
            pl.BlockSpec((1, t, width), lambda bi, j: (bi, 0, 1)),
            pl.BlockSpec((1, t, width), lambda bi, j: (bi, 0, 2)),
            cache_spec, cache_spec,
            pl.BlockSpec(ue.shape, lambda bi, j: (0, 0)),
        ],
        out_specs=pl.BlockSpec((1, t, width), lambda bi, j: (bi, 0, 0)),
        scratch_shapes=[
            pltpu.VMEM((n_heads, t, HEAD_DIM), BF16),
            pltpu.VMEM((n_heads // per_chain, per_chain * t, LANES), F32),
            pltpu.VMEM((n_heads, t, HEAD_DIM), F32),
        ],
        **_call_opts("sb_attention_cached", "parallel", "arbitrary"),
    )(p3, p3, p3, ckt, cvt, ue)


def _attn_body(q_ref, k_ref, v_ref, ut_ref, o_ref, kb, vt, qs, carry, acc, *, tq, tk, n_blocks, new_unroll, chains):
    i = pl.program_id(2)

    @pl.when(i == 0)
    def _():
        def fill(blk, _):
            rows = pl.ds(pl.multiple_of(blk * tk, tk), tk)
            kb[rows, :] = k_ref[0, rows, :].astype(BF16)
            vt[blk] = v_ref[0, rows, :].T.astype(BF16)
            return 0

        lax.fori_loop(0, n_blocks, fill, 0)

    q = q_ref[0] * (HEAD_DIM ** -0.5 * LOG2_E)
    n_heads = q.shape[1] // HEAD_DIM
    head_of_lane = lax.broadcasted_iota(jnp.int32, q.shape, 1) // HEAD_DIM
    zero = jnp.zeros_like(q)
    for h in range(n_heads):
        qs[h * tq:(h + 1) * tq, :] = jnp.where(head_of_lane == h, q, zero).astype(BF16)
    carry[...] = jnp.zeros_like(carry)
    acc[...] = jnp.zeros_like(acc)

    def span(blocks):
        q16 = [qs[c0:c1, :] for c0, c1, _, _ in chains]
        cw = chains[0][1] - chains[0][0]
        row = lax.broadcasted_iota(jnp.int32, (tk, cw), 0)
        lane = lax.broadcasted_iota(jnp.int32, (tk, cw), 1)
        mask = lambda off, c0: None if off is None else (row + off) < (lane + c0) % tq
        items = [(ci, q16[ci], kblk, vtblk[r0:r1], mask(off, c0))
                 for kblk, vtblk, off in blocks for ci, (c0, _, r0, r1) in enumerate(chains)]
        run = {ci: carry[:, c0:c1] for ci, (c0, c1, _, _) in enumerate(chains)}
        pv = {}
        _sb_items(items, ut_ref[...], tk, run, pv)
        for ci, (c0, c1, r0, r1) in enumerate(chains):
            carry[:, c0:c1] = run[ci]
            acc[r0:r1, c0:c1] += pv[ci]

    q0 = i * tq
    n_diag = tq // tk
    diag = []
    for m in reversed(range(n_diag)):
        blk = q0 // tk + m
        diag.append((kb[pl.ds(pl.multiple_of(blk * tk, tk), tk), :], vt[blk], m * tk))
    span(diag)

    def new_blocks(last, count):
        blocks = []
        for m in range(count):
            blk = last - m
            blocks.append((kb[pl.ds(pl.multiple_of(blk * tk, tk), tk), :], vt[blk], None))
        span(blocks)

    last = q0 // tk - 1
    per_trip = new_unroll
    while per_trip >= n_diag:
        def step(it, _, last=last, per_trip=per_trip):
            new_blocks(last - it * per_trip, per_trip)
            return 0

        trips = (last + 1) // per_trip
        lax.fori_loop(0, trips, step, 0)
        last = last - trips * per_trip
        per_trip //= 2

    out_t = acc[...].T
    out = out_t[:tq]
    for h in range(1, n_heads):
        out = jnp.where(head_of_lane == h, out_t[h * tq:(h + 1) * tq], out)
    o_ref[0] = out


def _attention(p3, ut, *, n_pairs, q_blk0, k_blk0, v_blk0, tq):
    b, t, _ = p3.shape
    tk = ATT_TK
    tq = min(tq, t)
    assert tq % tk == 0 and t % tq == 0, (t, tq, tk)
    cw = min(tq, ATT_CHAIN_LANES)
    chains = tuple((h * tq + c0, h * tq + c0 + cw, h * HEAD_DIM, (h + 1) * HEAD_DIM)
                   for h in range(2) for c0 in range(0, tq, cw))
    seq = lambda blk0: pl.BlockSpec((1, t, LANES), lambda bi, p, i: (bi, 0, blk0 + p))
    return pl.pallas_call(
        functools.partial(_attn_body, tq=tq, tk=tk, n_blocks=t // tk, new_unroll=4 * (tq // tk), chains=chains),
        out_shape=jax.ShapeDtypeStruct((b, t, n_pairs * LANES), F32),
        grid=(b, n_pairs, t // tq),
        in_specs=[
            pl.BlockSpec((1, tq, LANES), lambda bi, p, i: (bi, i, q_blk0 + p)),
            seq(k_blk0), seq(v_blk0),
            pl.BlockSpec(ut.shape, lambda bi, p, i: (0, 0)),
        ],
        out_specs=pl.BlockSpec((1, tq, LANES), lambda bi, p, i: (bi, i, p)),
        scratch_shapes=[
            pltpu.VMEM((t, LANES), BF16),
            pltpu.VMEM((t // tk, LANES, tk), BF16),
            pltpu.VMEM((2 * tq, LANES), BF16),
            pltpu.VMEM((8, 2 * tq), F32),
            pltpu.VMEM((LANES, 2 * tq), F32),
        ],
        **_call_opts("sb_attention", "parallel", "parallel", "arbitrary"),
    )(p3, p3, p3, ut)


def _prep_body(prkv_ref, plora_ref, s_rkv_ref, s_lora_ref, mu_rkv_ref, mu_lora_ref, w0_ref, a0_ref, kk_ref, ka_ref,
               ww2_ref, wa2_ref, wg2_ref, member_ref,
               r_o, wl_o, k_o, v_o, av_o, bv_o, g_o, c_rkv, c_lora, *, tc, width):
    t = pl.program_id(1)

    @pl.when(t == 0)
    def _():
        c_rkv[0:1, :] = s_rkv_ref[0]
        c_lora[0:1, :] = s_lora_ref[0]

    def token_mix(p, prev, mu):
        row = lax.broadcasted_iota(jnp.int32, p.shape, 0)
        shifted = jnp.where(row == 0, prev, pltpu.roll(p, 1, 0))
        return p + (shifted - p) * mu

    def rkv_seg(s):
        cs = slice(s * width, (s + 1) * width)
        p = prkv_ref[0, :, cs]
        x = token_mix(p, c_rkv[0:1, cs], mu_rkv_ref[:, cs])
        c_rkv[0:1, cs] = p[tc - 1:tc, :]
        return x

    pl_ = plora_ref[0]
    xl = token_mix(pl_, c_lora[0:1, :], mu_lora_ref[...])
    c_lora[0:1, :] = pl_[tc - 1:tc, :]

    r_o[0] = rkv_seg(0)
    v_o[0] = rkv_seg(2)
    xk = rkv_seg(1)

    dec = w0_ref[...] + _mm(jnp.tanh(xl).astype(BF16), ww2_ref[...])
    nd = -dec
    softplus = jnp.maximum(nd, 0.0) + jnp.log(1.0 + jnp.exp(-jnp.abs(nd)))
    w_log = -softplus - 0.5
    wl_o[0] = -jnp.exp(w_log)
    a = jax.nn.sigmoid(a0_ref[...] + _mm(xl.astype(BF16), wa2_ref[...]))
    g_o[0] = _mm(jax.nn.sigmoid(xl).astype(BF16), wg2_ref[...])
    kk = xk * kk_ref[...]
    k_o[0] = xk * (1.0 + (a - 1.0) * ka_ref[...])
    norm = jnp.sqrt(_head_sum2(kk * kk, member_ref[...]))
    kk = kk / jnp.maximum(norm, 1e-12)
    av_o[0] = -kk
    bv_o[0] = kk * a


def _rwkv_prep(p3, s_rkv, s_lora, mu_rkv, mu_lora, w0, a0, k_k, k_a, ww2, wa2, wg2, member, *, rkv_blk, lora_blk, tc):
    b, t, _ = p3.shape
    width = w0.shape[1]
    lw = mu_lora.shape[1]
    tc = min(tc, t)
    const = lambda shape: pl.BlockSpec(shape, lambda bi, ti: (0,) * len(shape))
    out_spec = pl.BlockSpec((1, tc, width), lambda bi, ti: (bi, ti, 0))
    return pl.pallas_call(
        functools.partial(_prep_body, tc=tc, width=width),
        out_shape=[jax.ShapeDtypeStruct((b, t, width), F32)] * 7,
        grid=(b, t // tc),
        in_specs=[
            pl.BlockSpec((1, tc, 3 * width), lambda bi, ti: (bi, ti, rkv_blk)),
            pl.BlockSpec((1, tc, lw), lambda bi, ti: (bi, ti, lora_blk)),
            pl.BlockSpec((1, 1, 3 * width), lambda bi, ti: (bi, 0, 0)),
            pl.BlockSpec((1, 1, lw), lambda bi, ti: (bi, 0, 0)),
            const((1, 3 * width)), const((1, lw)),
            const((1, width)), const((1, width)), const((1, width)), const((1, width)),
            const((lw, width)), const((lw, width)), const((lw, width)),
            const((width, LANES)),
        ],
        out_specs=[out_spec] * 7,
        scratch_shapes=[pltpu.VMEM((8, 3 * width), F32), pltpu.VMEM((8, lw), F32)],
        **_call_opts("rwkv_prep", "parallel", "arbitrary"),
    )(p3, p3, s_rkv, s_lora, mu_rkv, mu_lora, w0, a0, k_k, k_a, ww2, wa2, wg2, member)


def _scan_body(r_ref, wl_ref, k_ref, v_ref, a_ref, b_ref, s0_ref, tri_ref, y_ref, sout_ref, s_scr, *,
               n_chunks, n_pairs):
    c_len = RW_CHUNK
    t = pl.program_id(2)

    @pl.when(t == 0)
    def _():
        s_scr[...] = s0_ref[0]

    row = lax.broadcasted_iota(jnp.int32, (c_len, c_len), 0)
    col = lax.broadcasted_iota(jnp.int32, (c_len, c_len), 1)
    strict = row > col
    incl = row >= col
    lane = lax.broadcasted_iota(jnp.int32, (c_len, LANES), 1)
    first = lane < HEAD_DIM
    brow = lax.broadcasted_iota(jnp.int32, (LANES, LANES), 0)
    bcol = lax.broadcasted_iota(jnp.int32, (LANES, LANES), 1)
    same_head = (brow // HEAD_DIM) == (bcol // HEAD_DIM)
    eye = brow == bcol
    tri = tri_ref[...]
    bf = lambda x: x.astype(BF16)

    cps = [(c, p) for c in range(n_chunks) for p in range(n_pairs)]
    sls = {(c, p): (slice(c * c_len, (c + 1) * c_len), slice(p * LANES, (p + 1) * LANES)) for c, p in cps}
    tiles = {}
    for c in range(n_chunks):
        sl = slice(c * c_len, (c + 1) * c_len)
        wl = wl_ref[0, sl, :]
        hi = wl.astype(BF16)
        rem = wl - hi.astype(F32)
        mid = rem.astype(BF16)
        lo = (rem - mid.astype(F32)).astype(BF16)
        cum = _mm(tri, hi) + _mm(tri, mid) + _mm(tri, lo)
        tot = cum[c_len - 1:c_len, :]
        e_neg = jnp.exp(-cum)
        e_end = jnp.exp(tot - cum)
        av, bv, kv, vv = a_ref[0, sl, :], b_ref[0, sl, :], k_ref[0, sl, :], v_ref[0, sl, :]
        tiles[c] = dict(at=av * jnp.exp(cum - wl), rt=r_ref[0, sl, :] * jnp.exp(cum), bt=bf(bv * e_neg),
                        kt=bf(kv * e_neg), bh=bv * e_end, kh=kv * e_end, vv=vv, etot=jnp.exp(tot))
    tile = lambda name, c, p: tiles[c][name][:, p * LANES:(p + 1) * LANES]

    chains = [(c, p, h) for c, p in cps for h in range(2)]
    mbk = {}
    for c, p, h in chains:
        sel = first if h == 0 else jnp.logical_not(first)
        at, rt = tile("at", c, p), tile("rt", c, p)
        zero = jnp.zeros_like(at)
        ar = jnp.concatenate([jnp.where(sel, at, zero), jnp.where(sel, rt, zero)], axis=0).astype(BF16)
        mbk[c, p, h] = (_nt(ar, tile("bt", c, p)), _nt(ar, tile("kt", c, p)))
    m_ab, p_rb, m_ak, p_rk, tm = {}, {}, {}, {}, {}
    for ch in chains:
        mb, mk = mbk[ch]
        m_ab[ch] = jnp.where(strict, mb[:c_len], 0.0)
        p_rb[ch] = bf(jnp.where(incl, mb[c_len:], 0.0))
        m_ak[ch] = bf(jnp.where(strict, mk[:c_len], 0.0))
        p_rk[ch] = bf(jnp.where(incl, mk[c_len:], 0.0))
        tm[ch] = jnp.where(row == col, 1.0, 0.0) + jnp.where((row // 2) == (col // 2), m_ab[ch], 0.0)
    s = 2
    while s < c_len:
        off = jnp.logical_and((row // (2 * s)) == (col // (2 * s)), (row // s) != (col // s))
        half = {ch: bf(_mm(bf(tm[ch]), bf(jnp.where(off, m_ab[ch], 0.0)))) for ch in chains}
        tm = {ch: tm[ch] + _mm(half[ch], bf(tm[ch])) for ch in chains}
        s *= 2
    t16 = {ch: bf(tm[ch]) for ch in chains}
    mv = {(c, p, h): _mm(m_ak[c, p, h], bf(tile("vv", c, p))) for c, p, h in chains}
    w1 = {(c, p, h): _mm(t16[c, p, h], bf(tile("at", c, p))) for c, p, h in chains}
    w2 = {ch: _mm(t16[ch], bf(mv[ch])) for ch in chains}
    qc = {(c, p, h): tile("rt", c, p) + _mm(p_rb[c, p, h], bf(w1[c, p, h])) for c, p, h in chains}
    y1 = {(c, p, h): _mm(p_rb[c, p, h], bf(w2[c, p, h])) + _mm(p_rk[c, p, h], bf(tile("vv", c, p)))
          for c, p, h in chains}
    both = lambda d, c, p: jnp.where(first, d[c, p, 0], d[c, p, 1])
    ac_t, dc_t, qcs, y1s = {}, {}, {}, {}
    for c, p in cps:
        bh, kh, vv = tile("bh", c, p), tile("kh", c, p), tile("vv", c, p)
        w1p, w2p = both(w1, c, p), both(w2, c, p)
        a_full = jnp.where(same_head, _tn(bf(w1p), bf(bh)), 0.0) + jnp.where(eye, tile("etot", c, p), 0.0)
        ac_t[c, p] = _split2(a_full)
        dc_t[c, p] = jnp.where(same_head, _tn(bf(jnp.concatenate([w2p, vv], axis=0)),
                                              bf(jnp.concatenate([bh, kh], axis=0))), 0.0)
        qcs[c, p], y1s[c, p] = bf(both(qc, c, p)), both(y1, c, p)
    for p in range(n_pairs):
        state = s_scr[p]
        for c in range(n_chunks):
            rs, ls = sls[c, p]
            s_hi, s_lo = _split2(state)
            a_hi, a_lo = ac_t[c, p]
            y_ref[0, rs, ls] = _nt(qcs[c, p], s_hi) + y1s[c, p]
            state = _mm(s_hi, a_hi) + _mm(s_hi, a_lo) + _mm(s_lo, a_hi) + dc_t[c, p]
        s_scr[p] = state

    @pl.when(t == pl.num_programs(2) - 1)
    def _():
        sout_ref[0] = s_scr[...]


def _rwkv_scan(r, wl, k, v, av, bv, s0_bd, tri, *, tc, pairs_per_step):
    b, t, width = r.shape
    n_pairs = width // LANES
    tc = min(tc, t)
    npb = pairs_per_step
    seq = pl.BlockSpec((1, tc, npb * LANES), lambda bi, p, ti: (bi, ti, p))
    state = pl.BlockSpec((1, npb, LANES, LANES), lambda bi, p, ti: (bi, p, 0, 0))
    return pl.pallas_call(
        functools.partial(_scan_body, n_chunks=tc // RW_CHUNK, n_pairs=npb),
        out_shape=[jax.ShapeDtypeStruct((b, t, width), F32), jax.ShapeDtypeStruct(s0_bd.shape, F32)],
        grid=(b, n_pairs // npb, t // tc),
        in_specs=[seq] * 6 + [state, pl.BlockSpec(tri.shape, lambda bi, p, ti: (0, 0))],
        out_specs=[seq, state],
        scratch_shapes=[pltpu.VMEM((npb, LANES, LANES), F32)],
        **_call_opts("rwkv_scan", "parallel", "parallel", "arbitrary"),
    )(r, wl, k, v, av, bv, s0_bd, tri)


def _out_body(h_ref, osb_ref, y_ref, r_ref, k_ref, v_ref, g_ref, gs_ref, gr_ref, lnw_ref, lnb_ref, rk_ref, member_ref,
              wso_ref, wro_ref, wout_ref, o_ref, m_ref):
    j = pl.program_id(1)

    @pl.when(j == 0)
    def _():
        member = member_ref[...]
        y = y_ref[...]
        mu = _head_sum2(y, member) * (1.0 / HEAD_DIM)
        d = y - mu
        var = _head_sum2(d * d, member) * (1.0 / HEAD_DIM)
        yn = d * lax.rsqrt(var + GN_EPS) * lnw_ref[...] + lnb_ref[...]
        bonus = _head_sum2(r_ref[...] * k_ref[...] * rk_ref[...], member) * v_ref[...]
        yy = ((yn + bonus) * g_ref[...]).astype(BF16)
        o_sb = _mm(osb_ref[...].astype(BF16), wso_ref[...])
        o_rw = _mm(yy, wro_ref[...])
        merged = jax.nn.sigmoid(gs_ref[...]) * o_sb + jax.nn.sigmoid(gr_ref[...]) * o_rw
        m_ref[...] = merged.astype(BF16)

    o_ref[...] = h_ref[...] + _mm(m_ref[...], wout_ref[...])


def _merge_out(h, o_sb, y, r, k, v, g, p2, lnw, lnb, rk, member, wso, wro, wout, *, gs_blk, gr_blk, tm=128, tn=2048):
    n, d = h.shape
    width = o_sb.shape[1]
    tm, tn = min(tm, n), min(tn, d)
    tok = pl.BlockSpec((tm, width), lambda i, j: (i, 0))
    const = lambda shape: pl.BlockSpec(shape, lambda i, j: (0,) * len(shape))
    return pl.pallas_call(
        _out_body,
        out_shape=jax.ShapeDtypeStruct((n, d), F32),
        grid=(n // tm, d // tn),
        in_specs=[
            pl.BlockSpec((tm, tn), lambda i, j: (i, j)),
            tok, tok, tok, tok, tok, tok,
            pl.BlockSpec((tm, d), lambda i, j: (i, gs_blk)),
            pl.BlockSpec((tm, d), lambda i, j: (i, gr_blk)),
            const((1, width)), const((1, width)), const((1, width)),
            const((width, LANES)),
            const((width, d)), const((width, d)),
            pl.BlockSpec((d, tn), lambda i, j: (0, j)),
        ],
        out_specs=pl.BlockSpec((tm, tn), lambda i, j: (i, j)),
        scratch_shapes=[pltpu.VMEM((tm, d), BF16)],
        **_call_opts("merge_out", "parallel", "arbitrary"),
    )(h, o_sb, y, r, k, v, g, p2, p2, lnw, lnb, rk, member, wso, wro, wout)


def _layer(x, past_k, past_v, wkv0, shift0, w, *, tq, tc, scan_pairs):
    b, t, d = x.shape
    n = b * t
    width = w["w0"].shape[1]
    h1 = _ffn(x.reshape(n, d), w["ffn1_norm"], w["ffn1_wg"], w["ffn1_wu"], w["ffn1_wd"])
    p2, k_rows, v_rows = _mix(h1, w["mix_norm"], w["w_in"], w["head_gain"], w["bd"], width=width)
    p3 = p2.reshape(b, t, -1)
    n_pairs = width // LANES
    if past_k is None:
        o_sb = _attention(p3, w["u2"], n_pairs=n_pairs, q_blk0=0, k_blk0=n_pairs, v_blk0=2 * n_pairs, tq=tq)
    else:
        o_sb = _attention_cached(p3, past_k, past_v, w["ue"], width=width)

    lora_w = w["mu_lora"].shape[1]
    lora_blk = (6 * width + 2 * d) // lora_w
    lora_cols = w["lora_cols"]
    s_rkv = shift0[:, :, :3 * width]
    s_lora = jnp.pad(shift0[:, :, 3 * width:], ((0, 0), (0, 0), (0, lora_w - lora_cols)))
    r, wl, k, v, av, bv, g = _rwkv_prep(
        p3, s_rkv, s_lora, w["mu_rkv"], w["mu_lora"], w["w0"], w["a0"], w["k_k"], w["k_a"],
        w["ww2"], w["wa2"], w["wg2"], w["member"], rkv_blk=1, lora_blk=lora_blk, tc=tc)

    s0 = wkv0.reshape(b, n_pairs, 2, HEAD_DIM, HEAD_DIM)
    z = jnp.zeros_like(s0[:, :, 0])
    s0_bd = jnp.concatenate([jnp.concatenate([s0[:, :, 0], z], axis=-1),
                             jnp.concatenate([z, s0[:, :, 1]], axis=-1)], axis=-2)
    y, s_bd = _rwkv_scan(r, wl, k, v, av, bv, s0_bd, w["tri"], tc=tc, pairs_per_step=scan_pairs)
    wkv = jnp.stack([s_bd[:, :, :HEAD_DIM, :HEAD_DIM], s_bd[:, :, HEAD_DIM:, HEAD_DIM:]], axis=2)
    wkv = wkv.reshape(b, 2 * n_pairs, HEAD_DIM, HEAD_DIM)

    flat = lambda a: a.reshape(n, width)
    gs_blk = (6 * width) // d
    h2 = _merge_out(h1, flat(o_sb), flat(y), flat(r), flat(k), flat(v), flat(g), p2,
                    w["ln_w"], w["ln_b"], w["r_k"], w["member"], w["sb_wo"], w["rw_wo"], w["w_out"],
                    gs_blk=gs_blk, gr_blk=gs_blk + 1)
    out = _ffn(h2, w["ffn2_norm"], w["ffn2_wg"], w["ffn2_wu"], w["ffn2_wd"])

    heads = width // HEAD_DIM
    k_new = k_rows.reshape(b, t, heads, HEAD_DIM)
    v_new = v_rows.reshape(b, t, heads, HEAD_DIM)
    shift = jnp.concatenate([p3[:, t - 1:, 3 * width:6 * width],
                             p3[:, t - 1:, 6 * width + 2 * d:6 * width + 2 * d + lora_cols]], axis=-1)
    return out.reshape(b, t, d), k_new, v_new, wkv, shift


def _layer_weights(l, ffn1_norm, ffn1_w_gate, ffn1_w_up, ffn1_w_down, mix_norm, w_in, sb_q_norm, sb_k_norm, sb_w_o,
                   rwkv_mu, rwkv_w0, rwkv_w_w2, rwkv_a0, rwkv_w_a2, rwkv_w_g2, rwkv_k_k, rwkv_k_a, rwkv_r_k,
                   rwkv_ln_w, rwkv_ln_b, rwkv_w_o, w_out, ffn2_norm, ffn2_w_gate, ffn2_w_up, ffn2_w_down):
    d = w_in.shape[1]
    width = rwkv_w0.shape[1]
    heads = width // HEAD_DIM
    n_decay, n_iclr, n_gate = rwkv_w_w2.shape[1], rwkv_w_a2.shape[1], rwkv_w_g2.shape[1]
    lora_cols = n_decay + n_iclr + n_gate
    lora_w = -(-lora_cols // 512) * 512
    row = lambda a: a.reshape(1, -1).astype(F32)
    wi = w_in[l]
    w_in_p = jnp.concatenate([
        wi[:, :6 * width], wi[:, 6 * width + lora_cols:], wi[:, 6 * width:6 * width + lora_cols],
        jnp.zeros((d, lora_w - lora_cols), wi.dtype)], axis=1).astype(BF16)
    total = w_in_p.shape[1]
    head_gain = jnp.concatenate([jnp.tile(sb_q_norm[l], heads), jnp.tile(sb_k_norm[l], heads),
                                 jnp.ones((total - 2 * width,), F32)]).reshape(1, total)
    mu = rwkv_mu[l]

    def lora_pad(wm, r0):
        return jnp.zeros((lora_w, width), F32).at[r0:r0 + wm.shape[0]].set(wm).astype(BF16)

    hid = jnp.arange(width) // HEAD_DIM
    bid = jnp.arange(512) // HEAD_DIM
    bd = (bid[:, None] == bid[None, :]).astype(BF16)
    member = (hid[:, None] == jnp.arange(LANES)[None, :]).astype(BF16)
    tk = ATT_TK
    ki = jnp.arange(tk)
    u2 = jnp.concatenate([(ki[None, :] > ki[:, None]).astype(BF16), jnp.ones((16, tk), BF16)], axis=0)
    ue = jnp.concatenate([(ki[:, None] > ki[None, :]).astype(BF16), jnp.ones((tk, LANES), BF16)], axis=1)
    ci = jnp.arange(RW_CHUNK)
    tri = (ci[:, None] >= ci[None, :]).astype(BF16)
    return {
        "ffn1_norm": row(ffn1_norm[l]), "ffn1_wg": ffn1_w_gate[l].astype(BF16), "ffn1_wu": ffn1_w_up[l].astype(BF16),
        "ffn1_wd": ffn1_w_down[l].astype(BF16),
        "ffn2_norm": row(ffn2_norm[l]), "ffn2_wg": ffn2_w_gate[l].astype(BF16), "ffn2_wu": ffn2_w_up[l].astype(BF16),
        "ffn2_wd": ffn2_w_down[l].astype(BF16),
        "mix_norm": row(mix_norm[l]), "w_in": w_in_p, "head_gain": head_gain, "bd": bd, "member": member, "u2": u2, "ue": ue, "tri": tri,
        "mu_rkv": row(mu[:3 * width]), "mu_lora": row(jnp.pad(mu[3 * width:], (0, lora_w - lora_cols))),
        "lora_cols": lora_cols,
        "w0": row(rwkv_w0[l]), "a0": row(rwkv_a0[l]), "k_k": row(rwkv_k_k[l]), "k_a": row(rwkv_k_a[l]),
        "ww2": lora_pad(rwkv_w_w2[l], 0), "wa2": lora_pad(rwkv_w_a2[l], n_decay),
        "wg2": lora_pad(rwkv_w_g2[l], n_decay + n_iclr),
        "ln_w": row(rwkv_ln_w[l]), "ln_b": row(rwkv_ln_b[l]), "r_k": row(rwkv_r_k[l]),
        "sb_wo": sb_w_o[l].astype(BF16), "rw_wo": rwkv_w_o[l].astype(BF16), "w_out": w_out[l].astype(BF16),
    }


def kernel(x_prompt, x_sample, cache_sb_k, cache_sb_v, state_rwkv_wkv, state_rwkv_shift, ffn1_norm, ffn1_w_gate, ffn1_w_up, ffn1_w_down, mix_norm, w_in, sb_q_norm, sb_k_norm, sb_w_o, rwkv_mu, rwkv_w0, rwkv_w_w2, rwkv_a0, rwkv_w_a2, rwkv_w_g2, rwkv_k_k, rwkv_k_a, rwkv_r_k, rwkv_ln_w, rwkv_ln_b, rwkv_w_o, w_out, ffn2_norm, ffn2_w_gate, ffn2_w_up, ffn2_w_down):
    depth = w_in.shape[0]
    yp, ys = x_prompt, x_sample
    bp = x_prompt.shape[0]
    width = rwkv_w0.shape[1]
    heads = width // HEAD_DIM
    rw_cols = state_rwkv_shift.shape[-1]
    outs = [[] for _ in range(8)]
    for l in range(depth):
        w = _layer_weights(l, ffn1_norm, ffn1_w_gate, ffn1_w_up, ffn1_w_down, mix_norm, w_in, sb_q_norm, sb_k_norm,
                           sb_w_o, rwkv_mu, rwkv_w0, rwkv_w_w2, rwkv_a0, rwkv_w_a2, rwkv_w_g2, rwkv_k_k, rwkv_k_a,
                           rwkv_r_k, rwkv_ln_w, rwkv_ln_b, rwkv_w_o, w_out, ffn2_norm, ffn2_w_gate, ffn2_w_up,
                           ffn2_w_down)
        wkv_zero = jnp.zeros((bp, heads, HEAD_DIM, HEAD_DIM), F32)
        shift_zero = jnp.zeros((bp, 1, rw_cols), F32)
        yp, kp, vp, wkvp, shp = _layer(yp, None, None, wkv_zero, shift_zero, w, tq=512, tc=256, scan_pairs=4)
        ys, kn, vn, wkvn, shn = _layer(ys, cache_sb_k[l], cache_sb_v[l], state_rwkv_wkv[l], state_rwkv_shift[l], w, tq=64, tc=64, scan_pairs=8)
        for lst, val in zip(outs, (kp, vp, wkvp, shp, kn, vn, wkvn, shn)):
            lst.append(val)
    return (yp, ys) + tuple(jnp.stack(o) for o in outs)
```

```python
import functools

import jax
import jax.numpy as jnp
from jax import lax
from jax.experimental import pallas as pl
from jax.experimental.pallas import tpu as pltpu

F32 = jnp.float32
BF16 = jnp.bfloat16

HEAD_DIM = 64
LANES = 128
NORM_EPS = 1e-6
GN_EPS = 64e-5
RW_CHUNK = 64
ATT_TK = 256

MIB = 1024 * 1024
LOG2_E = 1.4426950408889634
SIGN_BIT = 0x80000000
ATT_CHAIN_LANES = 256
ATT_SKEW = 2


def _nt(x, y):
    return lax.dot_general(x, y, (((1,), (1,)), ((), ())), preferred_element_type=F32)


def _tn(x, y):
    return lax.dot_general(x, y, (((0,), (0,)), ((), ())), preferred_element_type=F32)


def _mm(x, y):
    return jnp.dot(x, y, preferred_element_type=F32)


def _neg_abs(x):
    return lax.bitcast_convert_type(lax.bitcast_convert_type(x, jnp.uint32) | jnp.uint32(SIGN_BIT), F32)


def _split2(x):
    hi = x.astype(BF16)
    lo = (x - hi.astype(F32)).astype(BF16)
    return hi, lo


def _head_sum2(x, member):
    hi, lo = _split2(x)
    shi, slo = _split2(_mm(hi, member) + _mm(lo, member))
    return _nt(shi, member) + _nt(slo, member)


def _rms(x, g):
    ms = jnp.mean(x * x, axis=-1, keepdims=True)
    return x * lax.rsqrt(ms + NORM_EPS) * g


VMEM_LIMIT_MIB = {"ffn": 48, "mix": 48, "sb_attention": 48, "sb_attention_cached": 48, "rwkv_prep": 48,
                  "rwkv_scan": 32, "merge_out": 56}


def _call_opts(name, *sem):
    return dict(name=name, compiler_params=pltpu.CompilerParams(
        dimension_semantics=sem, vmem_limit_bytes=VMEM_LIMIT_MIB[name] * MIB))


def _ffn_body(x_ref, g_ref, wg_ref, wu_ref, wd_ref, o_ref, n_ref, acc_ref):
    f = pl.program_id(1)

    @pl.when(f == 0)
    def _():
        n_ref[...] = _rms(x_ref[...], g_ref[...]).astype(BF16)
        acc_ref[...] = jnp.zeros_like(acc_ref)

    n = n_ref[...]
    g = _mm(n, wg_ref[...])
    u = _mm(n, wu_ref[...])
    a = (g * jax.nn.sigmoid(g) * u).astype(BF16)
    acc_ref[...] += _mm(a, wd_ref[...])

    @pl.when(f == pl.num_programs(1) - 1)
    def _():
        o_ref[...] = x_ref[...] + 0.5 * acc_ref[...]


def _ffn(x, g, wg, wu, wd, *, tm=512, tf=512):
    n, d = x.shape
    ff = wg.shape[1]
    tm, tf = min(tm, n), min(tf, ff)
    return pl.pallas_call(
        _ffn_body,
        out_shape=jax.ShapeDtypeStruct((n, d), F32),
        grid=(n // tm, ff // tf),
        in_specs=[
            pl.BlockSpec((tm, d), lambda i, f: (i, 0)),
            pl.BlockSpec((1, d), lambda i, f: (0, 0)),
            pl.BlockSpec((d, tf), lambda i, f: (0, f)),
            pl.BlockSpec((d, tf), lambda i, f: (0, f)),
            pl.BlockSpec((tf, d), lambda i, f: (f, 0)),
        ],
        out_specs=pl.BlockSpec((tm, d), lambda i, f: (i, 0)),
        scratch_shapes=[pltpu.VMEM((tm, d), BF16), pltpu.VMEM((tm, d), F32)],
        **_call_opts("ffn", "parallel", "arbitrary"),
    )(x, g, wg, wu, wd)


def _mix_body(h_ref, g_ref, w_ref, hg_ref, bd_ref, o_ref, n_ref, *, norm_tiles):
    j = pl.program_id(1)

    @pl.when(j == 0)
    def _():
        n_ref[...] = _rms(h_ref[...], g_ref[...]).astype(BF16)

    p = _mm(n_ref[...], w_ref[...])

    @pl.when(j < norm_tiles)
    def _():
        hi, lo = _split2(p * p)
        bd = bd_ref[...]
        ms = (_mm(hi, bd) + _mm(lo, bd)) * (1.0 / HEAD_DIM)
        o_ref[...] = p * lax.rsqrt(ms + NORM_EPS) * hg_ref[...]

    @pl.when(j >= norm_tiles)
    def _():
        o_ref[...] = p


def _mix(h, g, w, hgain, bd, *, width, tm=1024, tn=512):
    n, d = h.shape
    cols = w.shape[1]
    tm = min(tm, n)
    return pl.pallas_call(
        functools.partial(_mix_body, norm_tiles=2 * width // tn),
        out_shape=jax.ShapeDtypeStruct((n, cols), F32),
        grid=(n // tm, cols // tn),
        in_specs=[
            pl.BlockSpec((tm, d), lambda i, j: (i, 0)),
            pl.BlockSpec((1, d), lambda i, j: (0, 0)),
            pl.BlockSpec((d, tn), lambda i, j: (0, j)),
            pl.BlockSpec((1, tn), lambda i, j: (0, j)),
            pl.BlockSpec((tn, tn), lambda i, j: (0, 0)),
        ],
        out_specs=pl.BlockSpec((tm, tn), lambda i, j: (i, j)),
        scratch_shapes=[pltpu.VMEM((tm, d), BF16)],
        **_call_opts("mix", "parallel", "arbitrary"),
    )(h, g, w, hgain, bd)


def _sb_items(items, ut, tk, run, pv):
    zs, parts = {}, {}

    def scores(n):
        _, q16, kblk, _, _ = items[n]
        zs[n] = _nt(kblk, q16)

    def keep_sums(n):
        mask = items[n][4]
        z = zs.pop(n)
        lp = jnp.log(1.0 + jnp.exp2(_neg_abs(z))) * LOG2_E
        log_beta = jnp.minimum(z, 0.0) - lp
        log_keep = log_beta - z
        if mask is not None:
            log_keep = jnp.where(mask, log_keep, 0.0)
        parts[n] = (log_beta, _mm(ut, log_keep.astype(BF16)))

    def weigh(n):
        ci, _, _, vt_rows, mask = items[n]
        log_beta, ext = parts.pop(n)
        w = jnp.exp2(log_beta + (ext[:tk] + run[ci][0:1]))
        if mask is not None:
            w = jnp.where(mask, w, 0.0)
        d = _mm(vt_rows, w.astype(BF16))
        pv[ci] = d if ci not in pv else pv[ci] + d
        run[ci] = run[ci] + ext[tk:tk + 8]

    for step in range(len(items) + 2 * ATT_SKEW):
        if step < len(items):
            scores(step)
        if 0 <= step - ATT_SKEW < len(items):
            keep_sums(step - ATT_SKEW)
        if 0 <= step - 2 * ATT_SKEW < len(items):
            weigh(step - 2 * ATT_SKEW)


def _sb_rows(items, ue, tk, run, pv):
    zs, parts = {}, {}

    def scores(n):
        _, qh, kth, _, _ = items[n]
        zs[n] = jnp.concatenate([_mm(q, kt) for q, kt in zip(qh, kth)], axis=0)

    def keep_sums(n):
        mask = items[n][4]
        z = zs.pop(n)
        lp = jnp.log(1.0 + jnp.exp2(_neg_abs(z))) * LOG2_E
        log_beta = jnp.minimum(z, 0.0) - lp
        log_keep = log_beta - z
        if mask is not None:
            log_keep = jnp.where(mask, log_keep, 0.0)
        parts[n] = (log_beta, _mm(log_keep.astype(BF16), ue))

    def weigh(n):
        ci, qh, _, vth, mask = items[n]
        log_beta, ext = parts.pop(n)
        tail = jnp.concatenate([ext[:, c0:c0 + LANES] + run[ci] for c0 in range(0, tk, LANES)], axis=1)
        w = jnp.exp2(log_beta + tail)
        if mask is not None:
            w = jnp.where(mask, w, 0.0)
        w = w.astype(BF16)
        tq = qh[0].shape[0]
        for h, vt in enumerate(vth):
            d = _nt(w[h * tq:(h + 1) * tq], vt)
            pv[ci, h] = d if (ci, h) not in pv else pv[ci, h] + d
        run[ci] = run[ci] + ext[:, tk:]

    for step in range(len(items) + 2 * ATT_SKEW):
        if step < len(items):
            scores(step)
        if 0 <= step - ATT_SKEW < len(items):
            keep_sums(step - ATT_SKEW)
        if 0 <= step - 2 * ATT_SKEW < len(items):
            weigh(step - 2 * ATT_SKEW)


def _attn_cached_body(q_ref, k_ref, v_ref, ckt_ref, cvt_ref, ue_ref, o_ref, qs, carry, acc, *,
                      tq, tk, n_heads, chunk_blocks, per_chain):
    j = pl.program_id(1)
    n_chains = n_heads // per_chain
    ue = ue_ref[...]
    heads_of = lambda c: range(c * per_chain, (c + 1) * per_chain)

    def run_items(items):
        run = {c: carry[c] for c in range(n_chains)}
        pv = {}
        _sb_rows(items, ue, tk, run, pv)
        for c in range(n_chains):
            carry[c] = run[c]
            for i, h in enumerate(heads_of(c)):
                acc[h] += pv[c, i]

    @pl.when(j == 0)
    def _():
        carry[...] = jnp.zeros_like(carry)
        acc[...] = jnp.zeros_like(acc)
        for h in range(n_heads):
            qs[h] = (q_ref[0, :, h * HEAD_DIM:(h + 1) * HEAD_DIM] * (HEAD_DIM ** -0.5 * LOG2_E)).astype(BF16)
        row = lax.broadcasted_iota(jnp.int32, (per_chain * tq, tk), 0) % tq
        col = lax.broadcasted_iota(jnp.int32, (per_chain * tq, tk), 1)
        pad = jnp.zeros((HEAD_DIM, tk - tq), F32)
        items = []
        for c in range(n_chains):
            def new_t(ref):
                return [jnp.concatenate([ref[0, :, h * HEAD_DIM:(h + 1) * HEAD_DIM].T, pad], axis=1).astype(BF16)
                        for h in heads_of(c)]
            items.append((c, [qs[h] for h in heads_of(c)], new_t(k_ref), new_t(v_ref), col < row))
        run_items(items)

    items = []
    for m in reversed(range(chunk_blocks)):
        keys = slice(m * tk, (m + 1) * tk)
        for c in range(n_chains):
            items.append((c, [qs[h] for h in heads_of(c)],
                          [ckt_ref[0, h, :, keys].astype(BF16) for h in heads_of(c)],
                          [cvt_ref[0, h, :, keys].astype(BF16) for h in heads_of(c)], None))
    run_items(items)

    @pl.when(j == pl.num_programs(1) - 1)
    def _():
        o_ref[0] = jnp.concatenate([acc[h] for h in range(n_heads)], axis=1)


def _attention_cached(p3, cache_k, cache_v, ue, *, width, chunk_keys=1024, per_chain=4):
    b, t, _ = p3.shape
    tk = ATT_TK
    _, p_len, n_heads, _ = cache_k.shape
    chunk_keys = min(chunk_keys, p_len)
    n_chunks = p_len // chunk_keys
    ckt = jnp.transpose(cache_k, (0, 2, 3, 1))
    cvt = jnp.transpose(cache_v, (0, 2, 3, 1))
    cache_spec = pl.BlockSpec((1, n_heads, HEAD_DIM, chunk_keys), lambda bi, j: (bi, 0, 0, n_chunks - 1 - j))
    return pl.pallas_call(
        functools.partial(_attn_cached_body, tq=t, tk=tk, n_heads=n_heads, chunk_blocks=chunk_keys // tk,
                          per_chain=per_chain),
        out_shape=jax.ShapeDtypeStruct((b, t, width), F32),
        grid=(b, n_chunks),
        in_specs=[
            pl.BlockSpec((1, t, width), lambda bi, j: (bi, 0, 0)),
            pl.BlockSpec((1, t, width), lambda bi, j: (bi, 0, 1)),
            pl.BlockSpec((1, t, width), lambda bi, j: (bi, 0, 2)),
            cache_spec, cache_spec,
            pl.BlockSpec(ue.shape, lambda bi, j: (0, 0)),
        ],
        out_specs=pl.BlockSpec((1, t, width), lambda bi, j: (bi, 0, 0)),
        scratch_shapes=[
            pltpu.VMEM((n_heads, t, HEAD_DIM), BF16),
            pltpu.VMEM((n_heads // per_chain, per_chain * t, LANES), F32),
            pltpu.VMEM((n_heads, t, HEAD_DIM), F32),
        ],
        **_call_opts("sb_attention_cached", "parallel", "arbitrary"),
    )(p3, p3, p3, ckt, cvt, ue)


def _attn_body(q_ref, k_ref, v_ref, ut_ref, o_ref, kb, vt, qs, carry, acc, *, tq, tk, n_blocks, new_unroll, chains):
    i = pl.program_id(2)

    @pl.when(i == 0)
    def _():
        def fill(blk, _):
            rows = pl.ds(pl.multiple_of(blk * tk, tk), tk)
            kb[rows, :] = k_ref[0, rows, :].astype(BF16)
            vt[blk] = v_ref[0, rows, :].T.astype(BF16)
            return 0

        lax.fori_loop(0, n_blocks, fill, 0)

    q = q_ref[0] * (HEAD_DIM ** -0.5 * LOG2_E)
    n_heads = q.shape[1] // HEAD_DIM
    head_of_lane = lax.broadcasted_iota(jnp.int32, q.shape, 1) // HEAD_DIM
    zero = jnp.zeros_like(q)
    for h in range(n_heads):
        qs[h * tq:(h + 1) * tq, :] = jnp.where(head_of_lane == h, q, zero).astype(BF16)
    carry[...] = jnp.zeros_like(carry)
    acc[...] = jnp.zeros_like(acc)

    def span(blocks):
        q16 = [qs[c0:c1, :] for c0, c1, _, _ in chains]
        cw = chains[0][1] - chains[0][0]
        row = lax.broadcasted_iota(jnp.int32, (tk, cw), 0)
        lane = lax.broadcasted_iota(jnp.int32, (tk, cw), 1)
        mask = lambda off, c0: None if off is None else (row + off) < (lane + c0) % tq
        items = [(ci, q16[ci], kblk, vtblk[r0:r1], mask(off, c0))
                 for kblk, vtblk, off in blocks for ci, (c0, _, r0, r1) in enumerate(chains)]
        run = {ci: carry[:, c0:c1] for ci, (c0, c1, _, _) in enumerate(chains)}
        pv = {}
        _sb_items(items, ut_ref[...], tk, run, pv)
        for ci, (c0, c1, r0, r1) in enumerate(chains):
            carry[:, c0:c1] = run[ci]
            acc[r0:r1, c0:c1] += pv[ci]

    q0 = i * tq
    n_diag = tq // tk
    diag = []
    for m in reversed(range(n_diag)):
        blk = q0 // tk + m
        diag.append((kb[pl.ds(pl.multiple_of(blk * tk, tk), tk), :], vt[blk], m * tk))
    span(diag)

    def new_blocks(last, count):
        blocks = []
        for m in range(count):
            blk = last - m
            blocks.append((kb[pl.ds(pl.multiple_of(blk * tk, tk), tk), :], vt[blk], None))
        span(blocks)

    last = q0 // tk - 1
    per_trip = new_unroll
    while per_trip >= n_diag:
        def step(it, _, last=last, per_trip=per_trip):
            new_blocks(last - it * per_trip, per_trip)
            return 0

        trips = (last + 1) // per_trip
        lax.fori_loop(0, trips, step, 0)
        last = last - trips * per_trip
        per_trip //= 2

    out_t = acc[...].T
    out = out_t[:tq]
    for h in range(1, n_heads):
        out = jnp.where(head_of_lane == h, out_t[h * tq:(h + 1) * tq], out)
    o_ref[0] = out


def _attention(p3, ut, *, n_pairs, q_blk0, k_blk0, v_blk0, tq):
    b, t, _ = p3.shape
    tk = ATT_TK
    tq = min(tq, t)
    assert tq % tk == 0 and t % tq == 0, (t, tq, tk)
    cw = min(tq, ATT_CHAIN_LANES)
    chains = tuple((h * tq + c0, h * tq + c0 + cw, h * HEAD_DIM, (h + 1) * HEAD_DIM)
                   for h in range(2) for c0 in range(0, tq, cw))
    seq = lambda blk0: pl.BlockSpec((1, t, LANES), lambda bi, p, i: (bi, 0, blk0 + p))
    return pl.pallas_call(
        functools.partial(_attn_body, tq=tq, tk=tk, n_blocks=t // tk, new_unroll=4 * (tq // tk), chains=chains),
        out_shape=jax.ShapeDtypeStruct((b, t, n_pairs * LANES), F32),
        grid=(b, n_pairs, t // tq),
        in_specs=[
            pl.BlockSpec((1, tq, LANES), lambda bi, p, i: (bi, i, q_blk0 + p)),
            seq(k_blk0), seq(v_blk0),
            pl.BlockSpec(ut.shape, lambda bi, p, i: (0, 0)),
        ],
        out_specs=pl.BlockSpec((1, tq, LANES), lambda bi, p, i: (bi, i, p)),
        scratch_shapes=[
            pltpu.VMEM((t, LANES), BF16),
            pltpu.VMEM((t // tk, LANES, tk), BF16),
            pltpu.VMEM((2 * tq, LANES), BF16),
            pltpu.VMEM((8, 2 * tq), F32),
            pltpu.VMEM((LANES, 2 * tq), F32),
        ],
        **_call_opts("sb_attention", "parallel", "parallel", "arbitrary"),
    )(p3, p3, p3, ut)


def _prep_body(prkv_ref, plora_ref, s_rkv_ref, s_lora_ref, mu_rkv_ref, mu_lora_ref, w0_ref, a0_ref, kk_ref, ka_ref,
               ww2_ref, wa2_ref, wg2_ref, member_ref,
               r_o, wl_o, k_o, v_o, av_o, bv_o, g_o, c_rkv, c_lora, *, tc, width):
    t = pl.program_id(1)

    @pl.when(t == 0)
    def _():
        c_rkv[0:1, :] = s_rkv_ref[0]
        c_lora[0:1, :] = s_lora_ref[0]

    def token_mix(p, prev, mu):
        row = lax.broadcasted_iota(jnp.int32, p.shape, 0)
        shifted = jnp.where(row == 0, prev, pltpu.roll(p, 1, 0))
        return p + (shifted - p) * mu

    def rkv_seg(s):
        cs = slice(s * width, (s + 1) * width)
        p = prkv_ref[0, :, cs]
        x = token_mix(p, c_rkv[0:1, cs], mu_rkv_ref[:, cs])
        c_rkv[0:1, cs] = p[tc - 1:tc, :]
        return x

    pl_ = plora_ref[0]
    xl = token_mix(pl_, c_lora[0:1, :], mu_lora_ref[...])
    c_lora[0:1, :] = pl_[tc - 1:tc, :]

    r_o[0] = rkv_seg(0)
    v_o[0] = rkv_seg(2)
    xk = rkv_seg(1)

    dec = w0_ref[...] + _mm(jnp.tanh(xl).astype(BF16), ww2_ref[...])
    nd = -dec
    softplus = jnp.maximum(nd, 0.0) + jnp.log(1.0 + jnp.exp(-jnp.abs(nd)))
    w_log = -softplus - 0.5
    wl_o[0] = -jnp.exp(w_log)
    a = jax.nn.sigmoid(a0_ref[...] + _mm(xl.astype(BF16), wa2_ref[...]))
    g_o[0] = _mm(jax.nn.sigmoid(xl).astype(BF16), wg2_ref[...])
    kk = xk * kk_ref[...]
    k_o[0] = xk * (1.0 + (a - 1.0) * ka_ref[...])
    norm = jnp.sqrt(_head_sum2(kk * kk, member_ref[...]))
    kk = kk / jnp.maximum(norm, 1e-12)
    av_o[0] = -kk
    bv_o[0] = kk * a


def _rwkv_prep(p3, s_rkv, s_lora, mu_rkv, mu_lora, w0, a0, k_k, k_a, ww2, wa2, wg2, member, *, rkv_blk, lora_blk, tc):
    b, t, _ = p3.shape
    width = w0.shape[1]
    lw = mu_lora.shape[1]
    tc = min(tc, t)
    const = lambda shape: pl.BlockSpec(shape, lambda bi, ti: (0,) * len(shape))
    out_spec = pl.BlockSpec((1, tc, width), lambda bi, ti: (bi, ti, 0))
    return pl.pallas_call(
        functools.partial(_prep_body, tc=tc, width=width),
        out_shape=[jax.ShapeDtypeStruct((b, t, width), F32)] * 7,
        grid=(b, t // tc),
        in_specs=[
            pl.BlockSpec((1, tc, 3 * width), lambda bi, ti: (bi, ti, rkv_blk)),
            pl.BlockSpec((1, tc, lw), lambda bi, ti: (bi, ti, lora_blk)),
            pl.BlockSpec((1, 1, 3 * width), lambda bi, ti: (bi, 0, 0)),
            pl.BlockSpec((1, 1, lw), lambda bi, ti: (bi, 0, 0)),
            const((1, 3 * width)), const((1, lw)),
            const((1, width)), const((1, width)), const((1, width)), const((1, width)),
            const((lw, width)), const((lw, width)), const((lw, width)),
            const((width, LANES)),
        ],
        out_specs=[out_spec] * 7,
        scratch_shapes=[pltpu.VMEM((8, 3 * width), F32), pltpu.VMEM((8, lw), F32)],
        **_call_opts("rwkv_prep", "parallel", "arbitrary"),
    )(p3, p3, s_rkv, s_lora, mu_rkv, mu_lora, w0, a0, k_k, k_a, ww2, wa2, wg2, member)


def _scan_body(r_ref, wl_ref, k_ref, v_ref, a_ref, b_ref, s0_ref, tri_ref, y_ref, sout_ref, s_scr, *,
               n_chunks, n_pairs):
    c_len = RW_CHUNK
    t = pl.program_id(2)

    @pl.when(t == 0)
    def _():
        s_scr[...] = s0_ref[0]

    row = lax.broadcasted_iota(jnp.int32, (c_len, c_len), 0)
    col = lax.broadcasted_iota(jnp.int32, (c_len, c_len), 1)
    strict = row > col
    incl = row >= col
    lane = lax.broadcasted_iota(jnp.int32, (c_len, LANES), 1)
    first = lane < HEAD_DIM
    brow = lax.broadcasted_iota(jnp.int32, (LANES, LANES), 0)
    bcol = lax.broadcasted_iota(jnp.int32, (LANES, LANES), 1)
    same_head = (brow // HEAD_DIM) == (bcol // HEAD_DIM)
    eye = brow == bcol
    tri = tri_ref[...]
    bf = lambda x: x.astype(BF16)

    cps = [(c, p) for c in range(n_chunks) for p in range(n_pairs)]
    sls = {(c, p): (slice(c * c_len, (c + 1) * c_len), slice(p * LANES, (p + 1) * LANES)) for c, p in cps}
    tiles = {}
    for c in range(n_chunks):
        sl = slice(c * c_len, (c + 1) * c_len)
        wl = wl_ref[0, sl, :]
        hi = wl.astype(BF16)
        rem = wl - hi.astype(F32)
        mid = rem.astype(BF16)
        lo = (rem - mid.astype(F32)).astype(BF16)
        cum = _mm(tri, hi) + _mm(tri, mid) + _mm(tri, lo)
        tot = cum[c_len - 1:c_len, :]
        e_neg = jnp.exp(-cum)
        e_end = jnp.exp(tot - cum)
        av, bv, kv, vv = a_ref[0, sl, :], b_ref[0, sl, :], k_ref[0, sl, :], v_ref[0, sl, :]
        tiles[c] = dict(at=av * jnp.exp(cum - wl), rt=r_ref[0, sl, :] * jnp.exp(cum), bt=bf(bv * e_neg),
                        kt=bf(kv * e_neg), bh=bv * e_end, kh=kv * e_end, vv=vv, etot=jnp.exp(tot))
    tile = lambda name, c, p: tiles[c][name][:, p * LANES:(p + 1) * LANES]

    chains = [(c, p, h) for c, p in cps for h in range(2)]
    mbk = {}
    for c, p, h in chains:
        sel = first if h == 0 else jnp.logical_not(first)
        at, rt = tile("at", c, p), tile("rt", c, p)
        zero = jnp.zeros_like(at)
        ar = jnp.concatenate([jnp.where(sel, at, zero), jnp.where(sel, rt, zero)], axis=0).astype(BF16)
        mbk[c, p, h] = (_nt(ar, tile("bt", c, p)), _nt(ar, tile("kt", c, p)))
    m_ab, p_rb, m_ak, p_rk, tm = {}, {}, {}, {}, {}
    for ch in chains:
        mb, mk = mbk[ch]
        m_ab[ch] = jnp.where(strict, mb[:c_len], 0.0)
        p_rb[ch] = bf(jnp.where(incl, mb[c_len:], 0.0))
        m_ak[ch] = bf(jnp.where(strict, mk[:c_len], 0.0))
        p_rk[ch] = bf(jnp.where(incl, mk[c_len:], 0.0))
        tm[ch] = jnp.where(row == col, 1.0, 0.0) + jnp.where((row // 2) == (col // 2), m_ab[ch], 0.0)
    s = 2
    while s < c_len:
        off = jnp.logical_and((row // (2 * s)) == (col // (2 * s)), (row // s) != (col // s))
        half = {ch: bf(_mm(bf(tm[ch]), bf(jnp.where(off, m_ab[ch], 0.0)))) for ch in chains}
        tm = {ch: tm[ch] + _mm(half[ch], bf(tm[ch])) for ch in chains}
        s *= 2
    t16 = {ch: bf(tm[ch]) for ch in chains}
    mv = {(c, p, h): _mm(m_ak[c, p, h], bf(tile("vv", c, p))) for c, p, h in chains}
    w1 = {(c, p, h): _mm(t16[c, p, h], bf(tile("at", c, p))) for c, p, h in chains}
    w2 = {ch: _mm(t16[ch], bf(mv[ch])) for ch in chains}
    qc = {(c, p, h): tile("rt", c, p) + _mm(p_rb[c, p, h], bf(w1[c, p, h])) for c, p, h in chains}
    y1 = {(c, p, h): _mm(p_rb[c, p, h], bf(w2[c, p, h])) + _mm(p_rk[c, p, h], bf(tile("vv", c, p)))
          for c, p, h in chains}
    both = lambda d, c, p: jnp.where(first, d[c, p, 0], d[c, p, 1])
    ac_t, dc_t, qcs, y1s = {}, {}, {}, {}
    for c, p in cps:
        bh, kh, vv = tile("bh", c, p), tile("kh", c, p), tile("vv", c, p)
        w1p, w2p = both(w1, c, p), both(w2, c, p)
        a_full = jnp.where(same_head, _tn(bf(w1p), bf(bh)), 0.0) + jnp.where(eye, tile("etot", c, p), 0.0)
        ac_t[c, p] = _split2(a_full)
        dc_t[c, p] = jnp.where(same_head, _tn(bf(jnp.concatenate([w2p, vv], axis=0)),
                                              bf(jnp.concatenate([bh, kh], axis=0))), 0.0)
        qcs[c, p], y1s[c, p] = bf(both(qc, c, p)), both(y1, c, p)
    for p in range(n_pairs):
        state = s_scr[p]
        for c in range(n_chunks):
            rs, ls = sls[c, p]
            s_hi, s_lo = _split2(state)
            a_hi, a_lo = ac_t[c, p]
            y_ref[0, rs, ls] = _nt(qcs[c, p], s_hi) + y1s[c, p]
            state = _mm(s_hi, a_hi) + _mm(s_hi, a_lo) + _mm(s_lo, a_hi) + dc_t[c, p]
        s_scr[p] = state

    @pl.when(t == pl.num_programs(2) - 1)
    def _():
        sout_ref[0] = s_scr[...]


def _rwkv_scan(r, wl, k, v, av, bv, s0_bd, tri, *, tc, pairs_per_step):
    b, t, width = r.shape
    n_pairs = width // LANES
    tc = min(tc, t)
    npb = pairs_per_step
    seq = pl.BlockSpec((1, tc, npb * LANES), lambda bi, p, ti: (bi, ti, p))
    state = pl.BlockSpec((1, npb, LANES, LANES), lambda bi, p, ti: (bi, p, 0, 0))
    return pl.pallas_call(
        functools.partial(_scan_body, n_chunks=tc // RW_CHUNK, n_pairs=npb),
        out_shape=[jax.ShapeDtypeStruct((b, t, width), F32), jax.ShapeDtypeStruct(s0_bd.shape, F32)],
        grid=(b, n_pairs // npb, t // tc),
        in_specs=[seq] * 6 + [state, pl.BlockSpec(tri.shape, lambda bi, p, ti: (0, 0))],
        out_specs=[seq, state],
        scratch_shapes=[pltpu.VMEM((npb, LANES, LANES), F32)],
        **_call_opts("rwkv_scan", "parallel", "parallel", "arbitrary"),
    )(r, wl, k, v, av, bv, s0_bd, tri)


def _out_body(h_ref, osb_ref, y_ref, r_ref, k_ref, v_ref, g_ref, gs_ref, gr_ref, lnw_ref, lnb_ref, rk_ref, member_ref,
              wso_ref, wro_ref, wout_ref, o_ref, m_ref):
    j = pl.program_id(1)

    @pl.when(j == 0)
    def _():
        member = member_ref[...]
        y = y_ref[...]
        mu = _head_sum2(y, member) * (1.0 / HEAD_DIM)
        d = y - mu
        var = _head_sum2(d * d, member) * (1.0 / HEAD_DIM)
        yn = d * lax.rsqrt(var + GN_EPS) * lnw_ref[...] + lnb_ref[...]
        bonus = _head_sum2(r_ref[...] * k_ref[...] * rk_ref[...], member) * v_ref[...]
        yy = ((yn + bonus) * g_ref[...]).astype(BF16)
        o_sb = _mm(osb_ref[...].astype(BF16), wso_ref[...])
        o_rw = _mm(yy, wro_ref[...])
        merged = jax.nn.sigmoid(gs_ref[...]) * o_sb + jax.nn.sigmoid(gr_ref[...]) * o_rw
        m_ref[...] = merged.astype(BF16)

    o_ref[...] = h_ref[...] + _mm(m_ref[...], wout_ref[...])


def _merge_out(h, o_sb, y, r, k, v, g, p2, lnw, lnb, rk, member, wso, wro, wout, *, gs_blk, gr_blk, tm=128, tn=2048):
    n, d = h.shape
    width = o_sb.shape[1]
    tm, tn = min(tm, n), min(tn, d)
    tok = pl.BlockSpec((tm, width), lambda i, j: (i, 0))
    const = lambda shape: pl.BlockSpec(shape, lambda i, j: (0,) * len(shape))
    return pl.pallas_call(
        _out_body,
        out_shape=jax.ShapeDtypeStruct((n, d), F32),
        grid=(n // tm, d // tn),
        in_specs=[
            pl.BlockSpec((tm, tn), lambda i, j: (i, j)),
            tok, tok, tok, tok, tok, tok,
            pl.BlockSpec((tm, d), lambda i, j: (i, gs_blk)),
            pl.BlockSpec((tm, d), lambda i, j: (i, gr_blk)),
            const((1, width)), const((1, width)), const((1, width)),
            const((width, LANES)),
            const((width, d)), const((width, d)),
            pl.BlockSpec((d, tn), lambda i, j: (0, j)),
        ],
        out_specs=pl.BlockSpec((tm, tn), lambda i, j: (i, j)),
        scratch_shapes=[pltpu.VMEM((tm, d), BF16)],
        **_call_opts("merge_out", "parallel", "arbitrary"),
    )(h, o_sb, y, r, k, v, g, p2, p2, lnw, lnb, rk, member, wso, wro, wout)


def _layer(x, past_k, past_v, wkv0, shift0, w, *, tq, tc, scan_pairs):
    b, t, d = x.shape
    n = b * t
    width = w["w0"].shape[1]
    h1 = _ffn(x.reshape(n, d), w["ffn1_norm"], w["ffn1_wg"], w["ffn1_wu"], w["ffn1_wd"])
    p2 = _mix(h1, w["mix_norm"], w["w_in"], w["head_gain"], w["bd"], width=width)
    p3 = p2.reshape(b, t, -1)
    n_pairs = width // LANES
    if past_k is None:
        o_sb = _attention(p3, w["u2"], n_pairs=n_pairs, q_blk0=0, k_blk0=n_pairs, v_blk0=2 * n_pairs, tq=tq)
    else:
        o_sb = _attention_cached(p3, past_k, past_v, w["ue"], width=width)

    lora_w = w["mu_lora"].shape[1]
    lora_blk = (6 * width + 2 * d) // lora_w
    lora_cols = w["lora_cols"]
    s_rkv = shift0[:, :, :3 * width]
    s_lora = jnp.pad(shift0[:, :, 3 * width:], ((0, 0), (0, 0), (0, lora_w - lora_cols)))
    r, wl, k, v, av, bv, g = _rwkv_prep(
        p3, s_rkv, s_lora, w["mu_rkv"], w["mu_lora"], w["w0"], w["a0"], w["k_k"], w["k_a"],
        w["ww2"], w["wa2"], w["wg2"], w["member"], rkv_blk=1, lora_blk=lora_blk, tc=tc)

    s0 = wkv0.reshape(b, n_pairs, 2, HEAD_DIM, HEAD_DIM)
    z = jnp.zeros_like(s0[:, :, 0])
    s0_bd = jnp.concatenate([jnp.concatenate([s0[:, :, 0], z], axis=-1),
                             jnp.concatenate([z, s0[:, :, 1]], axis=-1)], axis=-2)
    y, s_bd = _rwkv_scan(r, wl, k, v, av, bv, s0_bd, w["tri"], tc=tc, pairs_per_step=scan_pairs)
    wkv = jnp.stack([s_bd[:, :, :HEAD_DIM, :HEAD_DIM], s_bd[:, :, HEAD_DIM:, HEAD_DIM:]], axis=2)
    wkv = wkv.reshape(b, 2 * n_pairs, HEAD_DIM, HEAD_DIM)

    flat = lambda a: a.reshape(n, width)
    gs_blk = (6 * width) // d
    h2 = _merge_out(h1, flat(o_sb), flat(y), flat(r), flat(k), flat(v), flat(g), p2,
                    w["ln_w"], w["ln_b"], w["r_k"], w["member"], w["sb_wo"], w["rw_wo"], w["w_out"],
                    gs_blk=gs_blk, gr_blk=gs_blk + 1)
    out = _ffn(h2, w["ffn2_norm"], w["ffn2_wg"], w["ffn2_wu"], w["ffn2_wd"])

    heads = width // HEAD_DIM
    k_new = p3[:, :, width:2 * width].reshape(b, t, heads, HEAD_DIM)
    v_new = p3[:, :, 2 * width:3 * width].reshape(b, t, heads, HEAD_DIM)
    shift = jnp.concatenate([p3[:, t - 1:, 3 * width:6 * width],
                             p3[:, t - 1:, 6 * width + 2 * d:6 * width + 2 * d + lora_cols]], axis=-1)
    return out.reshape(b, t, d), k_new, v_new, wkv, shift


def _layer_weights(l, ffn1_norm, ffn1_w_gate, ffn1_w_up, ffn1_w_down, mix_norm, w_in, sb_q_norm, sb_k_norm, sb_w_o,
                   rwkv_mu, rwkv_w0, rwkv_w_w2, rwkv_a0, rwkv_w_a2, rwkv_w_g2, rwkv_k_k, rwkv_k_a, rwkv_r_k,
                   rwkv_ln_w, rwkv_ln_b, rwkv_w_o, w_out, ffn2_norm, ffn2_w_gate, ffn2_w_up, ffn2_w_down):
    d = w_in.shape[1]
    width = rwkv_w0.shape[1]
    heads = width // HEAD_DIM
    n_decay, n_iclr, n_gate = rwkv_w_w2.shape[1], rwkv_w_a2.shape[1], rwkv_w_g2.shape[1]
    lora_cols = n_decay + n_iclr + n_gate
    lora_w = -(-lora_cols // 512) * 512
    row = lambda a: a.reshape(1, -1).astype(F32)
    wi = w_in[l]
    w_in_p = jnp.concatenate([
        wi[:, :6 * width], wi[:, 6 * width + lora_cols:], wi[:, 6 * width:6 * width + lora_cols],
        jnp.zeros((d, lora_w - lora_cols), wi.dtype)], axis=1).astype(BF16)
    total = w_in_p.shape[1]
    head_gain = jnp.concatenate([jnp.tile(sb_q_norm[l], heads), jnp.tile(sb_k_norm[l], heads),
                                 jnp.ones((total - 2 * width,), F32)]).reshape(1, total)
    mu = rwkv_mu[l]

    def lora_pad(wm, r0):
        return jnp.zeros((lora_w, width), F32).at[r0:r0 + wm.shape[0]].set(wm).astype(BF16)

    hid = jnp.arange(width) // HEAD_DIM
    bid = jnp.arange(512) // HEAD_DIM
    bd = (bid[:, None] == bid[None, :]).astype(BF16)
    member = (hid[:, None] == jnp.arange(LANES)[None, :]).astype(BF16)
    tk = ATT_TK
    ki = jnp.arange(tk)
    u2 = jnp.concatenate([(ki[None, :] > ki[:, None]).astype(BF16), jnp.ones((16, tk), BF16)], axis=0)
    ue = jnp.concatenate([(ki[:, None] > ki[None, :]).astype(BF16), jnp.ones((tk, LANES), BF16)], axis=1)
    ci = jnp.arange(RW_CHUNK)
    tri = (ci[:, None] >= ci[None, :]).astype(BF16)
    return {
        "ffn1_norm": row(ffn1_norm[l]), "ffn1_wg": ffn1_w_gate[l].astype(BF16), "ffn1_wu": ffn1_w_up[l].astype(BF16),
        "ffn1_wd": ffn1_w_down[l].astype(BF16),
        "ffn2_norm": row(ffn2_norm[l]), "ffn2_wg": ffn2_w_gate[l].astype(BF16), "ffn2_wu": ffn2_w_up[l].astype(BF16),
        "ffn2_wd": ffn2_w_down[l].astype(BF16),
        "mix_norm": row(mix_norm[l]), "w_in": w_in_p, "head_gain": head_gain, "bd": bd, "member": member, "u2": u2, "ue": ue, "tri": tri,
        "mu_rkv": row(mu[:3 * width]), "mu_lora": row(jnp.pad(mu[3 * width:], (0, lora_w - lora_cols))),
        "lora_cols": lora_cols,
        "w0": row(rwkv_w0[l]), "a0": row(rwkv_a0[l]), "k_k": row(rwkv_k_k[l]), "k_a": row(rwkv_k_a[l]),
        "ww2": lora_pad(rwkv_w_w2[l], 0), "wa2": lora_pad(rwkv_w_a2[l], n_decay),
        "wg2": lora_pad(rwkv_w_g2[l], n_decay + n_iclr),
        "ln_w": row(rwkv_ln_w[l]), "ln_b": row(rwkv_ln_b[l]), "r_k": row(rwkv_r_k[l]),
        "sb_wo": sb_w_o[l].astype(BF16), "rw_wo": rwkv_w_o[l].astype(BF16), "w_out": w_out[l].astype(BF16),
    }


def kernel(x_prompt, x_sample, cache_sb_k, cache_sb_v, state_rwkv_wkv, state_rwkv_shift, ffn1_norm, ffn1_w_gate, ffn1_w_up, ffn1_w_down, mix_norm, w_in, sb_q_norm, sb_k_norm, sb_w_o, rwkv_mu, rwkv_w0, rwkv_w_w2, rwkv_a0, rwkv_w_a2, rwkv_w_g2, rwkv_k_k, rwkv_k_a, rwkv_r_k, rwkv_ln_w, rwkv_ln_b, rwkv_w_o, w_out, ffn2_norm, ffn2_w_gate, ffn2_w_up, ffn2_w_down):
    depth = w_in.shape[0]
    yp, ys = x_prompt, x_sample
    bp = x_prompt.shape[0]
    width = rwkv_w0.shape[1]
    heads = width // HEAD_DIM
    rw_cols = state_rwkv_shift.shape[-1]
    outs = [[] for _ in range(8)]
    for l in range(depth):
        w = _layer_weights(l, ffn1_norm, ffn1_w_gate, ffn1_w_up, ffn1_w_down, mix_norm, w_in, sb_q_norm, sb_k_norm,
                           sb_w_o, rwkv_mu, rwkv_w0, rwkv_w_w2, rwkv_a0, rwkv_w_a2, rwkv_w_g2, rwkv_k_k, rwkv_k_a,
                           rwkv_r_k, rwkv_ln_w, rwkv_ln_b, rwkv_w_o, w_out, ffn2_norm, ffn2_w_gate, ffn2_w_up,
                           ffn2_w_down)
        wkv_zero = jnp.zeros((bp, heads, HEAD_DIM, HEAD_DIM), F32)
        shift_zero = jnp.zeros((bp, 1, rw_cols), F32)
        yp, kp, vp, wkvp, shp = _layer(yp, None, None, wkv_zero, shift_zero, w, tq=512, tc=256, scan_pairs=4)
        ys, kn, vn, wkvn, shn = _layer(ys, cache_sb_k[l], cache_sb_v[l], state_rwkv_wkv[l], state_rwkv_shift[l], w, tq=64, tc=64, scan_pairs=8)
        for lst, val in zip(outs, (kp, vp, wkvp, shp, kn, vn, wkvn, shn)):
            lst.append(val)
    return (yp, ys) + tuple(jnp.stack(o) for o in outs)
```

```python
import functools

import jax
import jax.numpy as jnp
from jax import lax
from jax.experimental import pallas as pl
from jax.experimental.pallas import tpu as pltpu

F32 = jnp.float32
BF16 = jnp.bfloat16

HEAD_DIM = 64
LANES = 128
NORM_EPS = 1e-6
GN_EPS = 64e-5
RW_CHUNK = 64
ATT_TK = 256

MIB = 1024 * 1024
LOG2_E = 1.4426950408889634
SIGN_BIT = 0x80000000
ATT_CHAIN_LANES = 256
ATT_SKEW = 2


def _nt(x, y):
    return lax.dot_general(x, y, (((1,), (1,)), ((), ())), preferred_element_type=F32)


def _tn(x, y):
    return lax.dot_general(x, y, (((0,), (0,)), ((), ())), preferred_element_type=F32)


def _mm(x, y):
    return jnp.dot(x, y, preferred_element_type=F32)


def _neg_abs(x):
    return lax.bitcast_convert_type(lax.bitcast_convert_type(x, jnp.uint32) | jnp.uint32(SIGN_BIT), F32)


def _split2(x):
    hi = x.astype(BF16)
    lo = (x - hi.astype(F32)).astype(BF16)
    return hi, lo


def _head_sum2(x, member):
    hi, lo = _split2(x)
    shi, slo = _split2(_mm(hi, member) + _mm(lo, member))
    return _nt(shi, member) + _nt(slo, member)


def _rms(x, g):
    ms = jnp.mean(x * x, axis=-1, keepdims=True)
    return x * lax.rsqrt(ms + NORM_EPS) * g


VMEM_LIMIT_MIB = {"ffn": 48, "mix": 48, "sb_attention": 48, "sb_attention_cached": 48, "rwkv_prep": 48,
                  "rwkv_scan": 32, "merge_out": 56}


def _call_opts(name, *sem):
    return dict(name=name, compiler_params=pltpu.CompilerParams(
        dimension_semantics=sem, vmem_limit_bytes=VMEM_LIMIT_MIB[name] * MIB))


def _ffn_body(x_ref, g_ref, wg_ref, wu_ref, wd_ref, o_ref, n_ref, acc_ref):
    f = pl.program_id(1)

    @pl.when(f == 0)
    def _():
        n_ref[...] = _rms(x_ref[...], g_ref[...]).astype(BF16)
        acc_ref[...] = jnp.zeros_like(acc_ref)

    n = n_ref[...]
    g = _mm(n, wg_ref[...])
    u = _mm(n, wu_ref[...])
    a = (g * jax.nn.sigmoid(g) * u).astype(BF16)
    acc_ref[...] += _mm(a, wd_ref[...])

    @pl.when(f == pl.num_programs(1) - 1)
    def _():
        o_ref[...] = x_ref[...] + 0.5 * acc_ref[...]


def _ffn(x, g, wg, wu, wd, *, tm=512, tf=512):
    n, d = x.shape
    ff = wg.shape[1]
    tm, tf = min(tm, n), min(tf, ff)
    return pl.pallas_call(
        _ffn_body,
        out_shape=jax.ShapeDtypeStruct((n, d), F32),
        grid=(n // tm, ff // tf),
        in_specs=[
            pl.BlockSpec((tm, d), lambda i, f: (i, 0)),
            pl.BlockSpec((1, d), lambda i, f: (0, 0)),
            pl.BlockSpec((d, tf), lambda i, f: (0, f)),
            pl.BlockSpec((d, tf), lambda i, f: (0, f)),
            pl.BlockSpec((tf, d), lambda i, f: (f, 0)),
        ],
        out_specs=pl.BlockSpec((tm, d), lambda i, f: (i, 0)),
        scratch_shapes=[pltpu.VMEM((tm, d), BF16), pltpu.VMEM((tm, d), F32)],
        **_call_opts("ffn", "parallel", "arbitrary"),
    )(x, g, wg, wu, wd)


def _mix_body(h_ref, g_ref, w_ref, hg_ref, bd_ref, o_ref, n_ref, *, norm_tiles):
    j = pl.program_id(1)

    @pl.when(j == 0)
    def _():
        n_ref[...] = _rms(h_ref[...], g_ref[...]).astype(BF16)

    p = _mm(n_ref[...], w_ref[...])

    @pl.when(j < norm_tiles)
    def _():
        hi, lo = _split2(p * p)
        bd = bd_ref[...]
        ms = (_mm(hi, bd) + _mm(lo, bd)) * (1.0 / HEAD_DIM)
        o_ref[...] = p * lax.rsqrt(ms + NORM_EPS) * hg_ref[...]

    @pl.when(j >= norm_tiles)
    def _():
        o_ref[...] = p


def _mix(h, g, w, hgain, bd, *, width, tm=1024, tn=512):
    n, d = h.shape
    cols = w.shape[1]
    tm = min(tm, n)
    return pl.pallas_call(
        functools.partial(_mix_body, norm_tiles=2 * width // tn),
        out_shape=jax.ShapeDtypeStruct((n, cols), F32),
        grid=(n // tm, cols // tn),
        in_specs=[
            pl.BlockSpec((tm, d), lambda i, j: (i, 0)),
            pl.BlockSpec((1, d), lambda i, j: (0, 0)),
            pl.BlockSpec((d, tn), lambda i, j: (0, j)),
            pl.BlockSpec((1, tn), lambda i, j: (0, j)),
            pl.BlockSpec((tn, tn), lambda i, j: (0, 0)),
        ],
        out_specs=pl.BlockSpec((tm, tn), lambda i, j: (i, j)),
        scratch_shapes=[pltpu.VMEM((tm, d), BF16)],
        **_call_opts("mix", "parallel", "arbitrary"),
    )(h, g, w, hgain, bd)


def _sb_items(items, ut, tk, run, pv):
    zs, parts = {}, {}

    def scores(n):
        _, q16, kblk, _, _ = items[n]
        zs[n] = _nt(kblk, q16)

    def keep_sums(n):
        mask = items[n][4]
        z = zs.pop(n)
        lp = jnp.log(1.0 + jnp.exp2(_neg_abs(z))) * LOG2_E
        log_beta = jnp.minimum(z, 0.0) - lp
        log_keep = log_beta - z
        if mask is not None:
            log_keep = jnp.where(mask, log_keep, 0.0)
        parts[n] = (log_beta, _mm(ut, log_keep.astype(BF16)))

    def weigh(n):
        ci, _, _, vt_rows, mask = items[n]
        log_beta, ext = parts.pop(n)
        w = jnp.exp2(log_beta + (ext[:tk] + run[ci][0:1]))
        if mask is not None:
            w = jnp.where(mask, w, 0.0)
        d = _mm(vt_rows, w.astype(BF16))
        pv[ci] = d if ci not in pv else pv[ci] + d
        run[ci] = run[ci] + ext[tk:tk + 8]

    for step in range(len(items) + 2 * ATT_SKEW):
        if step < len(items):
            scores(step)
        if 0 <= step - ATT_SKEW < len(items):
            keep_sums(step - ATT_SKEW)
        if 0 <= step - 2 * ATT_SKEW < len(items):
            weigh(step - 2 * ATT_SKEW)


def _sb_rows(items, ue, tk, run, pv):
    zs, parts = {}, {}

    def scores(n):
        _, qh, kth, _, _ = items[n]
        zs[n] = jnp.concatenate([_mm(q, kt) for q, kt in zip(qh, kth)], axis=0)

    def keep_sums(n):
        mask = items[n][4]
        z = zs.pop(n)
        lp = jnp.log(1.0 + jnp.exp2(_neg_abs(z))) * LOG2_E
        log_beta = jnp.minimum(z, 0.0) - lp
        log_keep = log_beta - z
        if mask is not None:
            log_keep = jnp.where(mask, log_keep, 0.0)
        parts[n] = (log_beta, _mm(log_keep.astype(BF16), ue))

    def weigh(n):
        ci, qh, _, vth, mask = items[n]
        log_beta, ext = parts.pop(n)
        tail = jnp.concatenate([ext[:, c0:c0 + LANES] + run[ci] for c0 in range(0, tk, LANES)], axis=1)
        w = jnp.exp2(log_beta + tail)
        if mask is not None:
            w = jnp.where(mask, w, 0.0)
        w = w.astype(BF16)
        tq = qh[0].shape[0]
        for h, vt in enumerate(vth):
            d = _nt(w[h * tq:(h + 1) * tq], vt)
            pv[ci, h] = d if (ci, h) not in pv else pv[ci, h] + d
        run[ci] = run[ci] + ext[:, tk:]

    for step in range(len(items) + 2 * ATT_SKEW):
        if step < len(items):
            scores(step)
        if 0 <= step - ATT_SKEW < len(items):
            keep_sums(step - ATT_SKEW)
        if 0 <= step - 2 * ATT_SKEW < len(items):
            weigh(step - 2 * ATT_SKEW)


def _attn_cached_body(q_ref, k_ref, v_ref, ckt_ref, cvt_ref, ue_ref, o_ref, qs, carry, acc, *,
                      tq, tk, n_heads, chunk_blocks, per_chain):
    j = pl.program_id(1)
    n_chains = n_heads // per_chain
    ue = ue_ref[...]
    heads_of = lambda c: range(c * per_chain, (c + 1) * per_chain)

    def run_items(items):
        run = {c: carry[c] for c in range(n_chains)}
        pv = {}
        _sb_rows(items, ue, tk, run, pv)
        for c in range(n_chains):
            carry[c] = run[c]
            for i, h in enumerate(heads_of(c)):
                acc[h] += pv[c, i]

    @pl.when(j == 0)
    def _():
        carry[...] = jnp.zeros_like(carry)
        acc[...] = jnp.zeros_like(acc)
        for h in range(n_heads):
            qs[h] = (q_ref[0, :, h * HEAD_DIM:(h + 1) * HEAD_DIM] * (HEAD_DIM ** -0.5 * LOG2_E)).astype(BF16)
        row = lax.broadcasted_iota(jnp.int32, (per_chain * tq, tk), 0) % tq
        col = lax.broadcasted_iota(jnp.int32, (per_chain * tq, tk), 1)
        pad = jnp.zeros((HEAD_DIM, tk - tq), F32)
        items = []
        for c in range(n_chains):
            def new_t(ref):
                return [jnp.concatenate([ref[0, :, h * HEAD_DIM:(h + 1) * HEAD_DIM].T, pad], axis=1).astype(BF16)
                        for h in heads_of(c)]
            items.append((c, [qs[h] for h in heads_of(c)], new_t(k_ref), new_t(v_ref), col < row))
        run_items(items)

    items = []
    for m in reversed(range(chunk_blocks)):
        keys = slice(m * tk, (m + 1) * tk)
        for c in range(n_chains):
            items.append((c, [qs[h] for h in heads_of(c)],
                          [ckt_ref[0, h, :, keys].astype(BF16) for h in heads_of(c)],
                          [cvt_ref[0, h, :, keys].astype(BF16) for h in heads_of(c)], None))
    run_items(items)

    @pl.when(j == pl.num_programs(1) - 1)
    def _():
        o_ref[0] = jnp.concatenate([acc[h] for h in range(n_heads)], axis=1)


def _attention_cached(p3, cache_k, cache_v, ue, *, width, chunk_keys=1024, per_chain=4):
    b, t, _ = p3.shape
    tk = ATT_TK
    _, p_len, n_heads, _ = cache_k.shape
    chunk_keys = min(chunk_keys, p_len)
    n_chunks = p_len // chunk_keys
    ckt = jnp.transpose(cache_k, (0, 2, 3, 1))
    cvt = jnp.transpose(cache_v, (0, 2, 3, 1))
    cache_spec = pl.BlockSpec((1, n_heads, HEAD_DIM, chunk_keys), lambda bi, j: (bi, 0, 0, n_chunks - 1 - j))
    return pl.pallas_call(
        functools.partial(_attn_cached_body, tq=t, tk=tk, n_heads=n_heads, chunk_blocks=chunk_keys // tk,
                          per_chain=per_chain),
        out_shape=jax.ShapeDtypeStruct((b, t, width), F32),
        grid=(b, n_chunks),
        in_specs=[
            pl.BlockSpec((1, t, width), lambda bi, j: (bi, 0, 0)),
            pl.BlockSpec((1, t, width), lambda bi, j: (bi, 0, 1)),
            pl.BlockSpec((1, t, width), lambda bi, j: (bi, 0, 2)),
            cache_spec, cache_spec,
            pl.BlockSpec(ue.shape, lambda bi, j: (0, 0)),
        ],
        out_specs=pl.BlockSpec((1, t, width), lambda bi, j: (bi, 0, 0)),
        scratch_shapes=[
            pltpu.VMEM((n_heads, t, HEAD_DIM), BF16),
            pltpu.VMEM((n_heads // per_chain, per_chain * t, LANES), F32),
            pltpu.VMEM((n_heads, t, HEAD_DIM), F32),
        ],
        **_call_opts("sb_attention_cached", "parallel", "arbitrary"),
    )(p3, p3, p3, ckt, cvt, ue)


def _attn_body(q_ref, k_ref, v_ref, ut_ref, o_ref, kb, vt, qs, carry, acc, *, tq, tk, n_blocks, new_unroll, chains):
    i = pl.program_id(2)

    @pl.when(i == 0)
    def _():
        def fill(blk, _):
            rows = pl.ds(pl.multiple_of(blk * tk, tk), tk)
            kb[rows, :] = k_ref[0, rows, :].astype(BF16)
            vt[blk] = v_ref[0, rows, :].T.astype(BF16)
            return 0

        lax.fori_loop(0, n_blocks, fill, 0)

    q = q_ref[0] * (HEAD_DIM ** -0.5 * LOG2_E)
    n_heads = q.shape[1] // HEAD_DIM
    head_of_lane = lax.broadcasted_iota(jnp.int32, q.shape, 1) // HEAD_DIM
    zero = jnp.zeros_like(q)
    for h in range(n_heads):
        qs[h * tq:(h + 1) * tq, :] = jnp.where(head_of_lane == h, q, zero).astype(BF16)
    carry[...] = jnp.zeros_like(carry)
    acc[...] = jnp.zeros_like(acc)

    def span(blocks):
        q16 = [qs[c0:c1, :] for c0, c1, _, _ in chains]
        cw = chains[0][1] - chains[0][0]
        row = lax.broadcasted_iota(jnp.int32, (tk, cw), 0)
        lane = lax.broadcasted_iota(jnp.int32, (tk, cw), 1)
        mask = lambda off, c0: None if off is None else (row + off) < (lane + c0) % tq
        items = [(ci, q16[ci], kblk, vtblk[r0:r1], mask(off, c0))
                 for kblk, vtblk, off in blocks for ci, (c0, _, r0, r1) in enumerate(chains)]
        run = {ci: carry[:, c0:c1] for ci, (c0, c1, _, _) in enumerate(chains)}
        pv = {}
        _sb_items(items, ut_ref[...], tk, run, pv)
        for ci, (c0, c1, r0, r1) in enumerate(chains):
            carry[:, c0:c1] = run[ci]
            acc[r0:r1, c0:c1] += pv[ci]

    q0 = i * tq
    n_diag = tq // tk
    diag = []
    for m in reversed(range(n_diag)):
        blk = q0 // tk + m
        diag.append((kb[pl.ds(pl.multiple_of(blk * tk, tk), tk), :], vt[blk], m * tk))
    span(diag)

    def new_blocks(last, count):
        blocks = []
        for m in range(count):
            blk = last - m
            blocks.append((kb[pl.ds(pl.multiple_of(blk * tk, tk), tk), :], vt[blk], None))
        span(blocks)

    last = q0 // tk - 1
    per_trip = new_unroll
    while per_trip >= n_diag:
        def step(it, _, last=last, per_trip=per_trip):
            new_blocks(last - it * per_trip, per_trip)
            return 0

        trips = (last + 1) // per_trip
        lax.fori_loop(0, trips, step, 0)
        last = last - trips * per_trip
        per_trip //= 2

    out_t = acc[...].T
    out = out_t[:tq]
    for h in range(1, n_heads):
        out = jnp.where(head_of_lane == h, out_t[h * tq:(h + 1) * tq], out)
    o_ref[0] = out


def _attention(p3, ut, *, n_pairs, q_blk0, k_blk0, v_blk0, tq):
    b, t, _ = p3.shape
    tk = ATT_TK
    tq = min(tq, t)
    assert tq % tk == 0 and t % tq == 0, (t, tq, tk)
    cw = min(tq, ATT_CHAIN_LANES)
    chains = tuple((h * tq + c0, h * tq + c0 + cw, h * HEAD_DIM, (h + 1) * HEAD_DIM)
                   for h in range(2) for c0 in range(0, tq, cw))
    seq = lambda blk0: pl.BlockSpec((1, t, LANES), lambda bi, p, i: (bi, 0, blk0 + p))
    return pl.pallas_call(
        functools.partial(_attn_body, tq=tq, tk=tk, n_blocks=t // tk, new_unroll=4 * (tq // tk), chains=chains),
        out_shape=jax.ShapeDtypeStruct((b, t, n_pairs * LANES), F32),
        grid=(b, n_pairs, t // tq),
        in_specs=[
            pl.BlockSpec((1, tq, LANES), lambda bi, p, i: (bi, i, q_blk0 + p)),
            seq(k_blk0), seq(v_blk0),
            pl.BlockSpec(ut.shape, lambda bi, p, i: (0, 0)),
        ],
        out_specs=pl.BlockSpec((1, tq, LANES), lambda bi, p, i: (bi, i, p)),
        scratch_shapes=[
            pltpu.VMEM((t, LANES), BF16),
            pltpu.VMEM((t // tk, LANES, tk), BF16),
            pltpu.VMEM((2 * tq, LANES), BF16),
            pltpu.VMEM((8, 2 * tq), F32),
            pltpu.VMEM((LANES, 2 * tq), F32),
        ],
        **_call_opts("sb_attention", "parallel", "parallel", "arbitrary"),
    )(p3, p3, p3, ut)


def _prep_body(prkv_ref, plora_ref, s_rkv_ref, s_lora_ref, mu_rkv_ref, mu_lora_ref, w0_ref, a0_ref, kk_ref, ka_ref,
               ww2_ref, wa2_ref, wg2_ref, member_ref,
               r_o, wl_o, k_o, v_o, av_o, bv_o, g_o, c_rkv, c_lora, *, tc, width):
    t = pl.program_id(1)

    @pl.when(t == 0)
    def _():
        c_rkv[0:1, :] = s_rkv_ref[0]
        c_lora[0:1, :] = s_lora_ref[0]

    def token_mix(p, prev, mu):
        row = lax.broadcasted_iota(jnp.int32, p.shape, 0)
        shifted = jnp.where(row == 0, prev, pltpu.roll(p, 1, 0))
        return p + (shifted - p) * mu

    def rkv_seg(s):
        cs = slice(s * width, (s + 1) * width)
        p = prkv_ref[0, :, cs]
        x = token_mix(p, c_rkv[0:1, cs], mu_rkv_ref[:, cs])
        c_rkv[0:1, cs] = p[tc - 1:tc, :]
        return x

    pl_ = plora_ref[0]
    xl = token_mix(pl_, c_lora[0:1, :], mu_lora_ref[...])
    c_lora[0:1, :] = pl_[tc - 1:tc, :]

    r_o[0] = rkv_seg(0)
    v_o[0] = rkv_seg(2)
    xk = rkv_seg(1)

    dec = w0_ref[...] + _mm(jnp.tanh(xl).astype(BF16), ww2_ref[...])
    nd = -dec
    softplus = jnp.maximum(nd, 0.0) + jnp.log(1.0 + jnp.exp(-jnp.abs(nd)))
    w_log = -softplus - 0.5
    wl_o[0] = -jnp.exp(w_log)
    a = jax.nn.sigmoid(a0_ref[...] + _mm(xl.astype(BF16), wa2_ref[...]))
    g_o[0] = _mm(jax.nn.sigmoid(xl).astype(BF16), wg2_ref[...])
    kk = xk * kk_ref[...]
    k_o[0] = xk * (1.0 + (a - 1.0) * ka_ref[...])
    norm = jnp.sqrt(_head_sum2(kk * kk, member_ref[...]))
    kk = kk / jnp.maximum(norm, 1e-12)
    av_o[0] = -kk
    bv_o[0] = kk * a


def _rwkv_prep(p3, s_rkv, s_lora, mu_rkv, mu_lora, w0, a0, k_k, k_a, ww2, wa2, wg2, member, *, rkv_blk, lora_blk, tc):
    b, t, _ = p3.shape
    width = w0.shape[1]
    lw = mu_lora.shape[1]
    tc = min(tc, t)
    const = lambda shape: pl.BlockSpec(shape, lambda bi, ti: (0,) * len(shape))
    out_spec = pl.BlockSpec((1, tc, width), lambda bi, ti: (bi, ti, 0))
    return pl.pallas_call(
        functools.partial(_prep_body, tc=tc, width=width),
        out_shape=[jax.ShapeDtypeStruct((b, t, width), F32)] * 7,
        grid=(b, t // tc),
        in_specs=[
            pl.BlockSpec((1, tc, 3 * width), lambda bi, ti: (bi, ti, rkv_blk)),
            pl.BlockSpec((1, tc, lw), lambda bi, ti: (bi, ti, lora_blk)),
            pl.BlockSpec((1, 1, 3 * width), lambda bi, ti: (bi, 0, 0)),
            pl.BlockSpec((1, 1, lw), lambda bi, ti: (bi, 0, 0)),
            const((1, 3 * width)), const((1, lw)),
            const((1, width)), const((1, width)), const((1, width)), const((1, width)),
            const((lw, width)), const((lw, width)), const((lw, width)),
            const((width, LANES)),
        ],
        out_specs=[out_spec] * 7,
        scratch_shapes=[pltpu.VMEM((8, 3 * width), F32), pltpu.VMEM((8, lw), F32)],
        **_call_opts("rwkv_prep", "parallel", "arbitrary"),
    )(p3, p3, s_rkv, s_lora, mu_rkv, mu_lora, w0, a0, k_k, k_a, ww2, wa2, wg2, member)


def _scan_body(r_ref, wl_ref, k_ref, v_ref, a_ref, b_ref, s0_ref, tri_ref, y_ref, sout_ref, s_scr, *,
               n_chunks, n_pairs):
    c_len = RW_CHUNK
    t = pl.program_id(2)

    @pl.when(t == 0)
    def _():
        s_scr[...] = s0_ref[0]

    row = lax.broadcasted_iota(jnp.int32, (c_len, c_len), 0)
    col = lax.broadcasted_iota(jnp.int32, (c_len, c_len), 1)
    strict = row > col
    incl = row >= col
    lane = lax.broadcasted_iota(jnp.int32, (c_len, LANES), 1)
    first = lane < HEAD_DIM
    brow = lax.broadcasted_iota(jnp.int32, (LANES, LANES), 0)
    bcol = lax.broadcasted_iota(jnp.int32, (LANES, LANES), 1)
    same_head = (brow // HEAD_DIM) == (bcol // HEAD_DIM)
    eye = brow == bcol
    tri = tri_ref[...]
    bf = lambda x: x.astype(BF16)

    cps = [(c, p) for c in range(n_chunks) for p in range(n_pairs)]
    sls = {(c, p): (slice(c * c_len, (c + 1) * c_len), slice(p * LANES, (p + 1) * LANES)) for c, p in cps}
    tiles = {}
    for c in range(n_chunks):
        sl = slice(c * c_len, (c + 1) * c_len)
        wl = wl_ref[0, sl, :]
        hi = wl.astype(BF16)
        rem = wl - hi.astype(F32)
        mid = rem.astype(BF16)
        lo = (rem - mid.astype(F32)).astype(BF16)
        cum = _mm(tri, hi) + _mm(tri, mid) + _mm(tri, lo)
        tot = cum[c_len - 1:c_len, :]
        e_neg = jnp.exp(-cum)
        e_end = jnp.exp(tot - cum)
        av, bv, kv, vv = a_ref[0, sl, :], b_ref[0, sl, :], k_ref[0, sl, :], v_ref[0, sl, :]
        tiles[c] = dict(at=av * jnp.exp(cum - wl), rt=r_ref[0, sl, :] * jnp.exp(cum), bt=bf(bv * e_neg),
                        kt=bf(kv * e_neg), bh=bv * e_end, kh=kv * e_end, vv=vv, etot=jnp.exp(tot))
    tile = lambda name, c, p: tiles[c][name][:, p * LANES:(p + 1) * LANES]

    chains = [(c, p, h) for c, p in cps for h in range(2)]
    mbk = {}
    for c, p, h in chains:
        sel = first if h == 0 else jnp.logical_not(first)
        at, rt = tile("at", c, p), tile("rt", c, p)
        zero = jnp.zeros_like(at)
        ar = jnp.concatenate([jnp.where(sel, at, zero), jnp.where(sel, rt, zero)], axis=0).astype(BF16)
        mbk[c, p, h] = (_nt(ar, tile("bt", c, p)), _nt(ar, tile("kt", c, p)))
    m_ab, p_rb, m_ak, p_rk, tm = {}, {}, {}, {}, {}
    for ch in chains:
        mb, mk = mbk[ch]
        m_ab[ch] = jnp.where(strict, mb[:c_len], 0.0)
        p_rb[ch] = bf(jnp.where(incl, mb[c_len:], 0.0))
        m_ak[ch] = bf(jnp.where(strict, mk[:c_len], 0.0))
        p_rk[ch] = bf(jnp.where(incl, mk[c_len:], 0.0))
        tm[ch] = jnp.where(row == col, 1.0, 0.0) + jnp.where((row // 2) == (col // 2), m_ab[ch], 0.0)
    s = 2
    while s < c_len:
        off = jnp.logical_and((row // (2 * s)) == (col // (2 * s)), (row // s) != (col // s))
        half = {ch: bf(_mm(bf(tm[ch]), bf(jnp.where(off, m_ab[ch], 0.0)))) for ch in chains}
        tm = {ch: tm[ch] + _mm(half[ch], bf(tm[ch])) for ch in chains}
        s *= 2
    t16 = {ch: bf(tm[ch]) for ch in chains}
    mv = {(c, p, h): _mm(m_ak[c, p, h], bf(tile("vv", c, p))) for c, p, h in chains}
    w1 = {(c, p, h): _mm(t16[c, p, h], bf(tile("at", c, p))) for c, p, h in chains}
    w2 = {ch: _mm(t16[ch], bf(mv[ch])) for ch in chains}
    qc = {(c, p, h): tile("rt", c, p) + _mm(p_rb[c, p, h], bf(w1[c, p, h])) for c, p, h in chains}
    y1 = {(c, p, h): _mm(p_rb[c, p, h], bf(w2[c, p, h])) + _mm(p_rk[c, p, h], bf(tile("vv", c, p)))
          for c, p, h in chains}
    both = lambda d, c, p: jnp.where(first, d[c, p, 0], d[c, p, 1])
    ac_t, dc_t, qcs, y1s = {}, {}, {}, {}
    for c, p in cps:
        bh, kh, vv = tile("bh", c, p), tile("kh", c, p), tile("vv", c, p)
        w1p, w2p = both(w1, c, p), both(w2, c, p)
        a_full = jnp.where(same_head, _tn(bf(w1p), bf(bh)), 0.0) + jnp.where(eye, tile("etot", c, p), 0.0)
        ac_t[c, p] = _split2(a_full)
        dc_t[c, p] = jnp.where(same_head, _tn(bf(jnp.concatenate([w2p, vv], axis=0)),
                                              bf(jnp.concatenate([bh, kh], axis=0))), 0.0)
        qcs[c, p], y1s[c, p] = bf(both(qc, c, p)), both(y1, c, p)
    for p in range(n_pairs):
        state = s_scr[p]
        for c in range(n_chunks):
            rs, ls = sls[c, p]
            s_hi, s_lo = _split2(state)
            a_hi, a_lo = ac_t[c, p]
            y_ref[0, rs, ls] = _nt(qcs[c, p], s_hi) + y1s[c, p]
            state = _mm(s_hi, a_hi) + _mm(s_hi, a_lo) + _mm(s_lo, a_hi) + dc_t[c, p]
        s_scr[p] = state

    @pl.when(t == pl.num_programs(2) - 1)
    def _():
        sout_ref[0] = s_scr[...]


def _rwkv_scan(r, wl, k, v, av, bv, s0_bd, tri, *, tc, pairs_per_step):
    b, t, width = r.shape
    n_pairs = width // LANES
    tc = min(tc, t)
    npb = pairs_per_step
    seq = pl.BlockSpec((1, tc, npb * LANES), lambda bi, p, ti: (bi, ti, p))
    state = pl.BlockSpec((1, npb, LANES, LANES), lambda bi, p, ti: (bi, p, 0, 0))
    return pl.pallas_call(
        functools.partial(_scan_body, n_chunks=tc // RW_CHUNK, n_pairs=npb),
        out_shape=[jax.ShapeDtypeStruct((b, t, width), F32), jax.ShapeDtypeStruct(s0_bd.shape, F32)],
        grid=(b, n_pairs // npb, t // tc),
        in_specs=[seq] * 6 + [state, pl.BlockSpec(tri.shape, lambda bi, p, ti: (0, 0))],
        out_specs=[seq, state],
        scratch_shapes=[pltpu.VMEM((npb, LANES, LANES), F32)],
        **_call_opts("rwkv_scan", "parallel", "parallel", "arbitrary"),
    )(r, wl, k, v, av, bv, s0_bd, tri)


def _out_body(h_ref, osb_ref, y_ref, r_ref, k_ref, v_ref, g_ref, gs_ref, gr_ref, lnw_ref, lnb_ref, rk_ref, member_ref,
              wso_ref, wro_ref, wout_ref, o_ref, m_ref):
    j = pl.program_id(1)

    @pl.when(j == 0)
    def _():
        member = member_ref[...]
        y = y_ref[...]
        mu = _head_sum2(y, member) * (1.0 / HEAD_DIM)
        d = y - mu
        var = _head_sum2(d * d, member) * (1.0 / HEAD_DIM)
        yn = d * lax.rsqrt(var + GN_EPS) * lnw_ref[...] + lnb_ref[...]
        bonus = _head_sum2(r_ref[...] * k_ref[...] * rk_ref[...], member) * v_ref[...]
        yy = ((yn + bonus) * g_ref[...]).astype(BF16)
        o_sb = _mm(osb_ref[...].astype(BF16), wso_ref[...])
        o_rw = _mm(yy, wro_ref[...])
        merged = jax.nn.sigmoid(gs_ref[...]) * o_sb + jax.nn.sigmoid(gr_ref[...]) * o_rw
        m_ref[...] = merged.astype(BF16)

    o_ref[...] = h_ref[...] + _mm(m_ref[...], wout_ref[...])


def _merge_out(h, o_sb, y, r, k, v, g, p2, lnw, lnb, rk, member, wso, wro, wout, *, gs_blk, gr_blk, tm=256, tn=2048):
    n, d = h.shape
    width = o_sb.shape[1]
    tm, tn = min(tm, n), min(tn, d)
    tok = pl.BlockSpec((tm, width), lambda i, j: (i, 0))
    const = lambda shape: pl.BlockSpec(shape, lambda i, j: (0,) * len(shape), pipeline_mode=pl.Buffered(1))
    return pl.pallas_call(
        _out_body,
        out_shape=jax.ShapeDtypeStruct((n, d), F32),
        grid=(n // tm, d // tn),
        in_specs=[
            pl.BlockSpec((tm, tn), lambda i, j: (i, j)),
            tok, tok, tok, tok, tok, tok,
            pl.BlockSpec((tm, d), lambda i, j: (i, gs_blk)),
            pl.BlockSpec((tm, d), lambda i, j: (i, gr_blk)),
            const((1, width)), const((1, width)), const((1, width)),
            const((width, LANES)),
            const((width, d)), const((width, d)),
            pl.BlockSpec((d, tn), lambda i, j: (0, j), pipeline_mode=pl.Buffered(1)),
        ],
        out_specs=pl.BlockSpec((tm, tn), lambda i, j: (i, j)),
        scratch_shapes=[pltpu.VMEM((tm, d), BF16)],
        **_call_opts("merge_out", "parallel", "arbitrary"),
    )(h, o_sb, y, r, k, v, g, p2, p2, lnw, lnb, rk, member, wso, wro, wout)


def _layer(x, past_k, past_v, wkv0, shift0, w, *, tq, tc, scan_pairs):
    b, t, d = x.shape
    n = b * t
    width = w["w0"].shape[1]
    h1 = _ffn(x.reshape(n, d), w["ffn1_norm"], w["ffn1_wg"], w["ffn1_wu"], w["ffn1_wd"])
    p2 = _mix(h1, w["mix_norm"], w["w_in"], w["head_gain"], w["bd"], width=width)
    p3 = p2.reshape(b, t, -1)
    n_pairs = width // LANES
    if past_k is None:
        o_sb = _attention(p3, w["u2"], n_pairs=n_pairs, q_blk0=0, k_blk0=n_pairs, v_blk0=2 * n_pairs, tq=tq)
    else:
        o_sb = _attention_cached(p3, past_k, past_v, w["ue"], width=width)

    lora_w = w["mu_lora"].shape[1]
    lora_blk = (6 * width + 2 * d) // lora_w
    lora_cols = w["lora_cols"]
    s_rkv = shift0[:, :, :3 * width]
    s_lora = jnp.pad(shift0[:, :, 3 * width:], ((0, 0), (0, 0), (0, lora_w - lora_cols)))
    r, wl, k, v, av, bv, g = _rwkv_prep(
        p3, s_rkv, s_lora, w["mu_rkv"], w["mu_lora"], w["w0"], w["a0"], w["k_k"], w["k_a"],
        w["ww2"], w["wa2"], w["wg2"], w["member"], rkv_blk=1, lora_blk=lora_blk, tc=tc)

    s0 = wkv0.reshape(b, n_pairs, 2, HEAD_DIM, HEAD_DIM)
    z = jnp.zeros_like(s0[:, :, 0])
    s0_bd = jnp.concatenate([jnp.concatenate([s0[:, :, 0], z], axis=-1),
                             jnp.concatenate([z, s0[:, :, 1]], axis=-1)], axis=-2)
    y, s_bd = _rwkv_scan(r, wl, k, v, av, bv, s0_bd, w["tri"], tc=tc, pairs_per_step=scan_pairs)
    wkv = jnp.stack([s_bd[:, :, :HEAD_DIM, :HEAD_DIM], s_bd[:, :, HEAD_DIM:, HEAD_DIM:]], axis=2)
    wkv = wkv.reshape(b, 2 * n_pairs, HEAD_DIM, HEAD_DIM)

    flat = lambda a: a.reshape(n, width)
    gs_blk = (6 * width) // d
    h2 = _merge_out(h1, flat(o_sb), flat(y), flat(r), flat(k), flat(v), flat(g), p2,
                    w["ln_w"], w["ln_b"], w["r_k"], w["member"], w["sb_wo"], w["rw_wo"], w["w_out"],
                    gs_blk=gs_blk, gr_blk=gs_blk + 1)
    out = _ffn(h2, w["ffn2_norm"], w["ffn2_wg"], w["ffn2_wu"], w["ffn2_wd"])

    heads = width // HEAD_DIM
    k_new = p3[:, :, width:2 * width].reshape(b, t, heads, HEAD_DIM)
    v_new = p3[:, :, 2 * width:3 * width].reshape(b, t, heads, HEAD_DIM)
    shift = jnp.concatenate([p3[:, t - 1:, 3 * width:6 * width],
                             p3[:, t - 1:, 6 * width + 2 * d:6 * width + 2 * d + lora_cols]], axis=-1)
    return out.reshape(b, t, d), k_new, v_new, wkv, shift


def _layer_weights(l, ffn1_norm, ffn1_w_gate, ffn1_w_up, ffn1_w_down, mix_norm, w_in, sb_q_norm, sb_k_norm, sb_w_o,
                   rwkv_mu, rwkv_w0, rwkv_w_w2, rwkv_a0, rwkv_w_a2, rwkv_w_g2, rwkv_k_k, rwkv_k_a, rwkv_r_k,
                   rwkv_ln_w, rwkv_ln_b, rwkv_w_o, w_out, ffn2_norm, ffn2_w_gate, ffn2_w_up, ffn2_w_down):
    d = w_in.shape[1]
    width = rwkv_w0.shape[1]
    heads = width // HEAD_DIM
    n_decay, n_iclr, n_gate = rwkv_w_w2.shape[1], rwkv_w_a2.shape[1], rwkv_w_g2.shape[1]
    lora_cols = n_decay + n_iclr + n_gate
    lora_w = -(-lora_cols // 512) * 512
    row = lambda a: a.reshape(1, -1).astype(F32)
    wi = w_in[l]
    w_in_p = jnp.concatenate([
        wi[:, :6 * width], wi[:, 6 * width + lora_cols:], wi[:, 6 * width:6 * width + lora_cols],
        jnp.zeros((d, lora_w - lora_cols), wi.dtype)], axis=1).astype(BF16)
    total = w_in_p.shape[1]
    head_gain = jnp.concatenate([jnp.tile(sb_q_norm[l], heads), jnp.tile(sb_k_norm[l], heads),
                                 jnp.ones((total - 2 * width,), F32)]).reshape(1, total)
    mu = rwkv_mu[l]

    def lora_pad(wm, r0):
        return jnp.zeros((lora_w, width), F32).at[r0:r0 + wm.shape[0]].set(wm).astype(BF16)

    hid = jnp.arange(width) // HEAD_DIM
    bid = jnp.arange(512) // HEAD_DIM
    bd = (bid[:, None] == bid[None, :]).astype(BF16)
    member = (hid[:, None] == jnp.arange(LANES)[None, :]).astype(BF16)
    tk = ATT_TK
    ki = jnp.arange(tk)
    u2 = jnp.concatenate([(ki[None, :] > ki[:, None]).astype(BF16), jnp.ones((16, tk), BF16)], axis=0)
    ue = jnp.concatenate([(ki[:, None] > ki[None, :]).astype(BF16), jnp.ones((tk, LANES), BF16)], axis=1)
    ci = jnp.arange(RW_CHUNK)
    tri = (ci[:, None] >= ci[None, :]).astype(BF16)
    return {
        "ffn1_norm": row(ffn1_norm[l]), "ffn1_wg": ffn1_w_gate[l].astype(BF16), "ffn1_wu": ffn1_w_up[l].astype(BF16),
        "ffn1_wd": ffn1_w_down[l].astype(BF16),
        "ffn2_norm": row(ffn2_norm[l]), "ffn2_wg": ffn2_w_gate[l].astype(BF16), "ffn2_wu": ffn2_w_up[l].astype(BF16),
        "ffn2_wd": ffn2_w_down[l].astype(BF16),
        "mix_norm": row(mix_norm[l]), "w_in": w_in_p, "head_gain": head_gain, "bd": bd, "member": member, "u2": u2, "ue": ue, "tri": tri,
        "mu_rkv": row(mu[:3 * width]), "mu_lora": row(jnp.pad(mu[3 * width:], (0, lora_w - lora_cols))),
        "lora_cols": lora_cols,
        "w0": row(rwkv_w0[l]), "a0": row(rwkv_a0[l]), "k_k": row(rwkv_k_k[l]), "k_a": row(rwkv_k_a[l]),
        "ww2": lora_pad(rwkv_w_w2[l], 0), "wa2": lora_pad(rwkv_w_a2[l], n_decay),
        "wg2": lora_pad(rwkv_w_g2[l], n_decay + n_iclr),
        "ln_w": row(rwkv_ln_w[l]), "ln_b": row(rwkv_ln_b[l]), "r_k": row(rwkv_r_k[l]),
        "sb_wo": sb_w_o[l].astype(BF16), "rw_wo": rwkv_w_o[l].astype(BF16), "w_out": w_out[l].astype(BF16),
    }


def kernel(x_prompt, x_sample, cache_sb_k, cache_sb_v, state_rwkv_wkv, state_rwkv_shift, ffn1_norm, ffn1_w_gate, ffn1_w_up, ffn1_w_down, mix_norm, w_in, sb_q_norm, sb_k_norm, sb_w_o, rwkv_mu, rwkv_w0, rwkv_w_w2, rwkv_a0, rwkv_w_a2, rwkv_w_g2, rwkv_k_k, rwkv_k_a, rwkv_r_k, rwkv_ln_w, rwkv_ln_b, rwkv_w_o, w_out, ffn2_norm, ffn2_w_gate, ffn2_w_up, ffn2_w_down):
    depth = w_in.shape[0]
    yp, ys = x_prompt, x_sample
    bp = x_prompt.shape[0]
    width = rwkv_w0.shape[1]
    heads = width // HEAD_DIM
    rw_cols = state_rwkv_shift.shape[-1]
    outs = [[] for _ in range(8)]
    for l in range(depth):
        w = _layer_weights(l, ffn1_norm, ffn1_w_gate, ffn1_w_up, ffn1_w_down, mix_norm, w_in, sb_q_norm, sb_k_norm,
                           sb_w_o, rwkv_mu, rwkv_w0, rwkv_w_w2, rwkv_a0, rwkv_w_a2, rwkv_w_g2, rwkv_k_k, rwkv_k_a,
                           rwkv_r_k, rwkv_ln_w, rwkv_ln_b, rwkv_w_o, w_out, ffn2_norm, ffn2_w_gate, ffn2_w_up,
                           ffn2_w_down)
        wkv_zero = jnp.zeros((bp, heads, HEAD_DIM, HEAD_DIM), F32)
        shift_zero = jnp.zeros((bp, 1, rw_cols), F32)
        yp, kp, vp, wkvp, shp = _layer(yp, None, None, wkv_zero, shift_zero, w, tq=512, tc=256, scan_pairs=4)
        ys, kn, vn, wkvn, shn = _layer(ys, cache_sb_k[l], cache_sb_v[l], state_rwkv_wkv[l], state_rwkv_shift[l], w, tq=64, tc=64, scan_pairs=8)
        for lst, val in zip(outs, (kp, vp, wkvp, shp, kn, vn, wkvn, shn)):
            lst.append(val)
    return (yp, ys) + tuple(jnp.stack(o) for o in outs)
```
